```python
import math
import jax
import jax.numpy as jnp
from jax import lax
import numpy as np

D_MODEL = 1024
BATCH = 8
SEQ = 2048
DEPTH = 4

GRID_W = 64
CTX_LEN = 256
N_EVEN = (DEPTH + 1) // 2
N_ODD = DEPTH // 2
MLP_HIDDEN = 4 * D_MODEL
MIX_W = D_MODEL
GROUP_W = MIX_W // 2
EPS = 1e-6

S5_CH = GROUP_W
S5_GROUP = 16
S5_GROUPS = S5_CH // S5_GROUP
S5_STATE = 64
S5_MIN_DT = 1e-3
S5_MAX_DT = 1e-1

HY_CH = GROUP_W
HY_ORDER = 2
HY_EMB = 33
HY_BANDS = (HY_EMB - 1) // 2
HY_FFN = 64
SHORT_K = 3

RW_CH = GROUP_W
RW_HEAD = 64
RW_HEADS = RW_CH // RW_HEAD
RW_DECAY_LORA = 64
RW_A_LORA = 64
RW_G_LORA = 128
RW_LN_EPS = 64e-5
RW_IN = 3 * RW_CH + RW_DECAY_LORA + RW_A_LORA + RW_G_LORA

DA_HEADS = 4
DA_HEAD = 64
DA_V = 2 * DA_HEAD
DA_QK = DA_HEADS * 2 * DA_HEAD
DA_VW = DA_HEADS * DA_V
DA_SCALE = DA_HEAD ** -0.5
DA_SUBLN_EPS = 1e-5
Q_BLOCK = 128
ROPE_BASE = 10000.0
ROPE_FREQS = DA_HEAD // 4

EVEN_IN = S5_CH + (1 + HY_ORDER) * HY_CH
ODD_IN = RW_IN + 2 * DA_QK + DA_VW

F32 = jnp.float32

kernel_name = 'hybrid_s5_hyena_rwkv7_diffattn_prefix_dit'


def rmsnorm(x, g):
    xf = x.astype(F32)
    return xf * lax.rsqrt(jnp.mean(xf * xf, axis=-1, keepdims=True) + EPS) * g.astype(F32)


def modulate(h, shift, scale):
    return h * (1.0 + scale) + shift


def sq_relu_mlp(h, w1, w2):
    return jnp.square(jax.nn.relu(h @ w1)) @ w2


def short_conv(x, w, b=None):
    y = lax.conv_general_dilated(
        x.astype(F32), w.astype(F32)[:, None, :], window_strides=(1,),
        padding=((SHORT_K // 2, SHORT_K // 2),),
        dimension_numbers=('NWC', 'WIO', 'NWC'), feature_group_count=x.shape[-1])
    return y if b is None else y + b.astype(F32)


def s5_discretise(lam_re, lam_im, log_dt, b_re, b_im):
    lam_re = jnp.minimum(lam_re.astype(F32), -1e-4)
    lam_im = lam_im.astype(F32)
    dt = jnp.exp(log_dt.astype(F32))[:, None]
    mag = jnp.exp(lam_re * dt)
    lb_re = mag * jnp.cos(lam_im * dt)
    lb_im = mag * jnp.sin(lam_im * dt)
    den = lam_re * lam_re + lam_im * lam_im
    f_re = ((lb_re - 1.0) * lam_re + lb_im * lam_im) / den
    f_im = (lb_im * lam_re - (lb_re - 1.0) * lam_im) / den
    b_re = b_re.astype(F32)
    b_im = b_im.astype(F32)
    bb_re = f_re[..., None] * b_re - f_im[..., None] * b_im
    bb_im = f_re[..., None] * b_im + f_im[..., None] * b_re
    return lb_re, lb_im, bb_re, bb_im


def _complex_affine_combine(e1, e2):
    a1r, a1i, b1r, b1i = e1
    a2r, a2i, b2r, b2i = e2
    return (a2r * a1r - a2i * a1i, a2r * a1i + a2i * a1r,
            a2r * b1r - a2i * b1i + b2r, a2r * b1i + a2i * b1r + b2i)


def s5_scan(u, lb_re, lb_im, bb_re, bb_im, h0, reverse):
    bt, n = u.shape[:2]
    ug = u.astype(F32).reshape(bt, n, S5_GROUPS, S5_GROUP)
    bu_re = jnp.einsum('blgh,gph->blgp', ug, bb_re)
    bu_im = jnp.einsum('blgh,gph->blgp', ug, bb_im)
    shape = (1, n, S5_GROUPS, S5_STATE)
    a_re = jnp.broadcast_to(lb_re, shape)
    a_im = jnp.broadcast_to(lb_im, shape)
    acc_re, acc_im, h_re, h_im = lax.associative_scan(
        _complex_affine_combine, (a_re, a_im, bu_re, bu_im), reverse=reverse, axis=1)
    if h0 is not None:
        h0_re, h0_im = h0[0][:, None], h0[1][:, None]
        h_re = h_re + acc_re * h0_re - acc_im * h0_im
        h_im = h_im + acc_re * h0_im + acc_im * h0_re
    return h_re, h_im


def s5_readout(h_re, h_im, c_re, c_im):
    bt, n = h_re.shape[:2]
    y = (jnp.einsum('blgp,ghp->blgh', h_re, c_re.astype(F32))
         - jnp.einsum('blgp,ghp->blgh', h_im, c_im.astype(F32)))
    return y.reshape(bt, n, S5_CH)


def s5_mixer(u_ctx, u_lat, lam_re, lam_im, log_dt, b_re, b_im, c_re, c_im,
             d_skip, glu_w, glu_b, ctx_out):
    ys_c, ys_l = [], []
    for d in range(2):
        rev = d == 1
        lb_re, lb_im, bb_re, bb_im = s5_discretise(lam_re[d], lam_im[d], log_dt[d], b_re[d], b_im[d])
        hc_re, hc_im = s5_scan(u_ctx, lb_re, lb_im, bb_re, bb_im, None, rev)
        fin = 0 if rev else -1
        h0 = (hc_re[:, fin], hc_im[:, fin])
        hl_re, hl_im = s5_scan(u_lat, lb_re, lb_im, bb_re, bb_im, h0, rev)
        ys_l.append(s5_readout(hl_re, hl_im, c_re[d], c_im[d]))
        if ctx_out:
            ys_c.append(s5_readout(hc_re, hc_im, c_re[d], c_im[d]))

    def finish(ys, u):
        y = ys[0] + ys[1] + u * d_skip
        y = jax.nn.gelu(y, approximate=False)
        return y * jax.nn.sigmoid(y @ glu_w + glu_b)

    out_c = finish(ys_c, u_ctx) if ctx_out else None
    return out_c, finish(ys_l, u_lat)


def hyena_two_sided_filters(n, w1, b1, w2, b2, w3, freq, log_decay):
    t = jnp.linspace(0.0, 1.0, n, dtype=F32)[:, None]
    w = 2.0 * math.pi * jnp.arange(n, dtype=F32)[:, None] / n
    bands = jnp.linspace(1e-4, HY_BANDS - 1, HY_BANDS, dtype=F32)[None, :]
    z = jnp.concatenate([t, jnp.cos(bands * w), -jnp.sin(bands * w)], axis=-1)
    freq = freq.astype(F32)
    hid = jnp.sin(freq[0] * (z @ w1.astype(F32) + b1.astype(F32)))
    hid = jnp.sin(freq[1] * (hid @ w2.astype(F32) + b2.astype(F32)))
    h = (hid @ w3.astype(F32)).reshape(n, 2, HY_ORDER, HY_CH)
    h = h * jnp.exp(-t[:, :, None, None] * jnp.exp(log_decay.astype(F32)))
    h_fwd, h_bwd = h[:, 0], h[:, 1]
    return jnp.concatenate([h_fwd, jnp.zeros_like(h_fwd[:1]), h_bwd[:0:-1]], axis=0)


def fft_long_conv(u, filt, bias):
    n = u.shape[1]
    u_f = jnp.fft.rfft(u.astype(F32), n=2 * n, axis=1)
    k_f = jnp.fft.rfft(filt, n=2 * n, axis=0)
    y = jnp.fft.irfft(u_f * k_f[None], n=2 * n, axis=1)[:, :n]
    return y + u * bias.astype(F32)


def hyena_mixer(p_ctx, p_lat, conv_w, conv_b, f_w1, f_b1, f_w2, f_b2, f_w3, f_freq,
                log_decay, bias, ctx_out):
    def run(p):
        n = p.shape[1]
        streams = jnp.split(short_conv(p, conv_w, conv_b), 1 + HY_ORDER, axis=-1)
        filt = hyena_two_sided_filters(n, f_w1, f_b1, f_w2, f_b2, f_w3, f_freq, log_decay)
        z = streams[0]
        for o in range(HY_ORDER):
            z = streams[1 + o] * fft_long_conv(z, filt[:, o], bias[o])
        return z

    out_c = run(p_ctx) if ctx_out else None
    return out_c, run(p_lat)


def rwkv_scan(s0, decay, k, v, kk, a, r, reverse):
    seq = [decay, k, v, kk, kk * a] + ([] if r is None else [r])
    xs = tuple(jnp.moveaxis(t.astype(F32), 1, 0) for t in seq)

    def step(S, inp):
        w_t, k_t, v_t, kk_t, b_t = inp[:5]
        sa = jnp.einsum('bhij,bhj->bhi', S, kk_t)
        S = (S * w_t[:, :, None, :] - sa[..., None] * b_t[:, :, None, :]
             + v_t[..., None] * k_t[:, :, None, :])
        y = None if r is None else jnp.einsum('bhij,bhj->bhi', S, inp[5])
        return S, y

    s_fin, ys = lax.scan(step, s0, xs, reverse=reverse)
    y = None if r is None else jnp.moveaxis(ys, 0, 1)
    return y, s_fin


def rwkv7_mixer(p_ctx, p_lat, conv_w, w0, w_up, a0, a_up, g_up, k_k, k_a, r_k,
                ln_g, ln_b, ctx_out):
    def heads(t):
        return t.reshape(t.shape[0], t.shape[1], RW_HEADS, RW_HEAD)

    def prep(p):
        r, k, v = jnp.split(short_conv(p[..., :3 * RW_CH], conv_w), 3, axis=-1)
        o = 3 * RW_CH
        w_lo = p[..., o:o + RW_DECAY_LORA]
        o += RW_DECAY_LORA
        a_lo = p[..., o:o + RW_A_LORA]
        o += RW_A_LORA
        g_lo = p[..., o:o + RW_G_LORA]
        return r, k, v, w_lo, a_lo, g_lo

    def direction(k, w_lo, a_lo, d):
        w = -jax.nn.softplus(-(w0[d] + jnp.tanh(w_lo) @ w_up[d])) - 0.5
        a = jax.nn.sigmoid(a0[d] + a_lo @ a_up[d])
        kk = heads(k * k_k)
        kk = kk / jnp.maximum(jnp.sqrt(jnp.sum(kk * kk, axis=-1, keepdims=True)), 1e-12)
        k_eff = heads(k * (1.0 + (a - 1.0) * k_a))
        return heads(jnp.exp(-jnp.exp(w))), k_eff, kk, heads(a)

    def post(y, r, k_sum, v, g_lo):
        bt, n = y.shape[:2]
        mu = jnp.mean(y, axis=-1, keepdims=True)
        var = jnp.mean(jnp.square(y - mu), axis=-1, keepdims=True)
        yn = ((y - mu) * lax.rsqrt(var + RW_LN_EPS)).reshape(bt, n, RW_CH) * ln_g + ln_b
        bonus = jnp.sum(heads(r) * k_sum * r_k, axis=-1, keepdims=True) * heads(v)
        gate = jax.nn.sigmoid(g_lo) @ g_up
        return (yn + bonus.reshape(bt, n, RW_CH)) * gate

    r_c, k_c, v_c, w_c, al_c, g_c = prep(p_ctx)
    r_l, k_l, v_l, w_l, al_l, g_l = prep(p_lat)
    s_zero = jnp.zeros((p_lat.shape[0], RW_HEADS, RW_HEAD, RW_HEAD), F32)
    y_cs, ks_c, y_ls, ks_l = [], [], [], []
    for d in range(2):
        rev = d == 1
        dec, ke, kk, aa = direction(k_c, w_c, al_c, d)
        y_cd, s_ctx = rwkv_scan(s_zero, dec, ke, heads(v_c), kk, aa,
                                heads(r_c) if ctx_out else None, rev)
        y_cs.append(y_cd)
        ks_c.append(ke)
        dec, ke, kk, aa = direction(k_l, w_l, al_l, d)
        y_ld, _ = rwkv_scan(s_ctx, dec, ke, heads(v_l), kk, aa, heads(r_l), rev)
        y_ls.append(y_ld)
        ks_l.append(ke)
    out_l = post(y_ls[0] + y_ls[1], r_l, ks_l[0] + ks_l[1], v_l, g_l)
    out_c = post(y_cs[0] + y_cs[1], r_c, ks_c[0] + ks_c[1], v_c, g_c) if ctx_out else None
    return out_c, out_l


def axial_rope_tables(n_lat):
    rows = n_lat // GRID_W
    row = jnp.repeat(jnp.arange(rows, dtype=F32), GRID_W)
    col = jnp.tile(jnp.arange(GRID_W, dtype=F32), rows)
    inv = ROPE_BASE ** (-jnp.arange(ROPE_FREQS, dtype=F32) / ROPE_FREQS)
    ang = jnp.stack([row[:, None] * inv, col[:, None] * inv], axis=1)
    return jnp.cos(ang), jnp.sin(ang)


def apply_axial_rope(x, cos, sin):
    shp = x.shape
    xr = x.reshape(shp[:-1] + (2, 2, ROPE_FREQS))
    x1, x2 = xr[..., 0, :], xr[..., 1, :]
    cos = cos[None, :, None, None]
    sin = sin[None, :, None, None]
    out = jnp.stack([x1 * cos - x2 * sin, x1 * sin + x2 * cos], axis=-2)
    return out.reshape(shp)


def diff_attn_block(q, k, v, lam):
    s = jnp.einsum('bqhmd,bshmd->bhmqs', q.astype(F32), k.astype(F32)) * DA_SCALE
    p = jax.nn.softmax(s, axis=-1)
    w = p[:, :, 0] - lam * p[:, :, 1]
    return jnp.einsum('bhqs,bshd->bqhd', w, v.astype(F32))


def diff_attention(p_ctx, p_lat, lam_p, subln_g, lam_init, rope, ctx_out):
    def split(p):
        bt, n = p.shape[:2]
        q = p[..., :DA_QK].reshape(bt, n, DA_HEADS, 2, DA_HEAD)
        k = p[..., DA_QK:2 * DA_QK].reshape(bt, n, DA_HEADS, 2, DA_HEAD)
        v = p[..., 2 * DA_QK:].reshape(bt, n, DA_HEADS, DA_V)
        return q, k, v

    def post(o):
        bt, n = o.shape[:2]
        on = o * lax.rsqrt(jnp.mean(o * o, axis=-1, keepdims=True) + DA_SUBLN_EPS)
        return (on * subln_g * (1.0 - lam_init)).reshape(bt, n, DA_VW)

    lp = lam_p.astype(F32)
    lam = jnp.exp(jnp.sum(lp[0] * lp[1])) - jnp.exp(jnp.sum(lp[2] * lp[3])) + lam_init
    q_c, k_c, v_c = split(p_ctx)
    q_l, k_l, v_l = split(p_lat)
    q_l = apply_axial_rope(q_l, *rope)
    k_l = apply_axial_rope(k_l, *rope)
    k_all = jnp.concatenate([k_l, k_c], axis=1)
    v_all = jnp.concatenate([v_l, v_c], axis=1)
    bt, n = q_l.shape[:2]
    nb = n // Q_BLOCK
    qb = jnp.moveaxis(q_l.reshape((bt, nb, Q_BLOCK) + q_l.shape[2:]), 1, 0)
    ob = lax.map(lambda qq: diff_attn_block(qq, k_all, v_all, lam), qb)
    o_l = jnp.moveaxis(ob, 0, 1).reshape(bt, n, DA_HEADS, DA_V)
    out_c = post(diff_attn_block(q_c, k_c, v_c, lam)) if ctx_out else None
    return out_c, post(o_l)


def setup_inputs(seed: int = 0) -> dict:
    key = jax.random.key(seed)
    ks = jax.random.split(key, 64)
    count = [0]

    def nxt():
        count[0] += 1
        return ks[count[0] - 1]

    def nrm(shape, std):
        return std * jax.random.normal(nxt(), shape, F32)

    def unif(shape, lo, hi):
        return jax.random.uniform(nxt(), shape, F32, lo, hi)

    D = D_MODEL
    E, O = N_EVEN, N_ODD
    G, P, H = S5_GROUPS, S5_STATE, S5_GROUP
    a_imag = math.pi * jnp.arange(P, dtype=F32)
    return {
        'x': nrm((BATCH, SEQ, D), 1.0),
        'c': nrm((BATCH, D), 1.0),
        'ctx': nrm((BATCH, CTX_LEN, D), 1.0),
        'c_ctx': nrm((D,), 1.0),
        'ada_w': nrm((DEPTH, D, 6 * D), 0.3 * D ** -0.5),
        'ada_b': nrm((DEPTH, 6 * D), 0.02),
        'norm1_g': 1.0 + nrm((DEPTH, D), 0.02),
        'norm2_g': 1.0 + nrm((DEPTH, D), 0.02),
        'mlp_w1': nrm((DEPTH, D, MLP_HIDDEN), D ** -0.5),
        'mlp_w2': nrm((DEPTH, MLP_HIDDEN, D), MLP_HIDDEN ** -0.5),
        'final_g': 1.0 + nrm((D,), 0.02),
        'ev_w_in': nrm((E, D, EVEN_IN), D ** -0.5),
        'ev_w_out': nrm((E, MIX_W, D), MIX_W ** -0.5),
        's5_lam_re': -0.5 + nrm((E, 2, G, P), 0.01),
        's5_lam_im': a_imag + nrm((E, 2, G, P), 0.01),
        's5_log_dt': unif((E, 2, G), math.log(S5_MIN_DT), math.log(S5_MAX_DT)),
        's5_b_re': nrm((E, 2, G, P, H), (2 * H) ** -0.5),
        's5_b_im': nrm((E, 2, G, P, H), (2 * H) ** -0.5),
        's5_c_re': nrm((E, 2, G, H, P), (2 * P) ** -0.5),
        's5_c_im': nrm((E, 2, G, H, P), (2 * P) ** -0.5),
        's5_d': nrm((E, S5_CH), 1.0),
        's5_glu_w': nrm((E, S5_CH, S5_CH), S5_CH ** -0.5),
        's5_glu_b': nrm((E, S5_CH), 0.02),
        'hy_conv_w': nrm((E, SHORT_K, (1 + HY_ORDER) * HY_CH), SHORT_K ** -0.5),
        'hy_conv_b': nrm((E, (1 + HY_ORDER) * HY_CH), 0.02),
        'hy_f_w1': nrm((E, HY_EMB, HY_FFN), HY_EMB ** -0.5),
        'hy_f_b1': nrm((E, HY_FFN), 0.1),
        'hy_f_w2': nrm((E, HY_FFN, HY_FFN), HY_FFN ** -0.5),
        'hy_f_b2': nrm((E, HY_FFN), 0.1),
        'hy_f_w3': nrm((E, HY_FFN, 2 * HY_ORDER * HY_CH), 0.01),
        'hy_f_freq': 1.0 + nrm((E, 2, HY_FFN), 0.05),
        'hy_log_decay': unif((E, 2, HY_ORDER, HY_CH), math.log(3.0), math.log(15.0)),
        'hy_bias': nrm((E, HY_ORDER, HY_CH), 0.1),
        'od_w_in': nrm((O, D, ODD_IN), D ** -0.5),
        'od_w_out': nrm((O, MIX_W, D), MIX_W ** -0.5),
        'rw_conv_w': nrm((O, SHORT_K, 3 * RW_CH), SHORT_K ** -0.5),
        'rw_w0': unif((O, 2, RW_CH), -6.0, -1.0),
        'rw_w_up': nrm((O, 2, RW_DECAY_LORA, RW_CH), 0.1 * RW_DECAY_LORA ** -0.5),
        'rw_a0': nrm((O, 2, RW_CH), 0.5),
        'rw_a_up': nrm((O, 2, RW_A_LORA, RW_CH), 0.1 * RW_A_LORA ** -0.5),
        'rw_g_up': nrm((O, RW_G_LORA, RW_CH), RW_G_LORA ** -0.5),
        'rw_k_k': 0.85 + nrm((O, RW_CH), 0.02),
        'rw_k_a': 1.0 + nrm((O, RW_CH), 0.02),
        'rw_r_k': nrm((O, RW_HEADS, RW_HEAD), 0.1),
        'rw_ln_g': 1.0 + nrm((O, RW_CH), 0.02),
        'rw_ln_b': nrm((O, RW_CH), 0.02),
        'da_lam': nrm((O, 4, DA_HEAD), 0.1),
        'da_subln_g': 1.0 + nrm((O, DA_V), 0.02),
    }


def reference(x, c, ctx, c_ctx, ada_w, ada_b, norm1_g, norm2_g, mlp_w1, mlp_w2, final_g,
              ev_w_in, ev_w_out, s5_lam_re, s5_lam_im, s5_log_dt, s5_b_re, s5_b_im,
              s5_c_re, s5_c_im, s5_d, s5_glu_w, s5_glu_b, hy_conv_w, hy_conv_b,
              hy_f_w1, hy_f_b1, hy_f_w2, hy_f_b2, hy_f_w3, hy_f_freq, hy_log_decay, hy_bias,
              od_w_in, od_w_out, rw_conv_w, rw_w0, rw_w_up, rw_a0, rw_a_up, rw_g_up,
              rw_k_k, rw_k_a, rw_r_k, rw_ln_g, rw_ln_b, da_lam, da_subln_g):
    lat = x.astype(F32)
    cx = ctx.astype(F32)
    silu_c = jax.nn.silu(c.astype(F32))
    silu_cc = jax.nn.silu(c_ctx.astype(F32))
    rope = axial_rope_tables(lat.shape[1])
    for l in range(DEPTH):
        keep_ctx = l < DEPTH - 1
        i = l // 2
        mods_l = jnp.split((silu_c @ ada_w[l] + ada_b[l])[:, None, :], 6, axis=-1)
        mods_c = jnp.split(silu_cc @ ada_w[l] + ada_b[l], 6, axis=-1)
        h_l = modulate(rmsnorm(lat, norm1_g[l]), mods_l[0], mods_l[1])
        h_c = modulate(rmsnorm(cx, norm1_g[l]), mods_c[0], mods_c[1])
        if l % 2 == 0:
            p_l = h_l @ ev_w_in[i]
            p_c = h_c @ ev_w_in[i]
            a_c, a_l = s5_mixer(p_c[..., :S5_CH], p_l[..., :S5_CH], s5_lam_re[i], s5_lam_im[i],
                                s5_log_dt[i], s5_b_re[i], s5_b_im[i], s5_c_re[i], s5_c_im[i],
                                s5_d[i], s5_glu_w[i], s5_glu_b[i], keep_ctx)
            b_c, b_l = hyena_mixer(p_c[..., S5_CH:], p_l[..., S5_CH:], hy_conv_w[i], hy_conv_b[i],
                                   hy_f_w1[i], hy_f_b1[i], hy_f_w2[i], hy_f_b2[i], hy_f_w3[i],
                                   hy_f_freq[i], hy_log_decay[i], hy_bias[i], keep_ctx)
            w_out = ev_w_out[i]
        else:
            p_l = h_l @ od_w_in[i]
            p_c = h_c @ od_w_in[i]
            a_c, a_l = rwkv7_mixer(p_c[..., :RW_IN], p_l[..., :RW_IN], rw_conv_w[i], rw_w0[i],
                                   rw_w_up[i], rw_a0[i], rw_a_up[i], rw_g_up[i], rw_k_k[i],
                                   rw_k_a[i], rw_r_k[i], rw_ln_g[i], rw_ln_b[i], keep_ctx)
            lam_init = 0.8 - 0.6 * math.exp(-0.3 * l)
            b_c, b_l = diff_attention(p_c[..., RW_IN:], p_l[..., RW_IN:], da_lam[i], da_subln_g[i],
                                      lam_init, rope, keep_ctx)
            w_out = od_w_out[i]
        lat = lat + mods_l[2] * (jnp.concatenate([a_l, b_l], axis=-1) @ w_out)
        lat = lat + mods_l[5] * sq_relu_mlp(
            modulate(rmsnorm(lat, norm2_g[l]), mods_l[3], mods_l[4]), mlp_w1[l], mlp_w2[l])
        if keep_ctx:
            cx = cx + mods_c[2] * (jnp.concatenate([a_c, b_c], axis=-1) @ w_out)
            cx = cx + mods_c[5] * sq_relu_mlp(
                modulate(rmsnorm(cx, norm2_g[l]), mods_c[3], mods_c[4]), mlp_w1[l], mlp_w2[l])
    return rmsnorm(lat, final_g)
```

```python
import functools
import math

import jax
import jax.numpy as jnp
from jax import lax
from jax.experimental import pallas as pl
from jax.experimental.pallas import tpu as pltpu

F32 = jnp.float32
BF16 = jnp.bfloat16
HIGHEST = lax.Precision.HIGHEST

D_MODEL = 1024
DEPTH = 4
GRID_W = 64
MLP_HIDDEN = 4 * D_MODEL
GROUP_W = D_MODEL // 2
EPS = 1e-6

S5_GROUP = 16
S5_GROUPS = GROUP_W // S5_GROUP
S5_STATE = 64
S5_W = S5_GROUPS * S5_STATE

HY_ORDER = 2
HY_EMB = 33
HY_BANDS = (HY_EMB - 1) // 2
SHORT_K = 3

RW_HEAD = 64
RW_HEADS = GROUP_W // RW_HEAD
RW_LORA_W = 256
RW_LN_EPS = 64e-5
RW_IN = 3 * GROUP_W + RW_LORA_W

DA_HEADS = 4
DA_HEAD = 64
DA_V = 2 * DA_HEAD
DA_SCALE = DA_HEAD ** -0.5
DA_SUBLN_EPS = 1e-5
ROPE_BASE = 10000.0
ROPE_FREQS = DA_HEAD // 4

LANE = 128
SUBLANE = 8
TOKEN_TILE = 256
VMEM_LIMIT = 48 * 1024 * 1024


def _params(*sem):
    return pltpu.CompilerParams(dimension_semantics=sem, vmem_limit_bytes=VMEM_LIMIT)


def _split_bf16(x):
    hi = x.astype(BF16)
    lo = (x - hi.astype(F32)).astype(BF16)
    return hi, lo


def _dot(a, b):
    return jnp.dot(a, b, preferred_element_type=F32)


def _ada_kernel(c_ref, w_ref, b_ref, o_ref):
    c = c_ref[...]
    s = c * jax.nn.sigmoid(c)
    o_ref[0] = jnp.dot(s, w_ref[0], preferred_element_type=F32, precision=HIGHEST) + b_ref[0]


def ada_mods(c_rows, ada_w, ada_b):
    depth, d, n = ada_w.shape
    tn = 1024
    return pl.pallas_call(
        _ada_kernel,
        grid=(depth, n // tn),
        in_specs=[pl.BlockSpec((16, d), lambda l, j: (0, 0)),
                  pl.BlockSpec((1, d, tn), lambda l, j: (l, 0, j)),
                  pl.BlockSpec((1, 1, tn), lambda l, j: (l, 0, j))],
        out_specs=pl.BlockSpec((1, 16, tn), lambda l, j: (l, 0, j)),
        out_shape=jax.ShapeDtypeStruct((depth, 16, n), F32),
        compiler_params=_params("arbitrary", "arbitrary"),
        name="ada_mods",
    )(c_rows, ada_w, ada_b.reshape(depth, 1, n))


def _norm_mod(x, g, ml, mc, is_ctx, k):
    ms = jnp.mean(x * x, axis=-1, keepdims=True)
    xn = x * lax.rsqrt(ms + EPS) * g
    m = jnp.where(is_ctx, mc, ml)
    return xn * (1.0 + m[k + 1:k + 2]) + m[k:k + 1]


def _inproj_kernel(x_ref, g_ref, ml_ref, mc_ref, w_ref, o_ref, h_ref, *, n_lat_tiles, tmaj):
    i = pl.program_id(1)
    j = pl.program_id(2)

    @pl.when(j == 0)
    def _():
        h = _norm_mod(x_ref[0], g_ref[...], ml_ref[0], mc_ref[0], i >= n_lat_tiles, 0)
        h_ref[...] = h.astype(BF16)

    r = _dot(h_ref[...], w_ref[...])
    if tmaj:
        o_ref[...] = r
    else:
        o_ref[0] = r


def in_proj(x, g, mods, w, n_lat, tn, tmaj):
    b, t, d = x.shape
    n = w.shape[1]
    tm = TOKEN_TILE
    nj = n // tn
    if tmaj:
        out_shape = jax.ShapeDtypeStruct((t, b * n), F32)
        out_spec = pl.BlockSpec((tm, tn), lambda bb, i, j: (i, bb * nj + j))
    else:
        out_shape = jax.ShapeDtypeStruct((b, t, n), F32)
        out_spec = pl.BlockSpec((1, tm, tn), lambda bb, i, j: (bb, i, j))
    return pl.pallas_call(
        functools.partial(_inproj_kernel, n_lat_tiles=n_lat // tm, tmaj=tmaj),
        grid=(b, t // tm, nj),
        in_specs=[pl.BlockSpec((1, tm, d), lambda bb, i, j: (bb, i, 0)),
                  pl.BlockSpec((1, d), lambda bb, i, j: (0, 0)),
                  pl.BlockSpec((1, 6, d), lambda bb, i, j: (bb, 0, 0)),
                  pl.BlockSpec((1, 6, d), lambda bb, i, j: (b, 0, 0)),
                  pl.BlockSpec((d, tn), lambda bb, i, j: (0, j))],
        out_specs=out_spec,
        out_shape=out_shape,
        scratch_shapes=[pltpu.VMEM((tm, d), BF16)],
        compiler_params=_params("arbitrary", "arbitrary", "arbitrary"),
        name="in_proj",
    )(x, g.reshape(1, d), mods, mods, w)


def _outproj_kernel(x_ref, a_ref, b_ref, wa_ref, wb_ref, ml_ref, mc_ref, o_ref, *, n_lat_tiles, a_tmaj):
    i = pl.program_id(1)
    a = a_ref[...] if a_tmaj else a_ref[0]
    y = _dot(a.astype(BF16), wa_ref[...]) + _dot(b_ref[0].astype(BF16), wb_ref[...])
    m = jnp.where(i >= n_lat_tiles, mc_ref[0], ml_ref[0])
    o_ref[0] = x_ref[0] + m[2:3] * y


def out_proj(x, a, bm, w_out, mods, n_lat, n_rows, a_tmaj):
    b, t, d = x.shape
    wd = bm.shape[-1]
    tm = TOKEN_TILE
    if a_tmaj:
        a_spec = pl.BlockSpec((tm, wd), lambda bb, i: (i, bb))
    else:
        a_spec = pl.BlockSpec((1, tm, wd), lambda bb, i: (bb, i, 0))
    return pl.pallas_call(
        functools.partial(_outproj_kernel, n_lat_tiles=n_lat // tm, a_tmaj=a_tmaj),
        grid=(b, n_rows // tm),
        in_specs=[pl.BlockSpec((1, tm, d), lambda bb, i: (bb, i, 0)),
                  a_spec,
                  pl.BlockSpec((1, tm, wd), lambda bb, i: (bb, i, 0)),
                  pl.BlockSpec((wd, d), lambda bb, i: (0, 0)),
                  pl.BlockSpec((wd, d), lambda bb, i: (1, 0)),
                  pl.BlockSpec((1, 6, d), lambda bb, i: (bb, 0, 0)),
                  pl.BlockSpec((1, 6, d), lambda bb, i: (b, 0, 0))],
        out_specs=pl.BlockSpec((1, tm, d), lambda bb, i: (bb, i, 0)),
        out_shape=jax.ShapeDtypeStruct((b, n_rows, d), F32),
        compiler_params=_params("arbitrary", "arbitrary"),
        name="out_proj",
    )(x, a, bm, w_out, w_out, mods, mods)


def _mlp_kernel(x_ref, g_ref, ml_ref, mc_ref, w1_ref, w2_ref, fg_ref, o_ref, h_ref, acc_ref,
                *, n_lat_tiles, final):
    i = pl.program_id(1)
    k = pl.program_id(2)
    is_ctx = i >= n_lat_tiles

    @pl.when(k == 0)
    def _():
        h = _norm_mod(x_ref[0], g_ref[...], ml_ref[0], mc_ref[0], is_ctx, 3)
        h_ref[...] = h.astype(BF16)
        acc_ref[...] = jnp.zeros_like(acc_ref)

    a = jnp.maximum(_dot(h_ref[...], w1_ref[...]), 0.0)
    acc_ref[...] += _dot((a * a).astype(BF16), w2_ref[...])

    @pl.when(k == pl.num_programs(2) - 1)
    def _():
        m = jnp.where(is_ctx, mc_ref[0], ml_ref[0])
        y = x_ref[0] + m[5:6] * acc_ref[...]
        if final:
            ms = jnp.mean(y * y, axis=-1, keepdims=True)
            y = y * lax.rsqrt(ms + EPS) * fg_ref[...]
        o_ref[0] = y


def mlp(x, g, mods, w1, w2, final_g, n_lat, n_rows, final):
    b, t, d = x.shape
    hid = w1.shape[1]
    tm = TOKEN_TILE
    hk = 1024
    return pl.pallas_call(
        functools.partial(_mlp_kernel, n_lat_tiles=n_lat // tm, final=final),
        grid=(b, n_rows // tm, hid // hk),
        in_specs=[pl.BlockSpec((1, tm, d), lambda bb, i, k: (bb, i, 0)),
                  pl.BlockSpec((1, d), lambda bb, i, k: (0, 0)),
                  pl.BlockSpec((1, 6, d), lambda bb, i, k: (bb, 0, 0)),
                  pl.BlockSpec((1, 6, d), lambda bb, i, k: (b, 0, 0)),
                  pl.BlockSpec((d, hk), lambda bb, i, k: (0, k)),
                  pl.BlockSpec((hk, d), lambda bb, i, k: (k, 0)),
                  pl.BlockSpec((1, d), lambda bb, i, k: (0, 0))],
        out_specs=pl.BlockSpec((1, tm, d), lambda bb, i, k: (bb, i, 0)),
        out_shape=jax.ShapeDtypeStruct((b, n_rows, d), F32),
        scratch_shapes=[pltpu.VMEM((tm, d), BF16), pltpu.VMEM((tm, d), F32)],
        compiler_params=_params("arbitrary", "arbitrary", "arbitrary"),
        name="mlp",
    )(x, g.reshape(1, d), mods, mods, w1, w2, final_g.reshape(1, d))


def _shortconv_kernel(x_ref, w_ref, b_ref, ol_ref, oc_ref, *, n_lat):
    x = x_ref[0]
    t = x.shape[0]
    row = lax.broadcasted_iota(jnp.int32, (t, 1), 0)
    first = (row == 0) | (row == n_lat)
    last = (row == n_lat - 1) | (row == t - 1)
    xm = jnp.where(first, 0.0, pltpu.roll(x, 1, 0))
    xp = jnp.where(last, 0.0, pltpu.roll(x, t - 1, 0))
    w = w_ref[...]
    y = xm * w[0:1] + x * w[1:2] + xp * w[2:3] + b_ref[...]
    ol_ref[0] = y[:n_lat]
    oc_ref[0] = y[n_lat:]


def short_conv(p, col0, w, bias, n_lat):
    b, t, _ = p.shape
    c = w.shape[1]
    cb = 512
    off = col0 // cb
    return pl.pallas_call(
        functools.partial(_shortconv_kernel, n_lat=n_lat),
        grid=(b, c // cb),
        in_specs=[pl.BlockSpec((1, t, cb), lambda bb, j: (bb, 0, off + j)),
                  pl.BlockSpec((SHORT_K, cb), lambda bb, j: (0, j)),
                  pl.BlockSpec((1, cb), lambda bb, j: (0, j))],
        out_specs=[pl.BlockSpec((1, n_lat, cb), lambda bb, j: (bb, 0, j)),
                   pl.BlockSpec((1, t - n_lat, cb), lambda bb, j: (bb, 0, j))],
        out_shape=[jax.ShapeDtypeStruct((b, n_lat, c), F32),
                   jax.ShapeDtypeStruct((b, t - n_lat, c), F32)],
        compiler_params=_params("arbitrary", "arbitrary"),
        name="short_conv",
    )(p, w, bias.reshape(1, c))


S5_KB = LANE
S5_NB = GROUP_W // S5_KB
S5_SB = S5_W // S5_NB


def _s5_kernel(*refs, tc, nb, reverse, finish):
    if finish:
        (u_ref, wbr_ref, wbi_ref, lr_ref, li_ref, wcr_ref, wci_ref, yf_ref, d_ref, gw_ref, gb_ref,
         o_ref, hr_s, hi_s, sr_s, si_s) = refs
    else:
        (u_ref, wbr_ref, wbi_ref, lr_ref, li_ref, wcr_ref, wci_ref,
         o_ref, hr_s, hi_s, sr_s, si_s) = refs

    @pl.when(pl.program_id(0) == 0)
    def _():
        sr_s[...] = jnp.zeros_like(sr_s)
        si_s[...] = jnp.zeros_like(si_s)

    u = u_ref[...]
    ub = u.astype(BF16)
    for j in range(S5_NB):
        uj = ub[:, j * S5_KB:(j + 1) * S5_KB]
        hr_s[:, j * S5_SB:(j + 1) * S5_SB] = _dot(uj, wbr_ref[j])
        hi_s[:, j * S5_SB:(j + 1) * S5_SB] = _dot(uj, wbi_ref[j])

    for s in range(S5_NB):
        sl = slice(s * S5_SB, (s + 1) * S5_SB)
        lr = jnp.broadcast_to(lr_ref[:, sl], (nb, S5_SB))
        li = jnp.broadcast_to(li_ref[:, sl], (nb, S5_SB))

        def body(k, carry, sl=sl, lr=lr, li=li):
            hr, hi = carry
            t = (tc - 1 - k) if reverse else k
            r0 = pl.multiple_of(t * nb, nb)
            nr = lr * hr - li * hi + hr_s[pl.ds(r0, nb), sl]
            ni = lr * hi + li * hr + hi_s[pl.ds(r0, nb), sl]
            hr_s[pl.ds(r0, nb), sl] = nr
            hi_s[pl.ds(r0, nb), sl] = ni
            return nr, ni

        hr, hi = lax.fori_loop(0, tc, body, (sr_s[:, sl], si_s[:, sl]), unroll=4)
        sr_s[:, sl] = hr
        si_s[:, sl] = hi

    for j in range(S5_NB):
        sl = slice(j * S5_SB, (j + 1) * S5_SB)
        co = slice(j * S5_KB, (j + 1) * S5_KB)
        y = _dot(hr_s[:, sl].astype(BF16), wcr_ref[j]) + _dot(hi_s[:, sl].astype(BF16), wci_ref[j])
        if finish:
            o_ref[:, co] = y + yf_ref[:, co] + u[:, co] * d_ref[:, co]
        else:
            o_ref[:, co] = y

    if finish:
        y = o_ref[...]
        y = 0.5 * y * (1.0 + lax.erf(y * (2.0 ** -0.5)))
        z = _dot(y.astype(BF16), gw_ref[...]) + gb_ref[...]
        o_ref[...] = y * jax.nn.sigmoid(z)


def s5_scan_call(u2, tabs, n_lat_steps, n_steps, nb, reverse, extra):
    tc = 32
    rows = tc * nb
    n_chunks = n_steps // tc
    n_lat_chunks = n_lat_steps // tc
    n_ctx_chunks = n_chunks - n_lat_chunks
    if reverse:
        def cmap(c):
            return (n_chunks - 1 - c, 0)
    else:
        def cmap(c):
            return (jnp.where(c < n_ctx_chunks, n_lat_chunks + c, c - n_ctx_chunks), 0)
    wbr, wbi, lr, li, wcr, wci = tabs
    full3 = lambda a: pl.BlockSpec(a.shape, lambda c: (0, 0, 0))
    full2 = lambda a: pl.BlockSpec(a.shape, lambda c: (0, 0))
    in_specs = [pl.BlockSpec((rows, GROUP_W), cmap), full3(wbr), full3(wbi), full2(lr), full2(li),
                full3(wcr), full3(wci)]
    args = [u2, wbr, wbi, lr, li, wcr, wci]
    finish = extra is not None
    if finish:
        yf, dsk, gw, gb = extra
        in_specs += [pl.BlockSpec((rows, GROUP_W), cmap), full2(dsk), full2(gw), full2(gb)]
        args += [yf, dsk, gw, gb]
    return pl.pallas_call(
        functools.partial(_s5_kernel, tc=tc, nb=nb, reverse=reverse, finish=finish),
        grid=(n_chunks,),
        in_specs=in_specs,
        out_specs=pl.BlockSpec((rows, GROUP_W), cmap),
        out_shape=jax.ShapeDtypeStruct(u2.shape, F32),
        scratch_shapes=[pltpu.VMEM((rows, S5_W), F32), pltpu.VMEM((rows, S5_W), F32),
                        pltpu.VMEM((nb, S5_W), F32), pltpu.VMEM((nb, S5_W), F32)],
        compiler_params=_params("arbitrary"),
        name="s5_bwd_finish" if finish else "s5_fwd",
    )(*args)


def _block_diag(x):
    nblk, g, r, c = x.shape
    eye = jnp.eye(g, dtype=x.dtype)
    return jnp.einsum('jgrc,gh->jgrhc', x, eye).reshape(nblk, g * r, g * c)


def s5_tables(lam_re, lam_im, log_dt, b_re, b_im, c_re, c_im):
    lam_re = jnp.minimum(lam_re.astype(F32), -1e-4)
    lam_im = lam_im.astype(F32)
    dt = jnp.exp(log_dt.astype(F32))[:, None]
    mag = jnp.exp(lam_re * dt)
    lb_re = mag * jnp.cos(lam_im * dt)
    lb_im = mag * jnp.sin(lam_im * dt)
    den = lam_re * lam_re + lam_im * lam_im
    f_re = ((lb_re - 1.0) * lam_re + lb_im * lam_im) / den
    f_im = (lb_im * lam_re - (lb_re - 1.0) * lam_im) / den
    b_re = b_re.astype(F32)
    b_im = b_im.astype(F32)
    bb_re = f_re[..., None] * b_re - f_im[..., None] * b_im
    bb_im = f_re[..., None] * b_im + f_im[..., None] * b_re
    gpb = S5_KB // S5_GROUP
    to_b = lambda x: _block_diag(
        jnp.swapaxes(x, 1, 2).reshape(S5_NB, gpb, S5_GROUP, S5_STATE)).astype(BF16)
    to_c = lambda x: _block_diag(
        jnp.swapaxes(x.astype(F32), 1, 2).reshape(S5_NB, gpb, S5_STATE, S5_GROUP)).astype(BF16)
    return (to_b(bb_re), to_b(bb_im), lb_re.reshape(1, S5_W), lb_im.reshape(1, S5_W),
            to_c(c_re), to_c(-c_im.astype(F32)))


def dft_tables(n):
    nn = 2 * n
    f = jnp.arange(n, dtype=jnp.int32)[:, None]
    t = jnp.arange(n, dtype=jnp.int32)[None, :]
    ang = ((f * t) % nn).astype(F32) * (2.0 * math.pi / nn)
    cos = jnp.cos(ang)
    nyq = jnp.where(t % 2 == 0, 1.0, -1.0).astype(F32)
    msin = jnp.where(f == 0, nyq, -jnp.sin(ang))
    fwd = jnp.concatenate([cos, msin], axis=0)
    wf = jnp.where(jnp.arange(n) == 0, 1.0 / nn, 2.0 / nn).astype(F32)
    return fwd, wf


def _filtdft_kernel(fh_ref, fl_ref, hh_ref, hl_ref, o_ref):
    fh = fh_ref[...]
    o_ref[...] = _dot(fh, hh_ref[...]) + _dot(fh, hl_ref[...]) + _dot(fl_ref[...], hh_ref[...])


def filter_dft(fwd_hi, fwd_lo, h):
    n2, n = fwd_hi.shape
    c = h.shape[1]
    tr = min(256, n2)
    tcn = 512
    hh, hl = _split_bf16(h)
    return pl.pallas_call(
        _filtdft_kernel,
        grid=(n2 // tr, c // tcn),
        in_specs=[pl.BlockSpec((tr, n), lambda i, j: (i, 0)),
                  pl.BlockSpec((tr, n), lambda i, j: (i, 0)),
                  pl.BlockSpec((n, tcn), lambda i, j: (0, j)),
                  pl.BlockSpec((n, tcn), lambda i, j: (0, j))],
        out_specs=pl.BlockSpec((tr, tcn), lambda i, j: (i, j)),
        out_shape=jax.ShapeDtypeStruct((n2, c), F32),
        compiler_params=_params("arbitrary", "arbitrary"),
        name="filter_dft",
    )(fwd_hi, fwd_lo, hh, hl)


def hyena_filter_response(n, fwd_hi, fwd_lo, wf, f_w1, f_b1, f_w2, f_b2, f_w3, f_freq, log_decay):
    t = jnp.linspace(0.0, 1.0, n, dtype=F32)[:, None]
    w = 2.0 * math.pi * jnp.arange(n, dtype=F32)[:, None] / n
    bands = jnp.linspace(1e-4, HY_BANDS - 1, HY_BANDS, dtype=F32)[None, :]
    z = jnp.concatenate([t, jnp.cos(bands * w), -jnp.sin(bands * w)], axis=-1)
    freq = f_freq.astype(F32)
    hid = jnp.sin(freq[0] * (jnp.dot(z, f_w1.astype(F32), precision=HIGHEST) + f_b1.astype(F32)))
    hid = jnp.sin(freq[1] * (jnp.dot(hid, f_w2.astype(F32), precision=HIGHEST) + f_b2.astype(F32)))
    h = jnp.dot(hid, f_w3.astype(F32), precision=HIGHEST).reshape(n, 2, HY_ORDER, GROUP_W)
    h = h * jnp.exp(-t[:, :, None, None] * jnp.exp(log_decay.astype(F32)))
    cw = HY_ORDER * GROUP_W
    h_fwd = h[:, 0].reshape(n, cw)
    h_bwd = h[:, 1].reshape(n, cw).at[0].set(0.0)
    resp = filter_dft(fwd_hi, fwd_lo, jnp.concatenate([h_fwd, h_bwd], axis=1))
    a_re, a_im = resp[:n, :cw], resp[n:, :cw]
    b_re, b_im = resp[:n, cw:], resp[n:, cw:]
    k_re = a_re + b_re
    first = (jnp.arange(n) == 0)[:, None]
    k_im = jnp.where(first, a_im + b_im, a_im - b_im)
    k_re = (k_re * wf[:, None]).reshape(n, HY_ORDER, GROUP_W).transpose(1, 0, 2)
    k_im = (k_im * wf[:, None]).reshape(n, HY_ORDER, GROUP_W).transpose(1, 0, 2)
    return k_re, k_im


def _hyena_kernel(z0_ref, x1_ref, x2_ref, fc_ref, fs_ref, gc_ref, gs_ref, kr_ref, ki_ref, bias_ref,
                  o_ref, zb_s, zf_s, acc_s):
    o = pl.program_id(1)
    f = pl.program_id(2)
    nf = pl.num_programs(2)

    @pl.when((o == 0) & (f == 0))
    def _():
        z0 = z0_ref[0]
        zf_s[...] = z0
        zb_s[...] = z0.astype(BF16)

    @pl.when(f == 0)
    def _():
        acc_s[...] = jnp.zeros_like(acc_s)

    zb = zb_s[...]
    xr = _dot(fc_ref[...], zb)
    xi = _dot(fs_ref[...], zb)
    kr = kr_ref[0]
    ki = ki_ref[0]
    row = lax.broadcasted_iota(jnp.int32, (xr.shape[0], 1), 0)
    packed = (row == 0) & (f == 0)
    yr = jnp.where(packed, xr * kr, xr * kr - xi * ki)
    yi = jnp.where(packed, xi * ki, xr * ki + xi * kr)
    acc_s[...] += _dot(gc_ref[...], yr.astype(BF16)) + _dot(gs_ref[...], yi.astype(BF16))

    @pl.when(f == nf - 1)
    def _():
        bias = bias_ref[...]

        @pl.when(o == 0)
        def _():
            z1 = x1_ref[0] * (acc_s[...] + zf_s[...] * bias[0:1])
            zf_s[...] = z1
            zb_s[...] = z1.astype(BF16)

        @pl.when(o == 1)
        def _():
            o_ref[0] = x2_ref[0] * (acc_s[...] + zf_s[...] * bias[1:2])


def hyena_call(pc, fwd_hi, fwd_t, k_re, k_im, bias, out_prev, t_total, row_blk):
    b, n, _ = pc.shape
    c = GROUP_W
    fb = min(256, n)
    nf = n // fb
    once = pl.Buffered(1)
    in_specs = [pl.BlockSpec((1, n, c), lambda bb, o, f: (bb, 0, 0), pipeline_mode=once),
                pl.BlockSpec((1, n, c), lambda bb, o, f: (bb, 0, 1), pipeline_mode=once),
                pl.BlockSpec((1, n, c), lambda bb, o, f: (bb, 0, 2), pipeline_mode=once),
                pl.BlockSpec((fb, n), lambda bb, o, f: (f, 0)),
                pl.BlockSpec((fb, n), lambda bb, o, f: (nf + f, 0)),
                pl.BlockSpec((n, fb), lambda bb, o, f: (0, f)),
                pl.BlockSpec((n, fb), lambda bb, o, f: (0, nf + f)),
                pl.BlockSpec((1, fb, c), lambda bb, o, f: (o, f, 0)),
                pl.BlockSpec((1, fb, c), lambda bb, o, f: (o, f, 0)),
                pl.BlockSpec((HY_ORDER, c), lambda bb, o, f: (0, 0))]
    args = [pc, pc, pc, fwd_hi, fwd_hi, fwd_t, fwd_t, k_re, k_im, bias]
    aliases = {}
    if out_prev is not None:
        in_specs.append(pl.BlockSpec(memory_space=pl.ANY))
        args.append(out_prev)
        aliases = {len(args) - 1: 0}
        kern = lambda *r: _hyena_kernel(*r[:10], *r[11:])
    else:
        kern = _hyena_kernel
    return pl.pallas_call(
        kern,
        grid=(b, HY_ORDER, nf),
        in_specs=in_specs,
        out_specs=pl.BlockSpec((1, n, c), lambda bb, o, f: (bb, row_blk, 0)),
        out_shape=jax.ShapeDtypeStruct((b, t_total, c), F32),
        scratch_shapes=[pltpu.VMEM((n, c), BF16), pltpu.VMEM((n, c), F32), pltpu.VMEM((n, c), F32)],
        input_output_aliases=aliases,
        compiler_params=_params("arbitrary", "arbitrary", "arbitrary"),
        name="hyena",
    )(*args)


def _rwkv_lora_kernel(p_ref, ww_ref, wa_ref, wg_ref, w0_ref, a0_ref, wp0_ref, wp1_ref, ap0_ref, ap1_ref,
                      g_ref):
    x = p_ref[0]
    wa_in = x[:, :LANE]
    th = jnp.tanh(wa_in).astype(BF16)
    lin = wa_in.astype(BF16)
    w0 = w0_ref[...]
    a0 = a0_ref[...]
    wp0_ref[0] = w0[0:1] + _dot(th, ww_ref[0])
    wp1_ref[0] = w0[1:2] + _dot(th, ww_ref[1])
    ap0_ref[0] = a0[0:1] + _dot(lin, wa_ref[0])
    ap1_ref[0] = a0[1:2] + _dot(lin, wa_ref[1])
    g_ref[0] = _dot(jax.nn.sigmoid(x[:, LANE:]).astype(BF16), wg_ref[...])


def rwkv_lora(p, w_up, a_up, g_up, w0, a0):
    b, t, _ = p.shape
    c = GROUP_W
    tm = TOKEN_TILE
    zeros = jnp.zeros((2, 64, c), F32)
    ww = jnp.concatenate([w_up.astype(F32), zeros], axis=1).astype(BF16)
    wa = jnp.concatenate([zeros, a_up.astype(F32)], axis=1).astype(BF16)
    out = jax.ShapeDtypeStruct((b, t, c), F32)
    ospec = pl.BlockSpec((1, tm, c), lambda bb, i: (bb, i, 0))
    return pl.pallas_call(
        _rwkv_lora_kernel,
        grid=(b, t // tm),
        in_specs=[pl.BlockSpec((1, tm, RW_LORA_W), lambda bb, i: (bb, i, 3 * c // RW_LORA_W)),
                  pl.BlockSpec((2, LANE, c), lambda bb, i: (0, 0, 0)),
                  pl.BlockSpec((2, LANE, c), lambda bb, i: (0, 0, 0)),
                  pl.BlockSpec((LANE, c), lambda bb, i: (0, 0)),
                  pl.BlockSpec((2, c), lambda bb, i: (0, 0)),
                  pl.BlockSpec((2, c), lambda bb, i: (0, 0))],
        out_specs=[ospec] * 5,
        out_shape=[out] * 5,
        compiler_params=_params("arbitrary", "arbitrary"),
        name="rwkv_lora",
    )(p, ww, wa, g_up.astype(BF16), w0, a0)


def _rwkv_scan_kernel(wp_ref, ap_ref, k_ref, v_ref, r_ref, kkc_ref, kac_ref, rkc_ref, y_ref, bon_ref,
                      s_ref, w_s, kk_s, b_s, ke_s, r_s, *, tb):
    n = RW_HEAD

    @pl.when(pl.program_id(0) == 0)
    def _():
        s_ref[...] = jnp.zeros_like(s_ref)

    def step(t, carry):
        k = k_ref[t]
        a = jax.nn.sigmoid(ap_ref[t])
        kk = k * kkc_ref[...]
        nrm = jnp.sqrt(jnp.sum(kk * kk, axis=0, keepdims=True))
        kk = kk / jnp.maximum(nrm, 1e-12)
        ke = k * (1.0 + (a - 1.0) * kac_ref[...])
        wl = -jax.nn.softplus(-wp_ref[t]) - 0.5
        r = r_ref[t]
        v = v_ref[t]
        w_s[...] = jnp.exp(-jnp.exp(wl))
        kk_s[...] = kk
        b_s[...] = kk * a
        ke_s[...] = ke
        r_s[...] = r
        bon_ref[pl.ds(t, 1), :] = jnp.sum(r * ke * rkc_ref[...], axis=0, keepdims=True)

        def p1(j, sa):
            return sa + s_ref[j] * kk_s[pl.ds(j, 1), :]

        sa = lax.fori_loop(0, n, p1, jnp.zeros((n, LANE), F32), unroll=8)

        def p2(j, y):
            s = s_ref[j] * w_s[pl.ds(j, 1), :] - sa * b_s[pl.ds(j, 1), :] + v * ke_s[pl.ds(j, 1), :]
            s_ref[j] = s
            return y + s * r_s[pl.ds(j, 1), :]

        y_ref[t] = lax.fori_loop(0, n, p2, jnp.zeros((n, LANE), F32), unroll=8)
        return carry

    lax.fori_loop(0, tb, step, 0)


def rwkv_scan(wp, ap, k, v, r, kkc, kac, rkc):
    steps = wp.shape[0]
    tb = 32
    n = RW_HEAD
    blk = pl.BlockSpec((tb, n, LANE), lambda c: (c, 0, 0))
    cst = pl.BlockSpec((n, LANE), lambda c: (0, 0))
    tile = pltpu.VMEM((n, LANE), F32)
    return pl.pallas_call(
        functools.partial(_rwkv_scan_kernel, tb=tb),
        grid=(steps // tb,),
        in_specs=[blk] * 5 + [cst] * 3,
        out_specs=[blk, pl.BlockSpec((tb, LANE), lambda c: (c, 0))],
        out_shape=[jax.ShapeDtypeStruct((steps, n, LANE), F32),
                   jax.ShapeDtypeStruct((steps, LANE), F32)],
        scratch_shapes=[pltpu.VMEM((n, n, LANE), F32), tile, tile, tile, tile, tile],
        compiler_params=_params("arbitrary"),
        name="rwkv_scan",
    )(wp, ap, k, v, r, kkc, kac, rkc)


def _rwkv_post_kernel(yf_ref, yb_ref, bf_ref, bb_ref, v_ref, g_ref, m_ref, lg_ref, lb_ref, o_ref):
    m = m_ref[...]

    def head_mean(x):
        hi, lo = _split_bf16(x)
        return _dot(hi, m) + _dot(lo, m)

    y = yf_ref[0] + yb_ref[0]
    d = y - head_mean(y)
    var = head_mean(d * d)
    yn = d * lax.rsqrt(var + RW_LN_EPS) * lg_ref[...] + lb_ref[...]
    o_ref[0] = (yn + (bf_ref[0] + bb_ref[0]) * v_ref[0]) * g_ref[0]


def rwkv_post(yf, yb, bonf, bonb, v, gate, ln_g, ln_b, n_rows):
    b, t, c = yf.shape
    tm = TOKEN_TILE
    hm = jnp.kron(jnp.eye(RW_HEADS, dtype=F32), jnp.full((RW_HEAD, RW_HEAD), 1.0 / RW_HEAD, F32)).astype(BF16)
    tok = pl.BlockSpec((1, tm, c), lambda bb, i: (bb, i, 0))
    vec = pl.BlockSpec((1, c), lambda bb, i: (0, 0))
    return pl.pallas_call(
        _rwkv_post_kernel,
        grid=(b, n_rows // tm),
        in_specs=[tok] * 6 + [pl.BlockSpec((c, c), lambda bb, i: (0, 0)), vec, vec],
        out_specs=tok,
        out_shape=jax.ShapeDtypeStruct((b, t, c), F32),
        compiler_params=_params("arbitrary", "arbitrary"),
        name="rwkv_post",
    )(yf, yb, bonf, bonb, v, gate, hm, ln_g.reshape(1, c), ln_b.reshape(1, c))


def _to_scan_layout(a_f, a_b, n_lat):
    b, t, _ = a_f.shape

    def lay(a):
        return a.reshape(b, t, RW_HEADS, RW_HEAD).transpose(1, 3, 0, 2).reshape(t, RW_HEAD, b * RW_HEADS)

    fwd = jnp.concatenate([a_f[:, n_lat:], a_f[:, :n_lat]], axis=1)
    bwd = jnp.concatenate([a_b[:, n_lat:][:, ::-1], a_b[:, :n_lat][:, ::-1]], axis=1)
    return jnp.concatenate([lay(fwd), lay(bwd)], axis=-1)


def _from_scan_layout(y, b, n_lat):
    t = y.shape[0]
    n_ctx = t - n_lat
    half = b * RW_HEADS
    if y.ndim == 3:
        unlay = lambda a: a.reshape(t, RW_HEAD, b, RW_HEADS).transpose(2, 0, 3, 1).reshape(b, t, GROUP_W)
    else:
        unlay = lambda a: jnp.repeat(a.reshape(t, b, RW_HEADS).transpose(1, 0, 2), RW_HEAD, axis=-1)
    yf = unlay(y[..., :half])
    yb = unlay(y[..., half:])
    yf = jnp.concatenate([yf[:, n_ctx:], yf[:, :n_ctx]], axis=1)
    yb = jnp.concatenate([yb[:, n_ctx:][:, ::-1], yb[:, :n_ctx][:, ::-1]], axis=1)
    return yf, yb


def _chain_const(x, b):
    return jnp.tile(x.astype(F32).reshape(RW_HEADS, RW_HEAD).T, (1, 2 * b))


def _rope(x, cos, sin):
    lane = lax.broadcasted_iota(jnp.int32, (1, LANE), 1)
    first = (lane % (2 * ROPE_FREQS)) < ROPE_FREQS
    partner = jnp.where(first, pltpu.roll(x, LANE - ROPE_FREQS, 1), pltpu.roll(x, ROPE_FREQS, 1))
    return x * cos + partner * sin


def _attn_tile(q, k, v, lam):
    lane = lax.broadcasted_iota(jnp.int32, (1, LANE), 1)
    m0 = lane < DA_HEAD
    q0 = jnp.where(m0, q, 0.0).astype(BF16)
    q1 = jnp.where(m0, 0.0, q).astype(BF16)
    dn = (((1,), (1,)), ((), ()))
    s0 = lax.dot_general(q0, k, dn, preferred_element_type=F32)
    s1 = lax.dot_general(q1, k, dn, preferred_element_type=F32)
    p0 = jnp.exp(s0 - jnp.max(s0, axis=-1, keepdims=True))
    p1 = jnp.exp(s1 - jnp.max(s1, axis=-1, keepdims=True))
    w = p0 / jnp.sum(p0, axis=-1, keepdims=True) - lam * (p1 / jnp.sum(p1, axis=-1, keepdims=True))
    return _dot(w.astype(BF16), v)


def _attn_kernel(q_ref, k_ref, v_ref, cq_ref, sq_ref, ck_ref, sk_ref, lp_ref, g_ref, o_ref,
                 kr_s, vb_s, *, n_lat, n_lat_tiles, lam_init):
    i = pl.program_id(2)

    @pl.when(i == 0)
    def _():
        kr_s[...] = _rope(k_ref[0], ck_ref[...], sk_ref[...]).astype(BF16)
        vb_s[...] = v_ref[0].astype(BF16)

    lp = lp_ref[...]
    lam = (jnp.exp(jnp.sum(lp[0:1] * lp[1:2], axis=-1, keepdims=True))
           - jnp.exp(jnp.sum(lp[2:3] * lp[3:4], axis=-1, keepdims=True)) + lam_init)
    q = _rope(q_ref[0], cq_ref[...], sq_ref[...]) * DA_SCALE

    def finish(o):
        on = o * lax.rsqrt(jnp.mean(o * o, axis=-1, keepdims=True) + DA_SUBLN_EPS)
        o_ref[0] = on * g_ref[...] * (1.0 - lam_init)

    @pl.when(i < n_lat_tiles)
    def _():
        finish(_attn_tile(q, kr_s[...], vb_s[...], lam))

    @pl.when(i >= n_lat_tiles)
    def _():
        finish(_attn_tile(q, kr_s[n_lat:], vb_s[n_lat:], lam))


def diff_attention(p, col0, cosf, sins, lam_p, subln_g, lam_init, n_lat, n_rows):
    b, t, _ = p.shape
    tq = TOKEN_TILE
    off = col0 // LANE
    hq = DA_HEADS
    return pl.pallas_call(
        functools.partial(_attn_kernel, n_lat=n_lat, n_lat_tiles=n_lat // tq, lam_init=lam_init),
        grid=(b, hq, n_rows // tq),
        in_specs=[pl.BlockSpec((1, tq, LANE), lambda bb, h, i: (bb, i, off + h)),
                  pl.BlockSpec((1, t, LANE), lambda bb, h, i: (bb, 0, off + hq + h)),
                  pl.BlockSpec((1, t, LANE), lambda bb, h, i: (bb, 0, off + 2 * hq + h)),
                  pl.BlockSpec((tq, LANE), lambda bb, h, i: (i, 0)),
                  pl.BlockSpec((tq, LANE), lambda bb, h, i: (i, 0)),
                  pl.BlockSpec((t, LANE), lambda bb, h, i: (0, 0)),
                  pl.BlockSpec((t, LANE), lambda bb, h, i: (0, 0)),
                  pl.BlockSpec((4, DA_HEAD), lambda bb, h, i: (0, 0)),
                  pl.BlockSpec((1, LANE), lambda bb, h, i: (0, 0))],
        out_specs=pl.BlockSpec((1, tq, LANE), lambda bb, h, i: (bb, i, h)),
        out_shape=jax.ShapeDtypeStruct((b, t, hq * DA_V), F32),
        scratch_shapes=[pltpu.VMEM((t, LANE), BF16), pltpu.VMEM((t, LANE), BF16)],
        compiler_params=_params("arbitrary", "arbitrary", "arbitrary"),
        name="diff_attention",
    )(p, p, p, cosf, sins, cosf, sins, lam_p, subln_g.reshape(1, DA_V))


def rope_tables(n_lat, n_ctx):
    rows = n_lat // GRID_W
    row = jnp.repeat(jnp.arange(rows, dtype=F32), GRID_W)
    col = jnp.tile(jnp.arange(GRID_W, dtype=F32), rows)
    inv = ROPE_BASE ** (-jnp.arange(ROPE_FREQS, dtype=F32) / ROPE_FREQS)
    ang = jnp.stack([row[:, None] * inv, col[:, None] * inv], axis=1)
    cos, sin = jnp.cos(ang), jnp.sin(ang)
    cosf = jnp.concatenate([cos, cos], axis=-1).reshape(n_lat, DA_HEAD)
    sins = jnp.concatenate([-sin, sin], axis=-1).reshape(n_lat, DA_HEAD)
    cosf = jnp.concatenate([jnp.tile(cosf, (1, 2)), jnp.ones((n_ctx, LANE), F32)], axis=0)
    sins = jnp.concatenate([jnp.tile(sins, (1, 2)), jnp.zeros((n_ctx, LANE), F32)], axis=0)
    return cosf, sins


def _even_layer(xs, mods, keep_ctx, n_lat, norm1_g, w_in, w_out, s5p, hyp, dft):
    b, t, _ = xs.shape
    n_ctx = t - n_lat
    n_rows = t if keep_ctx else n_lat
    w_in = w_in.astype(BF16)
    u_t = in_proj(xs, norm1_g, mods, w_in[:, :GROUP_W], n_lat, GROUP_W, True)
    p_h = in_proj(xs, norm1_g, mods, w_in[:, GROUP_W:], n_lat, 512, False)

    (lam_re, lam_im, log_dt, b_re, b_im, c_re, c_im, d_skip, glu_w, glu_b) = s5p
    u2 = u_t.reshape(t * b, GROUP_W)
    tabs = [s5_tables(lam_re[d], lam_im[d], log_dt[d], b_re[d], b_im[d], c_re[d], c_im[d]) for d in range(2)]
    y_f = s5_scan_call(u2, tabs[0], n_lat, t, b, False, None)
    a_t = s5_scan_call(u2, tabs[1], n_lat, t, b, True,
                       (y_f, d_skip.reshape(1, GROUP_W), glu_w.astype(BF16), glu_b.reshape(1, GROUP_W)))
    a_t = a_t.reshape(t, b * GROUP_W)

    (conv_w, conv_b, f_w1, f_b1, f_w2, f_b2, f_w3, f_freq, log_decay, bias) = hyp
    pc_l, pc_c = short_conv(p_h, 0, conv_w, conv_b, n_lat)
    fh_l, fl_l, ft_l, wf_l = dft[0]
    kr, ki = hyena_filter_response(n_lat, fh_l, fl_l, wf_l, f_w1, f_b1, f_w2, f_b2, f_w3, f_freq, log_decay)
    b_m = hyena_call(pc_l, fh_l, ft_l, kr, ki, bias, None, t, 0)
    if keep_ctx:
        fh_c, fl_c, ft_c, wf_c = dft[1]
        kr, ki = hyena_filter_response(n_ctx, fh_c, fl_c, wf_c, f_w1, f_b1, f_w2, f_b2, f_w3, f_freq,
                                       log_decay)
        b_m = hyena_call(pc_c, fh_c, ft_c, kr, ki, bias, b_m, t, n_lat // n_ctx)
    return out_proj(xs, a_t, b_m, w_out.astype(BF16), mods, n_lat, n_rows, True)


def _odd_layer(xs, mods, keep_ctx, n_lat, lam_init, norm1_g, w_in, w_out, rwp, dap, rope):
    b, t, _ = xs.shape
    n_rows = t if keep_ctx else n_lat
    p = in_proj(xs, norm1_g, mods, w_in.astype(BF16), n_lat, 256, False)

    (conv_w, w0, w_up, a0, a_up, g_up, k_k, k_a, r_k, ln_g, ln_b) = rwp
    rkv_l, rkv_c = short_conv(p, 0, conv_w, jnp.zeros((3 * GROUP_W,), F32), n_lat)
    rkv = jnp.concatenate([rkv_l, rkv_c], axis=1)
    r, k, v = (rkv[..., i * GROUP_W:(i + 1) * GROUP_W] for i in range(3))
    wp0, wp1, ap0, ap1, gate = rwkv_lora(p, w_up, a_up, g_up, w0, a0)
    y, bon = rwkv_scan(_to_scan_layout(wp0, wp1, n_lat), _to_scan_layout(ap0, ap1, n_lat),
                       _to_scan_layout(k, k, n_lat), _to_scan_layout(v, v, n_lat),
                       _to_scan_layout(r, r, n_lat),
                       _chain_const(k_k, b), _chain_const(k_a, b), _chain_const(r_k, b))
    yf, yb = _from_scan_layout(y, b, n_lat)
    bonf, bonb = _from_scan_layout(bon, b, n_lat)
    a_m = rwkv_post(yf, yb, bonf, bonb, v, gate, ln_g, ln_b, n_rows)

    lam_p, subln_g = dap
    b_m = diff_attention(p, RW_IN, rope[0], rope[1], lam_p, subln_g, lam_init, n_lat, n_rows)
    return out_proj(xs, a_m, b_m, w_out.astype(BF16), mods, n_lat, n_rows, False)


def kernel(x, c, ctx, c_ctx, ada_w, ada_b, norm1_g, norm2_g, mlp_w1, mlp_w2, final_g, ev_w_in, ev_w_out, s5_lam_re, s5_lam_im, s5_log_dt, s5_b_re, s5_b_im, s5_c_re, s5_c_im, s5_d, s5_glu_w, s5_glu_b, hy_conv_w, hy_conv_b, hy_f_w1, hy_f_b1, hy_f_w2, hy_f_b2, hy_f_w3, hy_f_freq, hy_log_decay, hy_bias, od_w_in, od_w_out, rw_conv_w, rw_w0, rw_w_up, rw_a0, rw_a_up, rw_g_up, rw_k_k, rw_k_a, rw_r_k, rw_ln_g, rw_ln_b, da_lam, da_subln_g):
    b, n_lat, d = x.shape
    n_ctx = ctx.shape[1]
    assert d == D_MODEL and b == SUBLANE
    assert n_lat % TOKEN_TILE == 0 and n_ctx % TOKEN_TILE == 0 and n_lat % n_ctx == 0
    xs = jnp.concatenate([x.astype(F32), ctx.astype(F32)], axis=1)

    c_rows = jnp.zeros((16, d), F32).at[:b].set(c.astype(F32)).at[b].set(c_ctx.astype(F32))
    depth = ada_w.shape[0]
    mods_all = ada_mods(c_rows, ada_w, ada_b).reshape(depth, 16, 6, d)

    rope = rope_tables(n_lat, n_ctx)
    dft = []
    for n in (n_lat, n_ctx):
        fwd, wf = dft_tables(n)
        hi, lo = _split_bf16(fwd)
        dft.append((hi, lo, hi.T, wf))

    for l in range(depth):
        keep_ctx = l < depth - 1
        i = l // 2
        mods = mods_all[l]
        if l % 2 == 0:
            s5p = (s5_lam_re[i], s5_lam_im[i], s5_log_dt[i], s5_b_re[i], s5_b_im[i], s5_c_re[i], s5_c_im[i],
                   s5_d[i], s5_glu_w[i], s5_glu_b[i])
            hyp = (hy_conv_w[i], hy_conv_b[i], hy_f_w1[i], hy_f_b1[i], hy_f_w2[i], hy_f_b2[i], hy_f_w3[i],
                   hy_f_freq[i], hy_log_decay[i], hy_bias[i])
            xs = _even_layer(xs, mods, keep_ctx, n_lat, norm1_g[l], ev_w_in[i], ev_w_out[i], s5p, hyp, dft)
        else:
            rwp = (rw_conv_w[i], rw_w0[i], rw_w_up[i], rw_a0[i], rw_a_up[i], rw_g_up[i], rw_k_k[i],
                   rw_k_a[i], rw_r_k[i], rw_ln_g[i], rw_ln_b[i])
            lam_init = 0.8 - 0.6 * math.exp(-0.3 * l)
            xs = _odd_layer(xs, mods, keep_ctx, n_lat, lam_init, norm1_g[l], od_w_in[i], od_w_out[i], rwp,
                            (da_lam[i], da_subln_g[i]), rope)
        n_rows = xs.shape[1]
        xs = mlp(xs, norm2_g[l], mods, mlp_w1[l].astype(BF16), mlp_w2[l].astype(BF16), final_g,
                 n_lat, n_rows, l == depth - 1)
    return xs
```

```python
import functools
import math

import jax
import jax.numpy as jnp
from jax import lax
from jax.experimental import pallas as pl
from jax.experimental.pallas import tpu as pltpu

F32 = jnp.float32
BF16 = jnp.bfloat16
HIGHEST = lax.Precision.HIGHEST

D_MODEL = 1024
DEPTH = 4
GRID_W = 64
MLP_HIDDEN = 4 * D_MODEL
GROUP_W = D_MODEL // 2
EPS = 1e-6

S5_GROUP = 16
S5_GROUPS = GROUP_W // S5_GROUP
S5_STATE = 64
S5_W = S5_GROUPS * S5_STATE

HY_ORDER = 2
HY_EMB = 33
HY_BANDS = (HY_EMB - 1) // 2
SHORT_K = 3

RW_HEAD = 64
RW_HEADS = GROUP_W // RW_HEAD
RW_LORA_W = 256
RW_LN_EPS = 64e-5
RW_IN = 3 * GROUP_W + RW_LORA_W

DA_HEADS = 4
DA_HEAD = 64
DA_V = 2 * DA_HEAD
DA_SCALE = DA_HEAD ** -0.5
DA_SUBLN_EPS = 1e-5
ROPE_BASE = 10000.0
ROPE_FREQS = DA_HEAD // 4

LANE = 128
SUBLANE = 8
TOKEN_TILE = 256
VMEM_LIMIT = 48 * 1024 * 1024


def _params(*sem):
    return pltpu.CompilerParams(dimension_semantics=sem, vmem_limit_bytes=VMEM_LIMIT)


def _split_bf16(x):
    hi = x.astype(BF16)
    lo = (x - hi.astype(F32)).astype(BF16)
    return hi, lo


def _dot(a, b):
    return jnp.dot(a, b, preferred_element_type=F32)


def _ada_kernel(c_ref, w_ref, b_ref, o_ref):
    c = c_ref[...]
    s = c * jax.nn.sigmoid(c)
    o_ref[0] = jnp.dot(s, w_ref[0], preferred_element_type=F32, precision=HIGHEST) + b_ref[0]


def ada_mods(c_rows, ada_w, ada_b):
    depth, d, n = ada_w.shape
    tn = 1024
    return pl.pallas_call(
        _ada_kernel,
        grid=(depth, n // tn),
        in_specs=[pl.BlockSpec((16, d), lambda l, j: (0, 0)),
                  pl.BlockSpec((1, d, tn), lambda l, j: (l, 0, j)),
                  pl.BlockSpec((1, 1, tn), lambda l, j: (l, 0, j))],
        out_specs=pl.BlockSpec((1, 16, tn), lambda l, j: (l, 0, j)),
        out_shape=jax.ShapeDtypeStruct((depth, 16, n), F32),
        compiler_params=_params("arbitrary", "arbitrary"),
        name="ada_mods",
    )(c_rows, ada_w, ada_b.reshape(depth, 1, n))


def _norm_mod(x, g, ml, mc, is_ctx, k):
    ms = jnp.mean(x * x, axis=-1, keepdims=True)
    xn = x * lax.rsqrt(ms + EPS) * g
    m = jnp.where(is_ctx, mc, ml)
    return xn * (1.0 + m[k + 1:k + 2]) + m[k:k + 1]


def _inproj_kernel(x_ref, g_ref, ml_ref, mc_ref, w_ref, *o_refs, n_lat_tiles, n_tmaj):
    i = pl.program_id(1)
    h = _norm_mod(x_ref[0], g_ref[...], ml_ref[0], mc_ref[0], i >= n_lat_tiles, 0)
    r = _dot(h.astype(BF16), w_ref[...])
    if n_tmaj:
        o_refs[0][...] = r[:, :n_tmaj]
        o_refs[1][0] = r[:, n_tmaj:]
    else:
        o_refs[0][0] = r


def in_proj(x, g, mods, w, n_lat, n_tmaj):
    b, t, d = x.shape
    n = w.shape[1]
    tm = TOKEN_TILE
    if n_tmaj:
        out_shape = [jax.ShapeDtypeStruct((t, b * n_tmaj), F32), jax.ShapeDtypeStruct((b, t, n - n_tmaj), F32)]
        out_specs = [pl.BlockSpec((tm, n_tmaj), lambda bb, i: (i, bb)),
                     pl.BlockSpec((1, tm, n - n_tmaj), lambda bb, i: (bb, i, 0))]
    else:
        out_shape = [jax.ShapeDtypeStruct((b, t, n), F32)]
        out_specs = [pl.BlockSpec((1, tm, n), lambda bb, i: (bb, i, 0))]
    return pl.pallas_call(
        functools.partial(_inproj_kernel, n_lat_tiles=n_lat // tm, n_tmaj=n_tmaj),
        grid=(b, t // tm),
        in_specs=[pl.BlockSpec((1, tm, d), lambda bb, i: (bb, i, 0)),
                  pl.BlockSpec((1, d), lambda bb, i: (0, 0)),
                  pl.BlockSpec((1, 6, d), lambda bb, i: (bb, 0, 0)),
                  pl.BlockSpec((1, 6, d), lambda bb, i: (b, 0, 0)),
                  pl.BlockSpec((d, n), lambda bb, i: (0, 0), pipeline_mode=pl.Buffered(1))],
        out_specs=out_specs,
        out_shape=out_shape,
        compiler_params=_params("arbitrary", "arbitrary"),
        name="in_proj",
    )(x, g.reshape(1, d), mods, mods, w)


def _outproj_kernel(x_ref, a_ref, b_ref, wa_ref, wb_ref, ml_ref, mc_ref, o_ref, *, n_lat_tiles, a_tmaj):
    i = pl.program_id(1)
    a = a_ref[...] if a_tmaj else a_ref[0]
    y = _dot(a.astype(BF16), wa_ref[...]) + _dot(b_ref[0].astype(BF16), wb_ref[...])
    m = jnp.where(i >= n_lat_tiles, mc_ref[0], ml_ref[0])
    o_ref[0] = x_ref[0] + m[2:3] * y


def out_proj(x, a, bm, w_out, mods, n_lat, n_rows, a_tmaj):
    b, t, d = x.shape
    wd = bm.shape[-1]
    tm = TOKEN_TILE
    if a_tmaj:
        a_spec = pl.BlockSpec((tm, wd), lambda bb, i: (i, bb))
    else:
        a_spec = pl.BlockSpec((1, tm, wd), lambda bb, i: (bb, i, 0))
    return pl.pallas_call(
        functools.partial(_outproj_kernel, n_lat_tiles=n_lat // tm, a_tmaj=a_tmaj),
        grid=(b, n_rows // tm),
        in_specs=[pl.BlockSpec((1, tm, d), lambda bb, i: (bb, i, 0)),
                  a_spec,
                  pl.BlockSpec((1, tm, wd), lambda bb, i: (bb, i, 0)),
                  pl.BlockSpec((wd, d), lambda bb, i: (0, 0)),
                  pl.BlockSpec((wd, d), lambda bb, i: (1, 0)),
                  pl.BlockSpec((1, 6, d), lambda bb, i: (bb, 0, 0)),
                  pl.BlockSpec((1, 6, d), lambda bb, i: (b, 0, 0))],
        out_specs=pl.BlockSpec((1, tm, d), lambda bb, i: (bb, i, 0)),
        out_shape=jax.ShapeDtypeStruct((b, n_rows, d), F32),
        compiler_params=_params("arbitrary", "arbitrary"),
        name="out_proj",
    )(x, a, bm, w_out, w_out, mods, mods)


MLP_CHUNK = 1024


def _mlp_kernel(x_ref, g_ref, ml_ref, mc_ref, w1_ref, w2_ref, fg_ref, o_ref, acc_ref, *, n_lat_tiles, final):
    i = pl.program_id(1)
    is_ctx = i >= n_lat_tiles
    x = x_ref[0]
    h = _norm_mod(x, g_ref[...], ml_ref[0], mc_ref[0], is_ctx, 3).astype(BF16)
    for k in range(w1_ref.shape[1] // MLP_CHUNK):
        ks = slice(k * MLP_CHUNK, (k + 1) * MLP_CHUNK)
        a = jnp.maximum(_dot(h, w1_ref[:, ks]), 0.0)
        part = _dot((a * a).astype(BF16), w2_ref[ks, :])
        if k == 0:
            acc_ref[...] = part
        else:
            acc_ref[...] += part
    m = jnp.where(is_ctx, mc_ref[0], ml_ref[0])
    y = x + m[5:6] * acc_ref[...]
    if final:
        ms = jnp.mean(y * y, axis=-1, keepdims=True)
        y = y * lax.rsqrt(ms + EPS) * fg_ref[...]
    o_ref[0] = y


def mlp(x, g, mods, w1, w2, final_g, n_lat, n_rows, final):
    b, t, d = x.shape
    hid = w1.shape[1]
    tm = TOKEN_TILE
    once = pl.Buffered(1)
    return pl.pallas_call(
        functools.partial(_mlp_kernel, n_lat_tiles=n_lat // tm, final=final),
        grid=(b, n_rows // tm),
        in_specs=[pl.BlockSpec((1, tm, d), lambda bb, i: (bb, i, 0)),
                  pl.BlockSpec((1, d), lambda bb, i: (0, 0)),
                  pl.BlockSpec((1, 6, d), lambda bb, i: (bb, 0, 0)),
                  pl.BlockSpec((1, 6, d), lambda bb, i: (b, 0, 0)),
                  pl.BlockSpec((d, hid), lambda bb, i: (0, 0), pipeline_mode=once),
                  pl.BlockSpec((hid, d), lambda bb, i: (0, 0), pipeline_mode=once),
                  pl.BlockSpec((1, d), lambda bb, i: (0, 0))],
        out_specs=pl.BlockSpec((1, tm, d), lambda bb, i: (bb, i, 0)),
        out_shape=jax.ShapeDtypeStruct((b, n_rows, d), F32),
        scratch_shapes=[pltpu.VMEM((tm, d), F32)],
        compiler_params=_params("arbitrary", "arbitrary"),
        name="mlp",
    )(x, g.reshape(1, d), mods, mods, w1, w2, final_g.reshape(1, d))


def _shortconv_kernel(x_ref, w_ref, b_ref, *o_refs, n_lat):
    x = x_ref[0]
    t = x.shape[0]
    row = lax.broadcasted_iota(jnp.int32, (t, 1), 0)
    first = (row == 0) | (row == n_lat)
    last = (row == n_lat - 1) | (row == t - 1)
    xm = jnp.where(first, 0.0, pltpu.roll(x, 1, 0))
    xp = jnp.where(last, 0.0, pltpu.roll(x, t - 1, 0))
    w = w_ref[...]
    y = xm * w[0:1] + x * w[1:2] + xp * w[2:3] + b_ref[...]
    if len(o_refs) == 2:
        o_refs[0][0] = y[:n_lat]
        o_refs[1][0] = y[n_lat:]
    else:
        o_refs[0][0] = y


def short_conv(p, col0, w, bias, n_lat, split):
    b, t, _ = p.shape
    c = w.shape[1]
    cb = 512
    off = col0 // cb
    if split:
        out_specs = [pl.BlockSpec((1, n_lat, cb), lambda bb, j: (bb, 0, j)),
                     pl.BlockSpec((1, t - n_lat, cb), lambda bb, j: (bb, 0, j))]
        out_shape = [jax.ShapeDtypeStruct((b, n_lat, c), F32), jax.ShapeDtypeStruct((b, t - n_lat, c), F32)]
    else:
        out_specs = [pl.BlockSpec((1, t, cb), lambda bb, j: (bb, 0, j))]
        out_shape = [jax.ShapeDtypeStruct((b, t, c), F32)]
    return pl.pallas_call(
        functools.partial(_shortconv_kernel, n_lat=n_lat),
        grid=(b, c // cb),
        in_specs=[pl.BlockSpec((1, t, cb), lambda bb, j: (bb, 0, off + j)),
                  pl.BlockSpec((SHORT_K, cb), lambda bb, j: (0, j)),
                  pl.BlockSpec((1, cb), lambda bb, j: (0, j))],
        out_specs=out_specs,
        out_shape=out_shape,
        compiler_params=_params("arbitrary", "arbitrary"),
        name="short_conv",
    )(p, w, bias.reshape(1, c))


S5_KB = LANE
S5_NB = GROUP_W // S5_KB
S5_SB = S5_W // S5_NB


def _s5_kernel(*refs, tc, nb, reverse, finish):
    if finish:
        (u_ref, wbr_ref, wbi_ref, lr_ref, li_ref, wcr_ref, wci_ref, yf_ref, d_ref, gw_ref, gb_ref,
         o_ref, hr_s, hi_s, sr_s, si_s) = refs
    else:
        (u_ref, wbr_ref, wbi_ref, lr_ref, li_ref, wcr_ref, wci_ref,
         o_ref, hr_s, hi_s, sr_s, si_s) = refs

    @pl.when(pl.program_id(0) == 0)
    def _():
        sr_s[...] = jnp.zeros_like(sr_s)
        si_s[...] = jnp.zeros_like(si_s)

    u = u_ref[...]
    ub = u.astype(BF16)
    for j in range(S5_NB):
        uj = ub[:, j * S5_KB:(j + 1) * S5_KB]
        hr_s[:, j * S5_SB:(j + 1) * S5_SB] = _dot(uj, wbr_ref[j])
        hi_s[:, j * S5_SB:(j + 1) * S5_SB] = _dot(uj, wbi_ref[j])

    for s in range(S5_NB):
        sl = slice(s * S5_SB, (s + 1) * S5_SB)
        lr = jnp.broadcast_to(lr_ref[:, sl], (nb, S5_SB))
        li = jnp.broadcast_to(li_ref[:, sl], (nb, S5_SB))

        def body(k, carry, sl=sl, lr=lr, li=li):
            hr, hi = carry
            t = (tc - 1 - k) if reverse else k
            r0 = pl.multiple_of(t * nb, nb)
            nr = lr * hr - li * hi + hr_s[pl.ds(r0, nb), sl]
            ni = lr * hi + li * hr + hi_s[pl.ds(r0, nb), sl]
            hr_s[pl.ds(r0, nb), sl] = nr
            hi_s[pl.ds(r0, nb), sl] = ni
            return nr, ni

        hr, hi = lax.fori_loop(0, tc, body, (sr_s[:, sl], si_s[:, sl]), unroll=4)
        sr_s[:, sl] = hr
        si_s[:, sl] = hi

    for j in range(S5_NB):
        sl = slice(j * S5_SB, (j + 1) * S5_SB)
        co = slice(j * S5_KB, (j + 1) * S5_KB)
        y = _dot(hr_s[:, sl].astype(BF16), wcr_ref[j]) + _dot(hi_s[:, sl].astype(BF16), wci_ref[j])
        if finish:
            o_ref[:, co] = y + yf_ref[:, co] + u[:, co] * d_ref[:, co]
        else:
            o_ref[:, co] = y

    if finish:
        y = o_ref[...]
        y = 0.5 * y * (1.0 + lax.erf(y * (2.0 ** -0.5)))
        z = _dot(y.astype(BF16), gw_ref[...]) + gb_ref[...]
        o_ref[...] = y * jax.nn.sigmoid(z)


def s5_scan_call(u2, tabs, n_lat_steps, n_steps, nb, reverse, extra):
    tc = 32
    rows = tc * nb
    n_chunks = n_steps // tc
    n_lat_chunks = n_lat_steps // tc
    n_ctx_chunks = n_chunks - n_lat_chunks
    if reverse:
        def cmap(c):
            return (n_chunks - 1 - c, 0)
    else:
        def cmap(c):
            return (jnp.where(c < n_ctx_chunks, n_lat_chunks + c, c - n_ctx_chunks), 0)
    wbr, wbi, lr, li, wcr, wci = tabs
    full3 = lambda a: pl.BlockSpec(a.shape, lambda c: (0, 0, 0))
    full2 = lambda a: pl.BlockSpec(a.shape, lambda c: (0, 0))
    in_specs = [pl.BlockSpec((rows, GROUP_W), cmap), full3(wbr), full3(wbi), full2(lr), full2(li),
                full3(wcr), full3(wci)]
    args = [u2, wbr, wbi, lr, li, wcr, wci]
    finish = extra is not None
    if finish:
        yf, dsk, gw, gb = extra
        in_specs += [pl.BlockSpec((rows, GROUP_W), cmap), full2(dsk), full2(gw), full2(gb)]
        args += [yf, dsk, gw, gb]
    return pl.pallas_call(
        functools.partial(_s5_kernel, tc=tc, nb=nb, reverse=reverse, finish=finish),
        grid=(n_chunks,),
        in_specs=in_specs,
        out_specs=pl.BlockSpec((rows, GROUP_W), cmap),
        out_shape=jax.ShapeDtypeStruct(u2.shape, F32),
        scratch_shapes=[pltpu.VMEM((rows, S5_W), F32), pltpu.VMEM((rows, S5_W), F32),
                        pltpu.VMEM((nb, S5_W), F32), pltpu.VMEM((nb, S5_W), F32)],
        compiler_params=_params("arbitrary"),
        name="s5_bwd_finish" if finish else "s5_fwd",
    )(*args)


def _block_diag(x):
    nblk, g, r, c = x.shape
    eye = jnp.eye(g, dtype=x.dtype)
    return jnp.einsum('jgrc,gh->jgrhc', x, eye).reshape(nblk, g * r, g * c)


def s5_tables(lam_re, lam_im, log_dt, b_re, b_im, c_re, c_im):
    lam_re = jnp.minimum(lam_re.astype(F32), -1e-4)
    lam_im = lam_im.astype(F32)
    dt = jnp.exp(log_dt.astype(F32))[:, None]
    mag = jnp.exp(lam_re * dt)
    lb_re = mag * jnp.cos(lam_im * dt)
    lb_im = mag * jnp.sin(lam_im * dt)
    den = lam_re * lam_re + lam_im * lam_im
    f_re = ((lb_re - 1.0) * lam_re + lb_im * lam_im) / den
    f_im = (lb_im * lam_re - (lb_re - 1.0) * lam_im) / den
    b_re = b_re.astype(F32)
    b_im = b_im.astype(F32)
    bb_re = f_re[..., None] * b_re - f_im[..., None] * b_im
    bb_im = f_re[..., None] * b_im + f_im[..., None] * b_re
    gpb = S5_KB // S5_GROUP
    to_b = lambda x: _block_diag(
        jnp.swapaxes(x, 1, 2).reshape(S5_NB, gpb, S5_GROUP, S5_STATE)).astype(BF16)
    to_c = lambda x: _block_diag(
        jnp.swapaxes(x.astype(F32), 1, 2).reshape(S5_NB, gpb, S5_STATE, S5_GROUP)).astype(BF16)
    return (to_b(bb_re), to_b(bb_im), lb_re.reshape(1, S5_W), lb_im.reshape(1, S5_W),
            to_c(c_re), to_c(-c_im.astype(F32)))


def dft_tables(n):
    nn = 2 * n
    f = jnp.arange(n, dtype=jnp.int32)[:, None]
    t = jnp.arange(n, dtype=jnp.int32)[None, :]
    ang = ((f * t) % nn).astype(F32) * (2.0 * math.pi / nn)
    cos = jnp.cos(ang)
    nyq = jnp.where(t % 2 == 0, 1.0, -1.0).astype(F32)
    msin = jnp.where(f == 0, nyq, -jnp.sin(ang))
    fwd = jnp.concatenate([cos, msin], axis=0)
    wf = jnp.where(jnp.arange(n) == 0, 1.0 / nn, 2.0 / nn).astype(F32)
    return fwd, wf


def _filtdft_kernel(fh_ref, fl_ref, hh_ref, hl_ref, o_ref):
    fh = fh_ref[...]
    o_ref[...] = _dot(fh, hh_ref[...]) + _dot(fh, hl_ref[...]) + _dot(fl_ref[...], hh_ref[...])


def filter_dft(fwd_hi, fwd_lo, h):
    n2, n = fwd_hi.shape
    c = h.shape[1]
    tr = min(256, n2)
    tcn = 512
    hh, hl = _split_bf16(h)
    return pl.pallas_call(
        _filtdft_kernel,
        grid=(n2 // tr, c // tcn),
        in_specs=[pl.BlockSpec((tr, n), lambda i, j: (i, 0)),
                  pl.BlockSpec((tr, n), lambda i, j: (i, 0)),
                  pl.BlockSpec((n, tcn), lambda i, j: (0, j)),
                  pl.BlockSpec((n, tcn), lambda i, j: (0, j))],
        out_specs=pl.BlockSpec((tr, tcn), lambda i, j: (i, j)),
        out_shape=jax.ShapeDtypeStruct((n2, c), F32),
        compiler_params=_params("arbitrary", "arbitrary"),
        name="filter_dft",
    )(fwd_hi, fwd_lo, hh, hl)


def hyena_filter_response(n, fwd_hi, fwd_lo, wf, f_w1, f_b1, f_w2, f_b2, f_w3, f_freq, log_decay):
    t = jnp.linspace(0.0, 1.0, n, dtype=F32)[:, None]
    w = 2.0 * math.pi * jnp.arange(n, dtype=F32)[:, None] / n
    bands = jnp.linspace(1e-4, HY_BANDS - 1, HY_BANDS, dtype=F32)[None, :]
    z = jnp.concatenate([t, jnp.cos(bands * w), -jnp.sin(bands * w)], axis=-1)
    freq = f_freq.astype(F32)
    hid = jnp.sin(freq[0] * (jnp.dot(z, f_w1.astype(F32), precision=HIGHEST) + f_b1.astype(F32)))
    hid = jnp.sin(freq[1] * (jnp.dot(hid, f_w2.astype(F32), precision=HIGHEST) + f_b2.astype(F32)))
    h = jnp.dot(hid, f_w3.astype(F32), precision=HIGHEST).reshape(n, 2, HY_ORDER, GROUP_W)
    h = h * jnp.exp(-t[:, :, None, None] * jnp.exp(log_decay.astype(F32)))
    cw = HY_ORDER * GROUP_W
    h_fwd = h[:, 0].reshape(n, cw)
    h_bwd = h[:, 1].reshape(n, cw).at[0].set(0.0)
    resp = filter_dft(fwd_hi, fwd_lo, jnp.concatenate([h_fwd, h_bwd], axis=1))
    a_re, a_im = resp[:n, :cw], resp[n:, :cw]
    b_re, b_im = resp[:n, cw:], resp[n:, cw:]
    k_re = a_re + b_re
    first = (jnp.arange(n) == 0)[:, None]
    k_im = jnp.where(first, a_im + b_im, a_im - b_im)
    k_re = (k_re * wf[:, None]).reshape(n, HY_ORDER, GROUP_W).transpose(1, 0, 2)
    k_im = (k_im * wf[:, None]).reshape(n, HY_ORDER, GROUP_W).transpose(1, 0, 2)
    return k_re, k_im


def _hyena_kernel(z0_ref, x1_ref, x2_ref, fc_ref, fs_ref, gc_ref, gs_ref, kr_ref, ki_ref, bias_ref,
                  o_ref, zb_s, zf_s, acc_s):
    o = pl.program_id(1)
    f = pl.program_id(2)
    nf = pl.num_programs(2)

    @pl.when((o == 0) & (f == 0))
    def _():
        z0 = z0_ref[0]
        zf_s[...] = z0
        zb_s[...] = z0.astype(BF16)

    @pl.when(f == 0)
    def _():
        acc_s[...] = jnp.zeros_like(acc_s)

    zb = zb_s[...]
    xr = _dot(fc_ref[...], zb)
    xi = _dot(fs_ref[...], zb)
    kr = kr_ref[0]
    ki = ki_ref[0]
    row = lax.broadcasted_iota(jnp.int32, (xr.shape[0], 1), 0)
    packed = (row == 0) & (f == 0)
    yr = jnp.where(packed, xr * kr, xr * kr - xi * ki)
    yi = jnp.where(packed, xi * ki, xr * ki + xi * kr)
    acc_s[...] += _dot(gc_ref[...], yr.astype(BF16)) + _dot(gs_ref[...], yi.astype(BF16))

    @pl.when(f == nf - 1)
    def _():
        bias = bias_ref[...]

        @pl.when(o == 0)
        def _():
            z1 = x1_ref[0] * (acc_s[...] + zf_s[...] * bias[0:1])
            zf_s[...] = z1
            zb_s[...] = z1.astype(BF16)

        @pl.when(o == 1)
        def _():
            o_ref[0] = x2_ref[0] * (acc_s[...] + zf_s[...] * bias[1:2])


def hyena_call(pc, fwd_hi, fwd_t, k_re, k_im, bias, out_prev, t_total, row_blk):
    b, n, _ = pc.shape
    c = GROUP_W
    fb = min(256, n)
    nf = n // fb
    once = pl.Buffered(1)
    in_specs = [pl.BlockSpec((1, n, c), lambda bb, o, f: (bb, 0, 0), pipeline_mode=once),
                pl.BlockSpec((1, n, c), lambda bb, o, f: (bb, 0, 1), pipeline_mode=once),
                pl.BlockSpec((1, n, c), lambda bb, o, f: (bb, 0, 2), pipeline_mode=once),
                pl.BlockSpec((fb, n), lambda bb, o, f: (f, 0)),
                pl.BlockSpec((fb, n), lambda bb, o, f: (nf + f, 0)),
                pl.BlockSpec((n, fb), lambda bb, o, f: (0, f)),
                pl.BlockSpec((n, fb), lambda bb, o, f: (0, nf + f)),
                pl.BlockSpec((1, fb, c), lambda bb, o, f: (o, f, 0)),
                pl.BlockSpec((1, fb, c), lambda bb, o, f: (o, f, 0)),
                pl.BlockSpec((HY_ORDER, c), lambda bb, o, f: (0, 0))]
    args = [pc, pc, pc, fwd_hi, fwd_hi, fwd_t, fwd_t, k_re, k_im, bias]
    aliases = {}
    if out_prev is not None:
        in_specs.append(pl.BlockSpec(memory_space=pl.ANY))
        args.append(out_prev)
        aliases = {len(args) - 1: 0}
        kern = lambda *r: _hyena_kernel(*r[:10], *r[11:])
    else:
        kern = _hyena_kernel
    return pl.pallas_call(
        kern,
        grid=(b, HY_ORDER, nf),
        in_specs=in_specs,
        out_specs=pl.BlockSpec((1, n, c), lambda bb, o, f: (bb, row_blk, 0)),
        out_shape=jax.ShapeDtypeStruct((b, t_total, c), F32),
        scratch_shapes=[pltpu.VMEM((n, c), BF16), pltpu.VMEM((n, c), F32), pltpu.VMEM((n, c), F32)],
        input_output_aliases=aliases,
        compiler_params=_params("arbitrary", "arbitrary", "arbitrary"),
        name="hyena",
    )(*args)


def _rwkv_lora_kernel(p_ref, ww_ref, wa_ref, wg_ref, w0_ref, a0_ref, wp0_ref, wp1_ref, ap0_ref, ap1_ref,
                      g_ref):
    x = p_ref[0]
    wa_in = x[:, :LANE]
    th = jnp.tanh(wa_in).astype(BF16)
    lin = wa_in.astype(BF16)
    w0 = w0_ref[...]
    a0 = a0_ref[...]
    wp0_ref[0] = w0[0:1] + _dot(th, ww_ref[0])
    wp1_ref[0] = w0[1:2] + _dot(th, ww_ref[1])
    ap0_ref[0] = a0[0:1] + _dot(lin, wa_ref[0])
    ap1_ref[0] = a0[1:2] + _dot(lin, wa_ref[1])
    g_ref[0] = _dot(jax.nn.sigmoid(x[:, LANE:]).astype(BF16), wg_ref[...])


def rwkv_lora(p, w_up, a_up, g_up, w0, a0):
    b, t, _ = p.shape
    c = GROUP_W
    tm = TOKEN_TILE
    zeros = jnp.zeros((2, 64, c), F32)
    ww = jnp.concatenate([w_up.astype(F32), zeros], axis=1).astype(BF16)
    wa = jnp.concatenate([zeros, a_up.astype(F32)], axis=1).astype(BF16)
    out = jax.ShapeDtypeStruct((b, t, c), F32)
    ospec = pl.BlockSpec((1, tm, c), lambda bb, i: (bb, i, 0))
    return pl.pallas_call(
        _rwkv_lora_kernel,
        grid=(b, t // tm),
        in_specs=[pl.BlockSpec((1, tm, RW_LORA_W), lambda bb, i: (bb, i, 3 * c // RW_LORA_W)),
                  pl.BlockSpec((2, LANE, c), lambda bb, i: (0, 0, 0)),
                  pl.BlockSpec((2, LANE, c), lambda bb, i: (0, 0, 0)),
                  pl.BlockSpec((LANE, c), lambda bb, i: (0, 0)),
                  pl.BlockSpec((2, c), lambda bb, i: (0, 0)),
                  pl.BlockSpec((2, c), lambda bb, i: (0, 0))],
        out_specs=[ospec] * 5,
        out_shape=[out] * 5,
        compiler_params=_params("arbitrary", "arbitrary"),
        name="rwkv_lora",
    )(p, ww, wa, g_up.astype(BF16), w0, a0)


def _rwkv_scan_kernel(wpf_ref, wpb_ref, apf_ref, apb_ref, kf_ref, kb_ref, vf_ref, vb_ref, rf_ref, rb_ref,
                      kkc_ref, kac_ref, rkc_ref, yf_ref, yb_ref, bonf_ref, bonb_ref,
                      s_ref, w_s, kk_s, b_s, ke_s, r_s, *, tb):
    n = RW_HEAD
    is_fwd = lax.broadcasted_iota(jnp.int32, (1, LANE), 1) < LANE // 2

    @pl.when(pl.program_id(0) == 0)
    def _():
        s_ref[...] = jnp.zeros_like(s_ref)

    def step(t, carry):
        tr = tb - 1 - t
        k = jnp.where(is_fwd, kf_ref[t], kb_ref[tr])
        a = jax.nn.sigmoid(jnp.where(is_fwd, apf_ref[t], apb_ref[tr]))
        kk = k * kkc_ref[...]
        nrm = jnp.sqrt(jnp.sum(kk * kk, axis=0, keepdims=True))
        kk = kk / jnp.maximum(nrm, 1e-12)
        ke = k * (1.0 + (a - 1.0) * kac_ref[...])
        wl = -jax.nn.softplus(-jnp.where(is_fwd, wpf_ref[t], wpb_ref[tr])) - 0.5
        r = jnp.where(is_fwd, rf_ref[t], rb_ref[tr])
        v = jnp.where(is_fwd, vf_ref[t], vb_ref[tr])
        w_s[...] = jnp.exp(-jnp.exp(wl))
        kk_s[...] = kk
        b_s[...] = kk * a
        ke_s[...] = ke
        r_s[...] = r
        bon = jnp.sum(r * ke * rkc_ref[...], axis=0, keepdims=True)
        bonf_ref[pl.ds(t, 1), :] = bon
        bonb_ref[pl.ds(tr, 1), :] = bon

        def p1(j, sa):
            return sa + s_ref[j] * kk_s[pl.ds(j, 1), :]

        sa = lax.fori_loop(0, n, p1, jnp.zeros((n, LANE), F32), unroll=8)

        def p2(j, y):
            s = s_ref[j] * w_s[pl.ds(j, 1), :] - sa * b_s[pl.ds(j, 1), :] + v * ke_s[pl.ds(j, 1), :]
            s_ref[j] = s
            return y + s * r_s[pl.ds(j, 1), :]

        y = lax.fori_loop(0, n, p2, jnp.zeros((n, LANE), F32), unroll=8)
        yf_ref[t] = y
        yb_ref[tr] = y
        return carry

    lax.fori_loop(0, tb, step, 0)


def rwkv_scan(wp0, wp1, ap0, ap1, k, v, r, kkc, kac, rkc, n_lat):
    t = k.shape[0]
    tb = 32
    n = RW_HEAD
    n_blk = t // tb
    n_lat_blk = n_lat // tb
    n_ctx_blk = n_blk - n_lat_blk

    def fblk(c):
        return jnp.where(c < n_ctx_blk, n_lat_blk + c, c - n_ctx_blk)

    def rblk(c):
        return n_blk - 1 - c

    fwd = pl.BlockSpec((tb, n, LANE), lambda c: (fblk(c), 0, 0))
    bwd = pl.BlockSpec((tb, n, LANE), lambda c: (rblk(c), 0, 0))
    cst = pl.BlockSpec((n, LANE), lambda c: (0, 0))
    tile = pltpu.VMEM((n, LANE), F32)
    return pl.pallas_call(
        functools.partial(_rwkv_scan_kernel, tb=tb),
        grid=(n_blk,),
        in_specs=[fwd, bwd] * 5 + [cst] * 3,
        out_specs=[fwd, bwd, pl.BlockSpec((tb, LANE), lambda c: (fblk(c), 0)),
                   pl.BlockSpec((tb, LANE), lambda c: (rblk(c), 0))],
        out_shape=[jax.ShapeDtypeStruct((t, n, LANE), F32), jax.ShapeDtypeStruct((t, n, LANE), F32),
                   jax.ShapeDtypeStruct((t, LANE), F32), jax.ShapeDtypeStruct((t, LANE), F32)],
        scratch_shapes=[pltpu.VMEM((n, n, LANE), F32), tile, tile, tile, tile, tile],
        compiler_params=_params("arbitrary"),
        name="rwkv_scan",
    )(wp0, wp1, ap0, ap1, k, k, v, v, r, r, kkc, kac, rkc)


def _rwkv_post_kernel(y_ref, bon_ref, v_ref, g_ref, m_ref, lg_ref, lb_ref, o_ref):
    m = m_ref[...]

    def head_mean(x):
        hi, lo = _split_bf16(x)
        return _dot(hi, m) + _dot(lo, m)

    y = y_ref[0]
    d = y - head_mean(y)
    var = head_mean(d * d)
    yn = d * lax.rsqrt(var + RW_LN_EPS) * lg_ref[...] + lb_ref[...]
    o_ref[0] = (yn + bon_ref[0] * v_ref[0]) * g_ref[0]


def rwkv_post(y, bon, rkv, gate, ln_g, ln_b, n_rows):
    b, t, c = y.shape
    tm = TOKEN_TILE
    hm = jnp.kron(jnp.eye(RW_HEADS, dtype=F32), jnp.full((RW_HEAD, RW_HEAD), 1.0 / RW_HEAD, F32)).astype(BF16)
    tok = pl.BlockSpec((1, tm, c), lambda bb, i: (bb, i, 0))
    vec = pl.BlockSpec((1, c), lambda bb, i: (0, 0))
    return pl.pallas_call(
        _rwkv_post_kernel,
        grid=(b, n_rows // tm),
        in_specs=[tok, tok, pl.BlockSpec((1, tm, c), lambda bb, i: (bb, i, 2)), tok,
                  pl.BlockSpec((c, c), lambda bb, i: (0, 0)), vec, vec],
        out_specs=tok,
        out_shape=jax.ShapeDtypeStruct((b, t, c), F32),
        compiler_params=_params("arbitrary", "arbitrary"),
        name="rwkv_post",
    )(y, bon, rkv, gate, hm, ln_g.reshape(1, c), ln_b.reshape(1, c))


def _to_scan_layout(a):
    b, t, _ = a.shape
    x = a.reshape(b, t, RW_HEADS, RW_HEAD).transpose(1, 3, 0, 2).reshape(t, RW_HEAD, b * RW_HEADS)
    return jnp.concatenate([x, x], axis=-1)


def _from_scan_layout(yf, yb, b):
    t = yf.shape[0]
    half = b * RW_HEADS
    y = yf[..., :half] + yb[..., half:]
    if y.ndim == 3:
        return y.reshape(t, RW_HEAD, b, RW_HEADS).transpose(2, 0, 3, 1).reshape(b, t, GROUP_W)
    return jnp.repeat(y.reshape(t, b, RW_HEADS).transpose(1, 0, 2), RW_HEAD, axis=-1)


def _chain_const(x, b):
    return jnp.tile(x.astype(F32).reshape(RW_HEADS, RW_HEAD).T, (1, 2 * b))


def _rope(x, cos, sin):
    lane = lax.broadcasted_iota(jnp.int32, (1, LANE), 1)
    first = (lane % (2 * ROPE_FREQS)) < ROPE_FREQS
    partner = jnp.where(first, pltpu.roll(x, LANE - ROPE_FREQS, 1), pltpu.roll(x, ROPE_FREQS, 1))
    return x * cos + partner * sin


def _attn_tile(q, k, v, lam):
    lane = lax.broadcasted_iota(jnp.int32, (1, LANE), 1)
    m0 = lane < DA_HEAD
    q0 = jnp.where(m0, q, 0.0).astype(BF16)
    q1 = jnp.where(m0, 0.0, q).astype(BF16)
    dn = (((1,), (1,)), ((), ()))
    s0 = lax.dot_general(q0, k, dn, preferred_element_type=F32)
    s1 = lax.dot_general(q1, k, dn, preferred_element_type=F32)
    p0 = jnp.exp(s0 - jnp.max(s0, axis=-1, keepdims=True))
    p1 = jnp.exp(s1 - jnp.max(s1, axis=-1, keepdims=True))
    w = p0 / jnp.sum(p0, axis=-1, keepdims=True) - lam * (p1 / jnp.sum(p1, axis=-1, keepdims=True))
    return _dot(w.astype(BF16), v)


def _attn_kernel(q_ref, k_ref, v_ref, cq_ref, sq_ref, ck_ref, sk_ref, lp_ref, g_ref, o_ref,
                 kr_s, vb_s, *, n_lat, n_lat_tiles, lam_init):
    i = pl.program_id(2)

    @pl.when(i == 0)
    def _():
        kr_s[...] = _rope(k_ref[0], ck_ref[...], sk_ref[...]).astype(BF16)
        vb_s[...] = v_ref[0].astype(BF16)

    lp = lp_ref[...]
    lam = (jnp.exp(jnp.sum(lp[0:1] * lp[1:2], axis=-1, keepdims=True))
           - jnp.exp(jnp.sum(lp[2:3] * lp[3:4], axis=-1, keepdims=True)) + lam_init)
    q = _rope(q_ref[0], cq_ref[...], sq_ref[...]) * DA_SCALE

    def finish(o):
        on = o * lax.rsqrt(jnp.mean(o * o, axis=-1, keepdims=True) + DA_SUBLN_EPS)
        o_ref[0] = on * g_ref[...] * (1.0 - lam_init)

    @pl.when(i < n_lat_tiles)
    def _():
        finish(_attn_tile(q, kr_s[...], vb_s[...], lam))

    @pl.when(i >= n_lat_tiles)
    def _():
        finish(_attn_tile(q, kr_s[n_lat:], vb_s[n_lat:], lam))


def diff_attention(p, col0, cosf, sins, lam_p, subln_g, lam_init, n_lat, n_rows):
    b, t, _ = p.shape
    tq = TOKEN_TILE
    off = col0 // LANE
    hq = DA_HEADS
    return pl.pallas_call(
        functools.partial(_attn_kernel, n_lat=n_lat, n_lat_tiles=n_lat // tq, lam_init=lam_init),
        grid=(b, hq, n_rows // tq),
        in_specs=[pl.BlockSpec((1, tq, LANE), lambda bb, h, i: (bb, i, off + h)),
                  pl.BlockSpec((1, t, LANE), lambda bb, h, i: (bb, 0, off + hq + h)),
                  pl.BlockSpec((1, t, LANE), lambda bb, h, i: (bb, 0, off + 2 * hq + h)),
                  pl.BlockSpec((tq, LANE), lambda bb, h, i: (i, 0)),
                  pl.BlockSpec((tq, LANE), lambda bb, h, i: (i, 0)),
                  pl.BlockSpec((t, LANE), lambda bb, h, i: (0, 0)),
                  pl.BlockSpec((t, LANE), lambda bb, h, i: (0, 0)),
                  pl.BlockSpec((4, DA_HEAD), lambda bb, h, i: (0, 0)),
                  pl.BlockSpec((1, LANE), lambda bb, h, i: (0, 0))],
        out_specs=pl.BlockSpec((1, tq, LANE), lambda bb, h, i: (bb, i, h)),
        out_shape=jax.ShapeDtypeStruct((b, t, hq * DA_V), F32),
        scratch_shapes=[pltpu.VMEM((t, LANE), BF16), pltpu.VMEM((t, LANE), BF16)],
        compiler_params=_params("arbitrary", "arbitrary", "arbitrary"),
        name="diff_attention",
    )(p, p, p, cosf, sins, cosf, sins, lam_p, subln_g.reshape(1, DA_V))


def rope_tables(n_lat, n_ctx):
    rows = n_lat // GRID_W
    row = jnp.repeat(jnp.arange(rows, dtype=F32), GRID_W)
    col = jnp.tile(jnp.arange(GRID_W, dtype=F32), rows)
    inv = ROPE_BASE ** (-jnp.arange(ROPE_FREQS, dtype=F32) / ROPE_FREQS)
    ang = jnp.stack([row[:, None] * inv, col[:, None] * inv], axis=1)
    cos, sin = jnp.cos(ang), jnp.sin(ang)
    cosf = jnp.concatenate([cos, cos], axis=-1).reshape(n_lat, DA_HEAD)
    sins = jnp.concatenate([-sin, sin], axis=-1).reshape(n_lat, DA_HEAD)
    cosf = jnp.concatenate([jnp.tile(cosf, (1, 2)), jnp.ones((n_ctx, LANE), F32)], axis=0)
    sins = jnp.concatenate([jnp.tile(sins, (1, 2)), jnp.zeros((n_ctx, LANE), F32)], axis=0)
    return cosf, sins


def _even_layer(xs, mods, keep_ctx, n_lat, norm1_g, w_in, w_out, s5p, hyp, dft):
    b, t, _ = xs.shape
    n_ctx = t - n_lat
    n_rows = t if keep_ctx else n_lat
    u_t, p_h = in_proj(xs, norm1_g, mods, w_in.astype(BF16), n_lat, GROUP_W)

    (lam_re, lam_im, log_dt, b_re, b_im, c_re, c_im, d_skip, glu_w, glu_b) = s5p
    u2 = u_t.reshape(t * b, GROUP_W)
    tabs = [s5_tables(lam_re[d], lam_im[d], log_dt[d], b_re[d], b_im[d], c_re[d], c_im[d]) for d in range(2)]
    y_f = s5_scan_call(u2, tabs[0], n_lat, t, b, False, None)
    a_t = s5_scan_call(u2, tabs[1], n_lat, t, b, True,
                       (y_f, d_skip.reshape(1, GROUP_W), glu_w.astype(BF16), glu_b.reshape(1, GROUP_W)))
    a_t = a_t.reshape(t, b * GROUP_W)

    (conv_w, conv_b, f_w1, f_b1, f_w2, f_b2, f_w3, f_freq, log_decay, bias) = hyp
    pc_l, pc_c = short_conv(p_h, 0, conv_w, conv_b, n_lat, True)
    fh_l, fl_l, ft_l, wf_l = dft[0]
    kr, ki = hyena_filter_response(n_lat, fh_l, fl_l, wf_l, f_w1, f_b1, f_w2, f_b2, f_w3, f_freq, log_decay)
    b_m = hyena_call(pc_l, fh_l, ft_l, kr, ki, bias, None, t, 0)
    if keep_ctx:
        fh_c, fl_c, ft_c, wf_c = dft[1]
        kr, ki = hyena_filter_response(n_ctx, fh_c, fl_c, wf_c, f_w1, f_b1, f_w2, f_b2, f_w3, f_freq,
                                       log_decay)
        b_m = hyena_call(pc_c, fh_c, ft_c, kr, ki, bias, b_m, t, n_lat // n_ctx)
    return out_proj(xs, a_t, b_m, w_out.astype(BF16), mods, n_lat, n_rows, True)


def _odd_layer(xs, mods, keep_ctx, n_lat, lam_init, norm1_g, w_in, w_out, rwp, dap, rope):
    b, t, _ = xs.shape
    n_rows = t if keep_ctx else n_lat
    (p,) = in_proj(xs, norm1_g, mods, w_in.astype(BF16), n_lat, 0)

    (conv_w, w0, w_up, a0, a_up, g_up, k_k, k_a, r_k, ln_g, ln_b) = rwp
    (rkv,) = short_conv(p, 0, conv_w, jnp.zeros((3 * GROUP_W,), F32), n_lat, False)
    r, k, v = (_to_scan_layout(rkv[..., i * GROUP_W:(i + 1) * GROUP_W]) for i in range(3))
    wp0, wp1, ap0, ap1, gate = rwkv_lora(p, w_up, a_up, g_up, w0, a0)
    yf, yb, bonf, bonb = rwkv_scan(_to_scan_layout(wp0), _to_scan_layout(wp1), _to_scan_layout(ap0),
                                   _to_scan_layout(ap1), k, v, r, _chain_const(k_k, b),
                                   _chain_const(k_a, b), _chain_const(r_k, b), n_lat)
    a_m = rwkv_post(_from_scan_layout(yf, yb, b), _from_scan_layout(bonf, bonb, b), rkv, gate,
                    ln_g, ln_b, n_rows)

    lam_p, subln_g = dap
    b_m = diff_attention(p, RW_IN, rope[0], rope[1], lam_p, subln_g, lam_init, n_lat, n_rows)
    return out_proj(xs, a_m, b_m, w_out.astype(BF16), mods, n_lat, n_rows, False)


def kernel(x, c, ctx, c_ctx, ada_w, ada_b, norm1_g, norm2_g, mlp_w1, mlp_w2, final_g, ev_w_in, ev_w_out, s5_lam_re, s5_lam_im, s5_log_dt, s5_b_re, s5_b_im, s5_c_re, s5_c_im, s5_d, s5_glu_w, s5_glu_b, hy_conv_w, hy_conv_b, hy_f_w1, hy_f_b1, hy_f_w2, hy_f_b2, hy_f_w3, hy_f_freq, hy_log_decay, hy_bias, od_w_in, od_w_out, rw_conv_w, rw_w0, rw_w_up, rw_a0, rw_a_up, rw_g_up, rw_k_k, rw_k_a, rw_r_k, rw_ln_g, rw_ln_b, da_lam, da_subln_g):
    b, n_lat, d = x.shape
    n_ctx = ctx.shape[1]
    assert d == D_MODEL and b == SUBLANE
    assert n_lat % TOKEN_TILE == 0 and n_ctx % TOKEN_TILE == 0 and n_lat % n_ctx == 0
    xs = jnp.concatenate([x.astype(F32), ctx.astype(F32)], axis=1)

    c_rows = jnp.zeros((16, d), F32).at[:b].set(c.astype(F32)).at[b].set(c_ctx.astype(F32))
    depth = ada_w.shape[0]
    mods_all = ada_mods(c_rows, ada_w, ada_b).reshape(depth, 16, 6, d)

    rope = rope_tables(n_lat, n_ctx)
    dft = []
    for n in (n_lat, n_ctx):
        fwd, wf = dft_tables(n)
        hi, lo = _split_bf16(fwd)
        dft.append((hi, lo, hi.T, wf))

    for l in range(depth):
        keep_ctx = l < depth - 1
        i = l // 2
        mods = mods_all[l]
        if l % 2 == 0:
            s5p = (s5_lam_re[i], s5_lam_im[i], s5_log_dt[i], s5_b_re[i], s5_b_im[i], s5_c_re[i], s5_c_im[i],
                   s5_d[i], s5_glu_w[i], s5_glu_b[i])
            hyp = (hy_conv_w[i], hy_conv_b[i], hy_f_w1[i], hy_f_b1[i], hy_f_w2[i], hy_f_b2[i], hy_f_w3[i],
                   hy_f_freq[i], hy_log_decay[i], hy_bias[i])
            xs = _even_layer(xs, mods, keep_ctx, n_lat, norm1_g[l], ev_w_in[i], ev_w_out[i], s5p, hyp, dft)
        else:
            rwp = (rw_conv_w[i], rw_w0[i], rw_w_up[i], rw_a0[i], rw_a_up[i], rw_g_up[i], rw_k_k[i],
                   rw_k_a[i], rw_r_k[i], rw_ln_g[i], rw_ln_b[i])
            lam_init = 0.8 - 0.6 * math.exp(-0.3 * l)
            xs = _odd_layer(xs, mods, keep_ctx, n_lat, lam_init, norm1_g[l], od_w_in[i], od_w_out[i], rwp,
                            (da_lam[i], da_subln_g[i]), rope)
        n_rows = xs.shape[1]
        xs = mlp(xs, norm2_g[l], mods, mlp_w1[l].astype(BF16), mlp_w2[l].astype(BF16), final_g,
                 n_lat, n_rows, l == depth - 1)
    return xs
```

```python
import functools
import math

import jax
import jax.numpy as jnp
from jax import lax
from jax.experimental import pallas as pl
from jax.experimental.pallas import tpu as pltpu

F32 = jnp.float32
BF16 = jnp.bfloat16
HIGHEST = lax.Precision.HIGHEST

D_MODEL = 1024
DEPTH = 4
GRID_W = 64
MLP_HIDDEN = 4 * D_MODEL
GROUP_W = D_MODEL // 2
EPS = 1e-6

S5_GROUP = 16
S5_GROUPS = GROUP_W // S5_GROUP
S5_STATE = 64
S5_W = S5_GROUPS * S5_STATE

HY_ORDER = 2
HY_EMB = 33
HY_BANDS = (HY_EMB - 1) // 2
SHORT_K = 3

RW_HEAD = 64
RW_HEADS = GROUP_W // RW_HEAD
RW_LORA_W = 256
RW_LN_EPS = 64e-5
RW_IN = 3 * GROUP_W + RW_LORA_W

DA_HEADS = 4
DA_HEAD = 64
DA_V = 2 * DA_HEAD
DA_SCALE = DA_HEAD ** -0.5
DA_SUBLN_EPS = 1e-5
ROPE_BASE = 10000.0
ROPE_FREQS = DA_HEAD // 4

LANE = 128
SUBLANE = 8
TOKEN_TILE = 256
VMEM_LIMIT = 48 * 1024 * 1024


def _params(*sem):
    return pltpu.CompilerParams(dimension_semantics=sem, vmem_limit_bytes=VMEM_LIMIT)


def _split_bf16(x):
    hi = x.astype(BF16)
    lo = (x - hi.astype(F32)).astype(BF16)
    return hi, lo


def _dot(a, b):
    return jnp.dot(a, b, preferred_element_type=F32)


def _ada_kernel(c_ref, w_ref, b_ref, o_ref):
    c = c_ref[...]
    s = c * jax.nn.sigmoid(c)
    o_ref[0] = jnp.dot(s, w_ref[0], preferred_element_type=F32, precision=HIGHEST) + b_ref[0]


def ada_mods(c_rows, ada_w, ada_b):
    depth, d, n = ada_w.shape
    tn = 1024
    return pl.pallas_call(
        _ada_kernel,
        grid=(depth, n // tn),
        in_specs=[pl.BlockSpec((16, d), lambda l, j: (0, 0)),
                  pl.BlockSpec((1, d, tn), lambda l, j: (l, 0, j)),
                  pl.BlockSpec((1, 1, tn), lambda l, j: (l, 0, j))],
        out_specs=pl.BlockSpec((1, 16, tn), lambda l, j: (l, 0, j)),
        out_shape=jax.ShapeDtypeStruct((depth, 16, n), F32),
        compiler_params=_params("arbitrary", "arbitrary"),
        name="ada_mods",
    )(c_rows, ada_w, ada_b.reshape(depth, 1, n))


def _norm_mod(x, g, ml, mc, is_ctx, k):
    ms = jnp.mean(x * x, axis=-1, keepdims=True)
    xn = x * lax.rsqrt(ms + EPS) * g
    m = jnp.where(is_ctx, mc, ml)
    return xn * (1.0 + m[k + 1:k + 2]) + m[k:k + 1]


def _inproj_kernel(x_ref, g_ref, ml_ref, mc_ref, w_ref, *o_refs, n_lat_tiles, n_tmaj):
    i = pl.program_id(1)
    h = _norm_mod(x_ref[0], g_ref[...], ml_ref[0], mc_ref[0], i >= n_lat_tiles, 0)
    r = _dot(h.astype(BF16), w_ref[...])
    if n_tmaj:
        o_refs[0][...] = r[:, :n_tmaj]
        o_refs[1][0] = r[:, n_tmaj:]
    else:
        o_refs[0][0] = r


def in_proj(x, g, mods, w, n_lat, n_tmaj):
    b, t, d = x.shape
    n = w.shape[1]
    tm = TOKEN_TILE
    if n_tmaj:
        out_shape = [jax.ShapeDtypeStruct((t, b * n_tmaj), F32), jax.ShapeDtypeStruct((b, t, n - n_tmaj), F32)]
        out_specs = [pl.BlockSpec((tm, n_tmaj), lambda bb, i: (i, bb)),
                     pl.BlockSpec((1, tm, n - n_tmaj), lambda bb, i: (bb, i, 0))]
    else:
        out_shape = [jax.ShapeDtypeStruct((b, t, n), F32)]
        out_specs = [pl.BlockSpec((1, tm, n), lambda bb, i: (bb, i, 0))]
    return pl.pallas_call(
        functools.partial(_inproj_kernel, n_lat_tiles=n_lat // tm, n_tmaj=n_tmaj),
        grid=(b, t // tm),
        in_specs=[pl.BlockSpec((1, tm, d), lambda bb, i: (bb, i, 0)),
                  pl.BlockSpec((1, d), lambda bb, i: (0, 0)),
                  pl.BlockSpec((1, 6, d), lambda bb, i: (bb, 0, 0)),
                  pl.BlockSpec((1, 6, d), lambda bb, i: (b, 0, 0)),
                  pl.BlockSpec((d, n), lambda bb, i: (0, 0), pipeline_mode=pl.Buffered(1))],
        out_specs=out_specs,
        out_shape=out_shape,
        compiler_params=_params("arbitrary", "arbitrary"),
        name="in_proj",
    )(x, g.reshape(1, d), mods, mods, w)


MLP_CHUNK = 1024


def _mix_mlp_kernel(x_ref, a_ref, b_ref, wa_ref, wb_ref, g_ref, ml_ref, mc_ref, w1_ref, w2_ref, fg_ref,
                    o_ref, acc_ref, *, n_lat_tiles, a_tmaj, final):
    i = pl.program_id(1)
    is_ctx = i >= n_lat_tiles
    m = jnp.where(is_ctx, mc_ref[0], ml_ref[0])
    a = a_ref[...] if a_tmaj else a_ref[0]
    mix = _dot(a.astype(BF16), wa_ref[...]) + _dot(b_ref[0].astype(BF16), wb_ref[...])
    x = x_ref[0] + m[2:3] * mix
    h = _norm_mod(x, g_ref[...], ml_ref[0], mc_ref[0], is_ctx, 3).astype(BF16)
    for k in range(w1_ref.shape[1] // MLP_CHUNK):
        ks = slice(k * MLP_CHUNK, (k + 1) * MLP_CHUNK)
        hid = jnp.maximum(_dot(h, w1_ref[:, ks]), 0.0)
        part = _dot((hid * hid).astype(BF16), w2_ref[ks, :])
        if k == 0:
            acc_ref[...] = part
        else:
            acc_ref[...] += part
    y = x + m[5:6] * acc_ref[...]
    if final:
        ms = jnp.mean(y * y, axis=-1, keepdims=True)
        y = y * lax.rsqrt(ms + EPS) * fg_ref[...]
    o_ref[0] = y


def mix_mlp(x, a, bm, w_out, g, mods, w1, w2, final_g, n_lat, n_rows, a_tmaj, final):
    b, t, d = x.shape
    wd = bm.shape[-1]
    hid = w1.shape[1]
    tm = TOKEN_TILE
    once = pl.Buffered(1)
    if a_tmaj:
        a_spec = pl.BlockSpec((tm, wd), lambda bb, i: (i, bb))
    else:
        a_spec = pl.BlockSpec((1, tm, wd), lambda bb, i: (bb, i, 0))
    return pl.pallas_call(
        functools.partial(_mix_mlp_kernel, n_lat_tiles=n_lat // tm, a_tmaj=a_tmaj, final=final),
        grid=(b, n_rows // tm),
        in_specs=[pl.BlockSpec((1, tm, d), lambda bb, i: (bb, i, 0)),
                  a_spec,
                  pl.BlockSpec((1, tm, wd), lambda bb, i: (bb, i, 0)),
                  pl.BlockSpec((wd, d), lambda bb, i: (0, 0), pipeline_mode=once),
                  pl.BlockSpec((wd, d), lambda bb, i: (1, 0), pipeline_mode=once),
                  pl.BlockSpec((1, d), lambda bb, i: (0, 0)),
                  pl.BlockSpec((1, 6, d), lambda bb, i: (bb, 0, 0)),
                  pl.BlockSpec((1, 6, d), lambda bb, i: (b, 0, 0)),
                  pl.BlockSpec((d, hid), lambda bb, i: (0, 0), pipeline_mode=once),
                  pl.BlockSpec((hid, d), lambda bb, i: (0, 0), pipeline_mode=once),
                  pl.BlockSpec((1, d), lambda bb, i: (0, 0))],
        out_specs=pl.BlockSpec((1, tm, d), lambda bb, i: (bb, i, 0)),
        out_shape=jax.ShapeDtypeStruct((b, n_rows, d), F32),
        scratch_shapes=[pltpu.VMEM((tm, d), F32)],
        compiler_params=_params("arbitrary", "arbitrary"),
        name="mix_mlp",
    )(x, a, bm, w_out, w_out, g.reshape(1, d), mods, mods, w1, w2, final_g.reshape(1, d))


def _conv3(x, w, first, last):
    rows = x.shape[0]
    xm = jnp.where(first, 0.0, pltpu.roll(x, 1, 0))
    xp = jnp.where(last, 0.0, pltpu.roll(x, rows - 1, 0))
    return xm * w[0:1] + x * w[1:2] + xp * w[2:3]


def _shortconv_kernel(x_ref, w_ref, o_ref, *, n_lat):
    x = x_ref[0]
    t = x.shape[0]
    row = lax.broadcasted_iota(jnp.int32, (t, 1), 0)
    first = (row == 0) | (row == n_lat)
    last = (row == n_lat - 1) | (row == t - 1)
    o_ref[0] = _conv3(x, w_ref[...], first, last)


def short_conv(p, w, n_lat):
    b, t, _ = p.shape
    c = w.shape[1]
    cb = 512
    return pl.pallas_call(
        functools.partial(_shortconv_kernel, n_lat=n_lat),
        grid=(b, c // cb),
        in_specs=[pl.BlockSpec((1, t, cb), lambda bb, j: (bb, 0, j)),
                  pl.BlockSpec((SHORT_K, cb), lambda bb, j: (0, j))],
        out_specs=pl.BlockSpec((1, t, cb), lambda bb, j: (bb, 0, j)),
        out_shape=jax.ShapeDtypeStruct((b, t, c), F32),
        compiler_params=_params("arbitrary", "arbitrary"),
        name="short_conv",
    )(p, w)


S5_KB = LANE
S5_NB = GROUP_W // S5_KB
S5_SB = S5_W // S5_NB


def _s5_kernel(*refs, tc, nb, reverse, finish):
    if finish:
        (u_ref, wbr_ref, wbi_ref, lr_ref, li_ref, wcr_ref, wci_ref, yf_ref, d_ref, gw_ref, gb_ref,
         o_ref, hr_s, hi_s, sr_s, si_s) = refs
    else:
        (u_ref, wbr_ref, wbi_ref, lr_ref, li_ref, wcr_ref, wci_ref,
         o_ref, hr_s, hi_s, sr_s, si_s) = refs

    @pl.when(pl.program_id(0) == 0)
    def _():
        sr_s[...] = jnp.zeros_like(sr_s)
        si_s[...] = jnp.zeros_like(si_s)

    u = u_ref[...]
    ub = u.astype(BF16)
    for j in range(S5_NB):
        uj = ub[:, j * S5_KB:(j + 1) * S5_KB]
        hr_s[:, j * S5_SB:(j + 1) * S5_SB] = _dot(uj, wbr_ref[j])
        hi_s[:, j * S5_SB:(j + 1) * S5_SB] = _dot(uj, wbi_ref[j])

    for s in range(S5_NB):
        sl = slice(s * S5_SB, (s + 1) * S5_SB)
        lr = jnp.broadcast_to(lr_ref[:, sl], (nb, S5_SB))
        li = jnp.broadcast_to(li_ref[:, sl], (nb, S5_SB))

        def body(k, carry, sl=sl, lr=lr, li=li):
            hr, hi = carry
            t = (tc - 1 - k) if reverse else k
            r0 = pl.multiple_of(t * nb, nb)
            nr = lr * hr - li * hi + hr_s[pl.ds(r0, nb), sl]
            ni = lr * hi + li * hr + hi_s[pl.ds(r0, nb), sl]
            hr_s[pl.ds(r0, nb), sl] = nr
            hi_s[pl.ds(r0, nb), sl] = ni
            return nr, ni

        hr, hi = lax.fori_loop(0, tc, body, (sr_s[:, sl], si_s[:, sl]), unroll=4)
        sr_s[:, sl] = hr
        si_s[:, sl] = hi

    for j in range(S5_NB):
        sl = slice(j * S5_SB, (j + 1) * S5_SB)
        co = slice(j * S5_KB, (j + 1) * S5_KB)
        y = _dot(hr_s[:, sl].astype(BF16), wcr_ref[j]) + _dot(hi_s[:, sl].astype(BF16), wci_ref[j])
        if finish:
            o_ref[:, co] = y + yf_ref[:, co] + u[:, co] * d_ref[:, co]
        else:
            o_ref[:, co] = y

    if finish:
        y = o_ref[...]
        y = 0.5 * y * (1.0 + lax.erf(y * (2.0 ** -0.5)))
        z = _dot(y.astype(BF16), gw_ref[...]) + gb_ref[...]
        o_ref[...] = y * jax.nn.sigmoid(z)


def s5_scan_call(u2, tabs, n_lat_steps, n_steps, nb, reverse, extra):
    tc = 32
    rows = tc * nb
    n_chunks = n_steps // tc
    n_lat_chunks = n_lat_steps // tc
    n_ctx_chunks = n_chunks - n_lat_chunks
    if reverse:
        def cmap(c):
            return (n_chunks - 1 - c, 0)
    else:
        def cmap(c):
            return (jnp.where(c < n_ctx_chunks, n_lat_chunks + c, c - n_ctx_chunks), 0)
    wbr, wbi, lr, li, wcr, wci = tabs
    full3 = lambda a: pl.BlockSpec(a.shape, lambda c: (0, 0, 0))
    full2 = lambda a: pl.BlockSpec(a.shape, lambda c: (0, 0))
    in_specs = [pl.BlockSpec((rows, GROUP_W), cmap), full3(wbr), full3(wbi), full2(lr), full2(li),
                full3(wcr), full3(wci)]
    args = [u2, wbr, wbi, lr, li, wcr, wci]
    finish = extra is not None
    if finish:
        yf, dsk, gw, gb = extra
        in_specs += [pl.BlockSpec((rows, GROUP_W), cmap), full2(dsk), full2(gw), full2(gb)]
        args += [yf, dsk, gw, gb]
    return pl.pallas_call(
        functools.partial(_s5_kernel, tc=tc, nb=nb, reverse=reverse, finish=finish),
        grid=(n_chunks,),
        in_specs=in_specs,
        out_specs=pl.BlockSpec((rows, GROUP_W), cmap),
        out_shape=jax.ShapeDtypeStruct(u2.shape, F32),
        scratch_shapes=[pltpu.VMEM((rows, S5_W), F32), pltpu.VMEM((rows, S5_W), F32),
                        pltpu.VMEM((nb, S5_W), F32), pltpu.VMEM((nb, S5_W), F32)],
        compiler_params=_params("arbitrary"),
        name="s5_bwd_finish" if finish else "s5_fwd",
    )(*args)


def _block_diag(x):
    nblk, g, r, c = x.shape
    eye = jnp.eye(g, dtype=x.dtype)
    return jnp.einsum('jgrc,gh->jgrhc', x, eye).reshape(nblk, g * r, g * c)


def s5_tables(lam_re, lam_im, log_dt, b_re, b_im, c_re, c_im):
    lam_re = jnp.minimum(lam_re.astype(F32), -1e-4)
    lam_im = lam_im.astype(F32)
    dt = jnp.exp(log_dt.astype(F32))[:, None]
    mag = jnp.exp(lam_re * dt)
    lb_re = mag * jnp.cos(lam_im * dt)
    lb_im = mag * jnp.sin(lam_im * dt)
    den = lam_re * lam_re + lam_im * lam_im
    f_re = ((lb_re - 1.0) * lam_re + lb_im * lam_im) / den
    f_im = (lb_im * lam_re - (lb_re - 1.0) * lam_im) / den
    b_re = b_re.astype(F32)
    b_im = b_im.astype(F32)
    bb_re = f_re[..., None] * b_re - f_im[..., None] * b_im
    bb_im = f_re[..., None] * b_im + f_im[..., None] * b_re
    gpb = S5_KB // S5_GROUP
    to_b = lambda x: _block_diag(
        jnp.swapaxes(x, 1, 2).reshape(S5_NB, gpb, S5_GROUP, S5_STATE)).astype(BF16)
    to_c = lambda x: _block_diag(
        jnp.swapaxes(x.astype(F32), 1, 2).reshape(S5_NB, gpb, S5_STATE, S5_GROUP)).astype(BF16)
    return (to_b(bb_re), to_b(bb_im), lb_re.reshape(1, S5_W), lb_im.reshape(1, S5_W),
            to_c(c_re), to_c(-c_im.astype(F32)))


def dft_tables(n):
    nn = 2 * n
    f = jnp.arange(n, dtype=jnp.int32)[:, None]
    t = jnp.arange(n, dtype=jnp.int32)[None, :]
    q = 1 << (int(math.log2(n)) // 2)

    def factor(tt):
        ang = ((f * tt[None, :]) % nn).astype(F32) * (2.0 * math.pi / nn)
        return jnp.cos(ang), jnp.sin(ang)

    c1, s1 = factor(jnp.arange(n // q, dtype=jnp.int32) * q)
    c0, s0 = factor(jnp.arange(q, dtype=jnp.int32))
    cos = (c1[:, :, None] * c0[:, None, :] - s1[:, :, None] * s0[:, None, :]).reshape(n, n)
    sin = (s1[:, :, None] * c0[:, None, :] + c1[:, :, None] * s0[:, None, :]).reshape(n, n)
    nyq = jnp.where(t % 2 == 0, 1.0, -1.0).astype(F32)
    msin = jnp.where(f == 0, nyq, -sin)
    fwd = jnp.concatenate([cos, msin], axis=0)
    wf = jnp.where(jnp.arange(n) == 0, 1.0 / nn, 2.0 / nn).astype(F32)
    return fwd, wf


def _filtdft_kernel(fh_ref, fl_ref, hh_ref, hl_ref, o_ref):
    fh = fh_ref[...]
    o_ref[...] = _dot(fh, hh_ref[...]) + _dot(fh, hl_ref[...]) + _dot(fl_ref[...], hh_ref[...])


def filter_dft(fwd_hi, fwd_lo, h):
    n2, n = fwd_hi.shape
    c = h.shape[1]
    tr = min(256, n2)
    tcn = 512
    hh, hl = _split_bf16(h)
    return pl.pallas_call(
        _filtdft_kernel,
        grid=(n2 // tr, c // tcn),
        in_specs=[pl.BlockSpec((tr, n), lambda i, j: (i, 0)),
                  pl.BlockSpec((tr, n), lambda i, j: (i, 0)),
                  pl.BlockSpec((n, tcn), lambda i, j: (0, j)),
                  pl.BlockSpec((n, tcn), lambda i, j: (0, j))],
        out_specs=pl.BlockSpec((tr, tcn), lambda i, j: (i, j)),
        out_shape=jax.ShapeDtypeStruct((n2, c), F32),
        compiler_params=_params("arbitrary", "arbitrary"),
        name="filter_dft",
    )(fwd_hi, fwd_lo, hh, hl)


def hyena_filter_response(n, fwd_hi, fwd_lo, wf, f_w1, f_b1, f_w2, f_b2, f_w3, f_freq, log_decay):
    t = jnp.linspace(0.0, 1.0, n, dtype=F32)[:, None]
    w = 2.0 * math.pi * jnp.arange(n, dtype=F32)[:, None] / n
    bands = jnp.linspace(1e-4, HY_BANDS - 1, HY_BANDS, dtype=F32)[None, :]
    z = jnp.concatenate([t, jnp.cos(bands * w), -jnp.sin(bands * w)], axis=-1)
    freq = f_freq.astype(F32)
    hid = jnp.sin(freq[0] * (jnp.dot(z, f_w1.astype(F32), precision=HIGHEST) + f_b1.astype(F32)))
    hid = jnp.sin(freq[1] * (jnp.dot(hid, f_w2.astype(F32), precision=HIGHEST) + f_b2.astype(F32)))
    h = jnp.dot(hid, f_w3.astype(F32), precision=HIGHEST).reshape(n, 2, HY_ORDER, GROUP_W)
    h = h * jnp.exp(-t[:, :, None, None] * jnp.exp(log_decay.astype(F32)))
    cw = HY_ORDER * GROUP_W
    h_fwd = h[:, 0].reshape(n, cw)
    h_bwd = h[:, 1].reshape(n, cw).at[0].set(0.0)
    resp = filter_dft(fwd_hi, fwd_lo, jnp.concatenate([h_fwd, h_bwd], axis=1))
    a_re, a_im = resp[:n, :cw], resp[n:, :cw]
    b_re, b_im = resp[:n, cw:], resp[n:, cw:]
    k_re = a_re + b_re
    first = (jnp.arange(n) == 0)[:, None]
    k_im = jnp.where(first, a_im + b_im, a_im - b_im)
    k_re = (k_re * wf[:, None]).reshape(n, HY_ORDER, GROUP_W).transpose(1, 0, 2)
    k_im = (k_im * wf[:, None]).reshape(n, HY_ORDER, GROUP_W).transpose(1, 0, 2)
    return k_re, k_im


def _hyena_kernel(z0_ref, x1_ref, x2_ref, cw_ref, cb_ref, fc_ref, fs_ref, gc_ref, gs_ref, kr_ref, ki_ref,
                  bias_ref, *rest):
    o_ref, zb_s, zf_s, acc_s = rest[-4:]
    o = pl.program_id(1)
    f = pl.program_id(2)
    nf = pl.num_programs(2)
    c = GROUP_W

    def stream(ref, s):
        x = ref[0]
        row = lax.broadcasted_iota(jnp.int32, (x.shape[0], 1), 0)
        cols = slice(s * c, (s + 1) * c)
        return _conv3(x, cw_ref[:, cols], row == 0, row == x.shape[0] - 1) + cb_ref[:, cols]

    @pl.when((o == 0) & (f == 0))
    def _():
        z0 = stream(z0_ref, 0)
        zf_s[...] = z0
        zb_s[...] = z0.astype(BF16)

    @pl.when(f == 0)
    def _():
        acc_s[...] = jnp.zeros_like(acc_s)

    zb = zb_s[...]
    xr = _dot(fc_ref[...], zb)
    xi = _dot(fs_ref[...], zb)
    kr = kr_ref[0]
    ki = ki_ref[0]
    row = lax.broadcasted_iota(jnp.int32, (xr.shape[0], 1), 0)
    packed = (row == 0) & (f == 0)
    yr = jnp.where(packed, xr * kr, xr * kr - xi * ki)
    yi = jnp.where(packed, xi * ki, xr * ki + xi * kr)
    acc_s[...] += _dot(gc_ref[...], yr.astype(BF16)) + _dot(gs_ref[...], yi.astype(BF16))

    @pl.when(f == nf - 1)
    def _():
        bias = bias_ref[...]

        @pl.when(o == 0)
        def _():
            z1 = stream(x1_ref, 1) * (acc_s[...] + zf_s[...] * bias[0:1])
            zf_s[...] = z1
            zb_s[...] = z1.astype(BF16)

        @pl.when(o == 1)
        def _():
            o_ref[0] = stream(x2_ref, 2) * (acc_s[...] + zf_s[...] * bias[1:2])


def hyena_call(p, n, row_blk, conv_w, conv_b, fwd_hi, fwd_t, k_re, k_im, bias, out_prev):
    b, t_total, _ = p.shape
    c = GROUP_W
    fb = min(256, n)
    nf = n // fb
    once = pl.Buffered(1)
    in_specs = [pl.BlockSpec((1, n, c), lambda bb, o, f: (bb, row_blk, 0), pipeline_mode=once),
                pl.BlockSpec((1, n, c), lambda bb, o, f: (bb, row_blk, 1), pipeline_mode=once),
                pl.BlockSpec((1, n, c), lambda bb, o, f: (bb, row_blk, 2), pipeline_mode=once),
                pl.BlockSpec((SHORT_K, 3 * c), lambda bb, o, f: (0, 0)),
                pl.BlockSpec((1, 3 * c), lambda bb, o, f: (0, 0)),
                pl.BlockSpec((fb, n), lambda bb, o, f: (f, 0)),
                pl.BlockSpec((fb, n), lambda bb, o, f: (nf + f, 0)),
                pl.BlockSpec((n, fb), lambda bb, o, f: (0, f)),
                pl.BlockSpec((n, fb), lambda bb, o, f: (0, nf + f)),
                pl.BlockSpec((1, fb, c), lambda bb, o, f: (o, f, 0)),
                pl.BlockSpec((1, fb, c), lambda bb, o, f: (o, f, 0)),
                pl.BlockSpec((HY_ORDER, c), lambda bb, o, f: (0, 0))]
    args = [p, p, p, conv_w, conv_b.reshape(1, 3 * c), fwd_hi, fwd_hi, fwd_t, fwd_t, k_re, k_im, bias]
    aliases = {}
    if out_prev is not None:
        in_specs.append(pl.BlockSpec(memory_space=pl.ANY))
        args.append(out_prev)
        aliases = {len(args) - 1: 0}
    return pl.pallas_call(
        _hyena_kernel,
        grid=(b, HY_ORDER, nf),
        in_specs=in_specs,
        out_specs=pl.BlockSpec((1, n, c), lambda bb, o, f: (bb, row_blk, 0)),
        out_shape=jax.ShapeDtypeStruct((b, t_total, c), F32),
        scratch_shapes=[pltpu.VMEM((n, c), BF16), pltpu.VMEM((n, c), F32), pltpu.VMEM((n, c), F32)],
        input_output_aliases=aliases,
        compiler_params=_params("arbitrary", "arbitrary", "arbitrary"),
        name="hyena",
    )(*args)


def _rwkv_lora_kernel(p_ref, ww_ref, wa_ref, wg_ref, w0_ref, a0_ref, wp0_ref, wp1_ref, ap0_ref, ap1_ref,
                      g_ref):
    x = p_ref[0]
    wa_in = x[:, :LANE]
    th = jnp.tanh(wa_in).astype(BF16)
    lin = wa_in.astype(BF16)
    w0 = w0_ref[...]
    a0 = a0_ref[...]
    wp0_ref[0] = w0[0:1] + _dot(th, ww_ref[0])
    wp1_ref[0] = w0[1:2] + _dot(th, ww_ref[1])
    ap0_ref[0] = a0[0:1] + _dot(lin, wa_ref[0])
    ap1_ref[0] = a0[1:2] + _dot(lin, wa_ref[1])
    g_ref[0] = _dot(jax.nn.sigmoid(x[:, LANE:]).astype(BF16), wg_ref[...])


def rwkv_lora(p, w_up, a_up, g_up, w0, a0):
    b, t, _ = p.shape
    c = GROUP_W
    tm = TOKEN_TILE
    zeros = jnp.zeros((2, 64, c), F32)
    ww = jnp.concatenate([w_up.astype(F32), zeros], axis=1).astype(BF16)
    wa = jnp.concatenate([zeros, a_up.astype(F32)], axis=1).astype(BF16)
    out = jax.ShapeDtypeStruct((b, t, c), F32)
    ospec = pl.BlockSpec((1, tm, c), lambda bb, i: (bb, i, 0))
    return pl.pallas_call(
        _rwkv_lora_kernel,
        grid=(b, t // tm),
        in_specs=[pl.BlockSpec((1, tm, RW_LORA_W), lambda bb, i: (bb, i, 3 * c // RW_LORA_W)),
                  pl.BlockSpec((2, LANE, c), lambda bb, i: (0, 0, 0)),
                  pl.BlockSpec((2, LANE, c), lambda bb, i: (0, 0, 0)),
                  pl.BlockSpec((LANE, c), lambda bb, i: (0, 0)),
                  pl.BlockSpec((2, c), lambda bb, i: (0, 0)),
                  pl.BlockSpec((2, c), lambda bb, i: (0, 0))],
        out_specs=[ospec] * 5,
        out_shape=[out] * 5,
        compiler_params=_params("arbitrary", "arbitrary"),
        name="rwkv_lora",
    )(p, ww, wa, g_up.astype(BF16), w0, a0)


def _rwkv_scan_kernel(wpf_ref, wpb_ref, apf_ref, apb_ref, kf_ref, kb_ref, vf_ref, vb_ref, rf_ref, rb_ref,
                      kkc_ref, kac_ref, rkc_ref, yf_ref, yb_ref, bonf_ref, bonb_ref,
                      s_ref, w_s, kk_s, b_s, ke_s, r_s, *, tb):
    n = RW_HEAD

    @pl.when(pl.program_id(0) == 0)
    def _():
        s_ref[...] = jnp.zeros_like(s_ref)

    def both(f_ref, b_ref, t):
        return jnp.concatenate([f_ref[t], b_ref[tb - 1 - t]], axis=-1)

    def prepare(t, slot):
        k = both(kf_ref, kb_ref, t)
        a = jax.nn.sigmoid(both(apf_ref, apb_ref, t))
        kk = k * kkc_ref[...]
        nrm = jnp.sqrt(jnp.sum(kk * kk, axis=0, keepdims=True))
        kk = kk / jnp.maximum(nrm, 1e-12)
        ke = k * (1.0 + (a - 1.0) * kac_ref[...])
        wl = -jax.nn.softplus(-both(wpf_ref, wpb_ref, t)) - 0.5
        r = both(rf_ref, rb_ref, t)
        w_s[slot] = jnp.exp(-jnp.exp(wl))
        kk_s[slot] = kk
        b_s[slot] = kk * a
        ke_s[slot] = ke
        r_s[slot] = r
        bon = jnp.sum(r * ke * rkc_ref[...], axis=0, keepdims=True)
        bonf_ref[pl.ds(t, 1), :] = bon
        bonb_ref[pl.ds(tb - 1 - t, 1), :] = bon

    prepare(0, 0)
    sa0 = jnp.zeros((n, LANE), F32)
    for j in range(n):
        sa0 = sa0 + s_ref[j] * kk_s[0, j:j + 1, :]

    def step(t, sa):
        slot = t % 2
        nxt = 1 - slot
        prepare(jnp.minimum(t + 1, tb - 1), nxt)
        v = both(vf_ref, vb_ref, t)
        y = jnp.zeros((n, LANE), F32)
        sa_next = jnp.zeros((n, LANE), F32)
        for j in range(n):
            row = pl.ds(j, 1)
            s = s_ref[j] * w_s[slot, row, :] - sa * b_s[slot, row, :] + v * ke_s[slot, row, :]
            s_ref[j] = s
            y = y + s * r_s[slot, row, :]
            sa_next = sa_next + s * kk_s[nxt, row, :]
        yf_ref[t] = y
        yb_ref[tb - 1 - t] = y
        return sa_next

    lax.fori_loop(0, tb, step, sa0)


def rwkv_scan(wp0, wp1, ap0, ap1, k, v, r, kkc, kac, rkc, n_lat):
    t = k.shape[0]
    tb = 32
    n = RW_HEAD
    half = LANE // 2
    n_blk = t // tb
    n_lat_blk = n_lat // tb
    n_ctx_blk = n_blk - n_lat_blk

    def fblk(c):
        return jnp.where(c < n_ctx_blk, n_lat_blk + c, c - n_ctx_blk)

    def rblk(c):
        return n_blk - 1 - c

    fwd_in = pl.BlockSpec((tb, n, half), lambda c: (fblk(c), 0, 0))
    bwd_in = pl.BlockSpec((tb, n, half), lambda c: (rblk(c), 0, 0))
    cst = pl.BlockSpec((n, LANE), lambda c: (0, 0))
    pair = pltpu.VMEM((2, n, LANE), F32)
    return pl.pallas_call(
        functools.partial(_rwkv_scan_kernel, tb=tb),
        grid=(n_blk,),
        in_specs=[fwd_in, bwd_in] * 5 + [cst] * 3,
        out_specs=[pl.BlockSpec((tb, n, LANE), lambda c: (fblk(c), 0, 0)),
                   pl.BlockSpec((tb, n, LANE), lambda c: (rblk(c), 0, 0)),
                   pl.BlockSpec((tb, LANE), lambda c: (fblk(c), 0)),
                   pl.BlockSpec((tb, LANE), lambda c: (rblk(c), 0))],
        out_shape=[jax.ShapeDtypeStruct((t, n, LANE), F32), jax.ShapeDtypeStruct((t, n, LANE), F32),
                   jax.ShapeDtypeStruct((t, LANE), F32), jax.ShapeDtypeStruct((t, LANE), F32)],
        scratch_shapes=[pltpu.VMEM((n, n, LANE), F32), pair, pair, pair, pair, pair],
        compiler_params=_params("arbitrary"),
        name="rwkv_scan",
    )(wp0, wp1, ap0, ap1, k, k, v, v, r, r, kkc, kac, rkc)


def _rwkv_post_kernel(y_ref, bon_ref, v_ref, g_ref, m_ref, lg_ref, lb_ref, o_ref):
    m = m_ref[...]

    def head_mean(x):
        hi, lo = _split_bf16(x)
        return _dot(hi, m) + _dot(lo, m)

    y = y_ref[0]
    d = y - head_mean(y)
    var = head_mean(d * d)
    yn = d * lax.rsqrt(var + RW_LN_EPS) * lg_ref[...] + lb_ref[...]
    o_ref[0] = (yn + bon_ref[0] * v_ref[0]) * g_ref[0]


def rwkv_post(y, bon, rkv, gate, ln_g, ln_b, n_rows):
    b, t, c = y.shape
    tm = TOKEN_TILE
    hm = jnp.kron(jnp.eye(RW_HEADS, dtype=F32), jnp.full((RW_HEAD, RW_HEAD), 1.0 / RW_HEAD, F32)).astype(BF16)
    tok = pl.BlockSpec((1, tm, c), lambda bb, i: (bb, i, 0))
    vec = pl.BlockSpec((1, c), lambda bb, i: (0, 0))
    return pl.pallas_call(
        _rwkv_post_kernel,
        grid=(b, n_rows // tm),
        in_specs=[tok, tok, pl.BlockSpec((1, tm, c), lambda bb, i: (bb, i, 2)), tok,
                  pl.BlockSpec((c, c), lambda bb, i: (0, 0)), vec, vec],
        out_specs=tok,
        out_shape=jax.ShapeDtypeStruct((b, t, c), F32),
        compiler_params=_params("arbitrary", "arbitrary"),
        name="rwkv_post",
    )(y, bon, rkv, gate, hm, ln_g.reshape(1, c), ln_b.reshape(1, c))


def _to_scan_layout(a):
    b, t, _ = a.shape
    return a.reshape(b, t, RW_HEADS, RW_HEAD).transpose(1, 3, 0, 2).reshape(t, RW_HEAD, b * RW_HEADS)


def _from_scan_layout(yf, yb, b):
    t = yf.shape[0]
    half = b * RW_HEADS
    y = yf[..., :half] + yb[..., half:]
    if y.ndim == 3:
        return y.reshape(t, RW_HEAD, b, RW_HEADS).transpose(2, 0, 3, 1).reshape(b, t, GROUP_W)
    return jnp.repeat(y.reshape(t, b, RW_HEADS).transpose(1, 0, 2), RW_HEAD, axis=-1)


def _chain_const(x, b):
    return jnp.tile(x.astype(F32).reshape(RW_HEADS, RW_HEAD).T, (1, 2 * b))


def _rope(x, cos, sin):
    lane = lax.broadcasted_iota(jnp.int32, (1, LANE), 1)
    first = (lane % (2 * ROPE_FREQS)) < ROPE_FREQS
    partner = jnp.where(first, pltpu.roll(x, LANE - ROPE_FREQS, 1), pltpu.roll(x, ROPE_FREQS, 1))
    return x * cos + partner * sin


def _attn_tile(q, k, v, lam):
    lane = lax.broadcasted_iota(jnp.int32, (1, LANE), 1)
    m0 = lane < DA_HEAD
    q0 = jnp.where(m0, q, 0.0).astype(BF16)
    q1 = jnp.where(m0, 0.0, q).astype(BF16)
    dn = (((1,), (1,)), ((), ()))
    s0 = lax.dot_general(q0, k, dn, preferred_element_type=F32)
    s1 = lax.dot_general(q1, k, dn, preferred_element_type=F32)
    p0 = jnp.exp(s0 - jnp.max(s0, axis=-1, keepdims=True))
    p1 = jnp.exp(s1 - jnp.max(s1, axis=-1, keepdims=True))
    w = p0 / jnp.sum(p0, axis=-1, keepdims=True) - lam * (p1 / jnp.sum(p1, axis=-1, keepdims=True))
    return _dot(w.astype(BF16), v)


def _attn_kernel(q_ref, k_ref, v_ref, cq_ref, sq_ref, ck_ref, sk_ref, lp_ref, g_ref, o_ref,
                 kr_s, vb_s, *, n_lat, n_lat_tiles, lam_init):
    i = pl.program_id(2)

    @pl.when(i == 0)
    def _():
        kr_s[...] = _rope(k_ref[0], ck_ref[...], sk_ref[...]).astype(BF16)
        vb_s[...] = v_ref[0].astype(BF16)

    lp = lp_ref[...]
    lam = (jnp.exp(jnp.sum(lp[0:1] * lp[1:2], axis=-1, keepdims=True))
           - jnp.exp(jnp.sum(lp[2:3] * lp[3:4], axis=-1, keepdims=True)) + lam_init)
    q = _rope(q_ref[0], cq_ref[...], sq_ref[...]) * DA_SCALE

    def finish(o):
        on = o * lax.rsqrt(jnp.mean(o * o, axis=-1, keepdims=True) + DA_SUBLN_EPS)
        o_ref[0] = on * g_ref[...] * (1.0 - lam_init)

    @pl.when(i < n_lat_tiles)
    def _():
        finish(_attn_tile(q, kr_s[...], vb_s[...], lam))

    @pl.when(i >= n_lat_tiles)
    def _():
        finish(_attn_tile(q, kr_s[n_lat:], vb_s[n_lat:], lam))


def diff_attention(p, col0, cosf, sins, lam_p, subln_g, lam_init, n_lat, n_rows):
    b, t, _ = p.shape
    tq = TOKEN_TILE
    off = col0 // LANE
    hq = DA_HEADS
    return pl.pallas_call(
        functools.partial(_attn_kernel, n_lat=n_lat, n_lat_tiles=n_lat // tq, lam_init=lam_init),
        grid=(b, hq, n_rows // tq),
        in_specs=[pl.BlockSpec((1, tq, LANE), lambda bb, h, i: (bb, i, off + h)),
                  pl.BlockSpec((1, t, LANE), lambda bb, h, i: (bb, 0, off + hq + h)),
                  pl.BlockSpec((1, t, LANE), lambda bb, h, i: (bb, 0, off + 2 * hq + h)),
                  pl.BlockSpec((tq, LANE), lambda bb, h, i: (i, 0)),
                  pl.BlockSpec((tq, LANE), lambda bb, h, i: (i, 0)),
                  pl.BlockSpec((t, LANE), lambda bb, h, i: (0, 0)),
                  pl.BlockSpec((t, LANE), lambda bb, h, i: (0, 0)),
                  pl.BlockSpec((4, DA_HEAD), lambda bb, h, i: (0, 0)),
                  pl.BlockSpec((1, LANE), lambda bb, h, i: (0, 0))],
        out_specs=pl.BlockSpec((1, tq, LANE), lambda bb, h, i: (bb, i, h)),
        out_shape=jax.ShapeDtypeStruct((b, t, hq * DA_V), F32),
        scratch_shapes=[pltpu.VMEM((t, LANE), BF16), pltpu.VMEM((t, LANE), BF16)],
        compiler_params=_params("arbitrary", "arbitrary", "arbitrary"),
        name="diff_attention",
    )(p, p, p, cosf, sins, cosf, sins, lam_p, subln_g.reshape(1, DA_V))


def rope_tables(n_lat, n_ctx):
    rows = n_lat // GRID_W
    row = jnp.repeat(jnp.arange(rows, dtype=F32), GRID_W)
    col = jnp.tile(jnp.arange(GRID_W, dtype=F32), rows)
    inv = ROPE_BASE ** (-jnp.arange(ROPE_FREQS, dtype=F32) / ROPE_FREQS)
    ang = jnp.stack([row[:, None] * inv, col[:, None] * inv], axis=1)
    cos, sin = jnp.cos(ang), jnp.sin(ang)
    cosf = jnp.concatenate([cos, cos], axis=-1).reshape(n_lat, DA_HEAD)
    sins = jnp.concatenate([-sin, sin], axis=-1).reshape(n_lat, DA_HEAD)
    cosf = jnp.concatenate([jnp.tile(cosf, (1, 2)), jnp.ones((n_ctx, LANE), F32)], axis=0)
    sins = jnp.concatenate([jnp.tile(sins, (1, 2)), jnp.zeros((n_ctx, LANE), F32)], axis=0)
    return cosf, sins


def _even_mixers(xs, mods, keep_ctx, n_lat, norm1_g, w_in, s5p, hyp, dft):
    b, t, _ = xs.shape
    n_ctx = t - n_lat
    u_t, p_h = in_proj(xs, norm1_g, mods, w_in.astype(BF16), n_lat, GROUP_W)

    (lam_re, lam_im, log_dt, b_re, b_im, c_re, c_im, d_skip, glu_w, glu_b) = s5p
    u2 = u_t.reshape(t * b, GROUP_W)
    tabs = [s5_tables(lam_re[d], lam_im[d], log_dt[d], b_re[d], b_im[d], c_re[d], c_im[d]) for d in range(2)]
    y_f = s5_scan_call(u2, tabs[0], n_lat, t, b, False, None)
    a_t = s5_scan_call(u2, tabs[1], n_lat, t, b, True,
                       (y_f, d_skip.reshape(1, GROUP_W), glu_w.astype(BF16), glu_b.reshape(1, GROUP_W)))

    (conv_w, conv_b, f_w1, f_b1, f_w2, f_b2, f_w3, f_freq, log_decay, bias) = hyp
    fh_l, fl_l, ft_l, wf_l = dft[0]
    kr, ki = hyena_filter_response(n_lat, fh_l, fl_l, wf_l, f_w1, f_b1, f_w2, f_b2, f_w3, f_freq, log_decay)
    b_m = hyena_call(p_h, n_lat, 0, conv_w, conv_b, fh_l, ft_l, kr, ki, bias, None)
    if keep_ctx:
        fh_c, fl_c, ft_c, wf_c = dft[1]
        kr, ki = hyena_filter_response(n_ctx, fh_c, fl_c, wf_c, f_w1, f_b1, f_w2, f_b2, f_w3, f_freq,
                                       log_decay)
        b_m = hyena_call(p_h, n_ctx, n_lat // n_ctx, conv_w, conv_b, fh_c, ft_c, kr, ki, bias, b_m)
    return a_t.reshape(t, b * GROUP_W), b_m


def _odd_mixers(xs, mods, keep_ctx, n_lat, lam_init, norm1_g, w_in, rwp, dap, rope):
    b, t, _ = xs.shape
    n_rows = t if keep_ctx else n_lat
    (p,) = in_proj(xs, norm1_g, mods, w_in.astype(BF16), n_lat, 0)

    (conv_w, w0, w_up, a0, a_up, g_up, k_k, k_a, r_k, ln_g, ln_b) = rwp
    rkv = short_conv(p, conv_w, n_lat)
    r, k, v = (_to_scan_layout(rkv[..., i * GROUP_W:(i + 1) * GROUP_W]) for i in range(3))
    wp0, wp1, ap0, ap1, gate = rwkv_lora(p, w_up, a_up, g_up, w0, a0)
    yf, yb, bonf, bonb = rwkv_scan(_to_scan_layout(wp0), _to_scan_layout(wp1), _to_scan_layout(ap0),
                                   _to_scan_layout(ap1), k, v, r, _chain_const(k_k, b),
                                   _chain_const(k_a, b), _chain_const(r_k, b), n_lat)
    a_m = rwkv_post(_from_scan_layout(yf, yb, b), _from_scan_layout(bonf, bonb, b), rkv, gate,
                    ln_g, ln_b, n_rows)

    lam_p, subln_g = dap
    b_m = diff_attention(p, RW_IN, rope[0], rope[1], lam_p, subln_g, lam_init, n_lat, n_rows)
    return a_m, b_m


def kernel(x, c, ctx, c_ctx, ada_w, ada_b, norm1_g, norm2_g, mlp_w1, mlp_w2, final_g, ev_w_in, ev_w_out, s5_lam_re, s5_lam_im, s5_log_dt, s5_b_re, s5_b_im, s5_c_re, s5_c_im, s5_d, s5_glu_w, s5_glu_b, hy_conv_w, hy_conv_b, hy_f_w1, hy_f_b1, hy_f_w2, hy_f_b2, hy_f_w3, hy_f_freq, hy_log_decay, hy_bias, od_w_in, od_w_out, rw_conv_w, rw_w0, rw_w_up, rw_a0, rw_a_up, rw_g_up, rw_k_k, rw_k_a, rw_r_k, rw_ln_g, rw_ln_b, da_lam, da_subln_g):
    b, n_lat, d = x.shape
    n_ctx = ctx.shape[1]
    assert d == D_MODEL and b == SUBLANE
    assert n_lat % TOKEN_TILE == 0 and n_ctx % TOKEN_TILE == 0 and n_lat % n_ctx == 0
    xs = jnp.concatenate([x.astype(F32), ctx.astype(F32)], axis=1)

    c_rows = jnp.zeros((16, d), F32).at[:b].set(c.astype(F32)).at[b].set(c_ctx.astype(F32))
    depth = ada_w.shape[0]
    mods_all = ada_mods(c_rows, ada_w, ada_b).reshape(depth, 16, 6, d)

    rope = rope_tables(n_lat, n_ctx)
    dft = []
    for n in (n_lat, n_ctx):
        fwd, wf = dft_tables(n)
        hi, lo = _split_bf16(fwd)
        dft.append((hi, lo, hi.T, wf))

    for l in range(depth):
        keep_ctx = l < depth - 1
        i = l // 2
        mods = mods_all[l]
        if l % 2 == 0:
            s5p = (s5_lam_re[i], s5_lam_im[i], s5_log_dt[i], s5_b_re[i], s5_b_im[i], s5_c_re[i], s5_c_im[i],
                   s5_d[i], s5_glu_w[i], s5_glu_b[i])
            hyp = (hy_conv_w[i], hy_conv_b[i], hy_f_w1[i], hy_f_b1[i], hy_f_w2[i], hy_f_b2[i], hy_f_w3[i],
                   hy_f_freq[i], hy_log_decay[i], hy_bias[i])
            a_m, b_m = _even_mixers(xs, mods, keep_ctx, n_lat, norm1_g[l], ev_w_in[i], s5p, hyp, dft)
            w_out = ev_w_out[i]
        else:
            rwp = (rw_conv_w[i], rw_w0[i], rw_w_up[i], rw_a0[i], rw_a_up[i], rw_g_up[i], rw_k_k[i],
                   rw_k_a[i], rw_r_k[i], rw_ln_g[i], rw_ln_b[i])
            lam_init = 0.8 - 0.6 * math.exp(-0.3 * l)
            a_m, b_m = _odd_mixers(xs, mods, keep_ctx, n_lat, lam_init, norm1_g[l], od_w_in[i], rwp,
                                   (da_lam[i], da_subln_g[i]), rope)
            w_out = od_w_out[i]
        n_rows = n_lat + n_ctx if keep_ctx else n_lat
        xs = mix_mlp(xs, a_m, b_m, w_out.astype(BF16), norm2_g[l], mods, mlp_w1[l].astype(BF16),
                     mlp_w2[l].astype(BF16), final_g, n_lat, n_rows, l % 2 == 0, l == depth - 1)
    return xs
```

```python
import functools
import math

import jax
import jax.numpy as jnp
from jax import lax
from jax.experimental import pallas as pl
from jax.experimental.pallas import tpu as pltpu

F32 = jnp.float32
BF16 = jnp.bfloat16
HIGHEST = lax.Precision.HIGHEST

D_MODEL = 1024
DEPTH = 4
GRID_W = 64
MLP_HIDDEN = 4 * D_MODEL
GROUP_W = D_MODEL // 2
EPS = 1e-6

S5_GROUP = 16
S5_GROUPS = GROUP_W // S5_GROUP
S5_STATE = 64
S5_W = S5_GROUPS * S5_STATE

HY_ORDER = 2
HY_EMB = 33
HY_BANDS = (HY_EMB - 1) // 2
SHORT_K = 3

RW_HEAD = 64
RW_HEADS = GROUP_W // RW_HEAD
RW_LORA_W = 256
RW_LN_EPS = 64e-5
RW_IN = 3 * GROUP_W + RW_LORA_W

DA_HEADS = 4
DA_HEAD = 64
DA_V = 2 * DA_HEAD
DA_SCALE = DA_HEAD ** -0.5
DA_SUBLN_EPS = 1e-5
ROPE_BASE = 10000.0
ROPE_FREQS = DA_HEAD // 4

LANE = 128
SUBLANE = 8
TOKEN_TILE = 256
VMEM_LIMIT = 48 * 1024 * 1024


def _params(*sem):
    return pltpu.CompilerParams(dimension_semantics=sem, vmem_limit_bytes=VMEM_LIMIT)


def _split_bf16(x):
    hi = x.astype(BF16)
    lo = (x - hi.astype(F32)).astype(BF16)
    return hi, lo


def _dot(a, b):
    return jnp.dot(a, b, preferred_element_type=F32)


def _ada_kernel(c_ref, w_ref, b_ref, o_ref):
    c = c_ref[...]
    s = c * jax.nn.sigmoid(c)
    o_ref[0] = jnp.dot(s, w_ref[0], preferred_element_type=F32, precision=HIGHEST) + b_ref[0]


def ada_mods(c_rows, ada_w, ada_b):
    depth, d, n = ada_w.shape
    tn = 1024
    return pl.pallas_call(
        _ada_kernel,
        grid=(depth, n // tn),
        in_specs=[pl.BlockSpec((16, d), lambda l, j: (0, 0)),
                  pl.BlockSpec((1, d, tn), lambda l, j: (l, 0, j)),
                  pl.BlockSpec((1, 1, tn), lambda l, j: (l, 0, j))],
        out_specs=pl.BlockSpec((1, 16, tn), lambda l, j: (l, 0, j)),
        out_shape=jax.ShapeDtypeStruct((depth, 16, n), F32),
        compiler_params=_params("arbitrary", "arbitrary"),
        name="ada_mods",
    )(c_rows, ada_w, ada_b.reshape(depth, 1, n))


def _norm_mod(x, g, ml, mc, is_ctx, k):
    ms = jnp.mean(x * x, axis=-1, keepdims=True)
    xn = x * lax.rsqrt(ms + EPS) * g
    m = jnp.where(is_ctx, mc, ml)
    return xn * (1.0 + m[k + 1:k + 2]) + m[k:k + 1]


def _inproj_kernel(x_ref, g_ref, ml_ref, mc_ref, w_ref, *o_refs, n_lat_tiles, n_tmaj):
    i = pl.program_id(1)
    h = _norm_mod(x_ref[0], g_ref[...], ml_ref[0], mc_ref[0], i >= n_lat_tiles, 0)
    r = _dot(h.astype(BF16), w_ref[...])
    if n_tmaj:
        o_refs[0][...] = r[:, :n_tmaj]
        o_refs[1][0] = r[:, n_tmaj:]
    else:
        o_refs[0][0] = r


def in_proj(x, g, mods, w, n_lat, n_tmaj):
    b, t, d = x.shape
    n = w.shape[1]
    tm = TOKEN_TILE
    if n_tmaj:
        out_shape = [jax.ShapeDtypeStruct((t, b * n_tmaj), F32), jax.ShapeDtypeStruct((b, t, n - n_tmaj), F32)]
        out_specs = [pl.BlockSpec((tm, n_tmaj), lambda bb, i: (i, bb)),
                     pl.BlockSpec((1, tm, n - n_tmaj), lambda bb, i: (bb, i, 0))]
    else:
        out_shape = [jax.ShapeDtypeStruct((b, t, n), F32)]
        out_specs = [pl.BlockSpec((1, tm, n), lambda bb, i: (bb, i, 0))]
    return pl.pallas_call(
        functools.partial(_inproj_kernel, n_lat_tiles=n_lat // tm, n_tmaj=n_tmaj),
        grid=(b, t // tm),
        in_specs=[pl.BlockSpec((1, tm, d), lambda bb, i: (bb, i, 0)),
                  pl.BlockSpec((1, d), lambda bb, i: (0, 0)),
                  pl.BlockSpec((1, 6, d), lambda bb, i: (bb, 0, 0)),
                  pl.BlockSpec((1, 6, d), lambda bb, i: (b, 0, 0)),
                  pl.BlockSpec((d, n), lambda bb, i: (0, 0), pipeline_mode=pl.Buffered(1))],
        out_specs=out_specs,
        out_shape=out_shape,
        compiler_params=_params("arbitrary", "arbitrary"),
        name="in_proj",
    )(x, g.reshape(1, d), mods, mods, w)


MLP_CHUNK = 1024


def _mix_mlp_kernel(x_ref, a_ref, b_ref, wa_ref, wb_ref, g_ref, ml_ref, mc_ref, w1_ref, w2_ref, fg_ref,
                    o_ref, acc_ref, *, n_lat_tiles, final):
    i = pl.program_id(1)
    is_ctx = i >= n_lat_tiles
    m = jnp.where(is_ctx, mc_ref[0], ml_ref[0])
    mix = _dot(a_ref[...].astype(BF16), wa_ref[...]) + _dot(b_ref[0].astype(BF16), wb_ref[...])
    x = x_ref[0] + m[2:3] * mix
    h = _norm_mod(x, g_ref[...], ml_ref[0], mc_ref[0], is_ctx, 3).astype(BF16)
    for k in range(w1_ref.shape[1] // MLP_CHUNK):
        ks = slice(k * MLP_CHUNK, (k + 1) * MLP_CHUNK)
        hid = jnp.maximum(_dot(h, w1_ref[:, ks]), 0.0)
        part = _dot((hid * hid).astype(BF16), w2_ref[ks, :])
        if k == 0:
            acc_ref[...] = part
        else:
            acc_ref[...] += part
    y = x + m[5:6] * acc_ref[...]
    if final:
        ms = jnp.mean(y * y, axis=-1, keepdims=True)
        y = y * lax.rsqrt(ms + EPS) * fg_ref[...]
    o_ref[0] = y


def mix_mlp(x, a, bm, w_out, g, mods, w1, w2, final_g, n_lat, n_rows, final):
    b, t, d = x.shape
    wd = bm.shape[-1]
    hid = w1.shape[1]
    tm = TOKEN_TILE
    once = pl.Buffered(1)
    return pl.pallas_call(
        functools.partial(_mix_mlp_kernel, n_lat_tiles=n_lat // tm, final=final),
        grid=(b, n_rows // tm),
        in_specs=[pl.BlockSpec((1, tm, d), lambda bb, i: (bb, i, 0)),
                  pl.BlockSpec((tm, wd), lambda bb, i: (i, bb)),
                  pl.BlockSpec((1, tm, wd), lambda bb, i: (bb, i, 0)),
                  pl.BlockSpec((wd, d), lambda bb, i: (0, 0), pipeline_mode=once),
                  pl.BlockSpec((wd, d), lambda bb, i: (1, 0), pipeline_mode=once),
                  pl.BlockSpec((1, d), lambda bb, i: (0, 0)),
                  pl.BlockSpec((1, 6, d), lambda bb, i: (bb, 0, 0)),
                  pl.BlockSpec((1, 6, d), lambda bb, i: (b, 0, 0)),
                  pl.BlockSpec((d, hid), lambda bb, i: (0, 0), pipeline_mode=once),
                  pl.BlockSpec((hid, d), lambda bb, i: (0, 0), pipeline_mode=once),
                  pl.BlockSpec((1, d), lambda bb, i: (0, 0))],
        out_specs=pl.BlockSpec((1, tm, d), lambda bb, i: (bb, i, 0)),
        out_shape=jax.ShapeDtypeStruct((b, n_rows, d), F32),
        scratch_shapes=[pltpu.VMEM((tm, d), F32)],
        compiler_params=_params("arbitrary", "arbitrary"),
        name="mix_mlp",
    )(x, a, bm, w_out, w_out, g.reshape(1, d), mods, mods, w1, w2, final_g.reshape(1, d))


def _conv3(x, w, first, last):
    rows = x.shape[0]
    xm = jnp.where(first, 0.0, pltpu.roll(x, 1, 0))
    xp = jnp.where(last, 0.0, pltpu.roll(x, rows - 1, 0))
    return xm * w[0:1] + x * w[1:2] + xp * w[2:3]


def _shortconv_kernel(x_ref, w_ref, o_ref, *, n_lat):
    x = x_ref[0]
    t = x.shape[0]
    row = lax.broadcasted_iota(jnp.int32, (t, 1), 0)
    first = (row == 0) | (row == n_lat)
    last = (row == n_lat - 1) | (row == t - 1)
    o_ref[0] = _conv3(x, w_ref[...], first, last)


def short_conv(p, w, n_lat):
    b, t, _ = p.shape
    c = w.shape[1]
    cb = GROUP_W
    return pl.pallas_call(
        functools.partial(_shortconv_kernel, n_lat=n_lat),
        grid=(b, c // cb),
        in_specs=[pl.BlockSpec((1, t, cb), lambda bb, j: (bb, 0, j)),
                  pl.BlockSpec((SHORT_K, cb), lambda bb, j: (0, j))],
        out_specs=pl.BlockSpec((1, t, cb), lambda bb, j: (j, 0, bb)),
        out_shape=jax.ShapeDtypeStruct((c // cb, t, b * cb), F32),
        compiler_params=_params("arbitrary", "arbitrary"),
        name="short_conv",
    )(p, w)


S5_KB = LANE
S5_NB = GROUP_W // S5_KB
S5_SB = S5_W // S5_NB


def _s5_kernel(*refs, tc, nb, reverse, finish):
    if finish:
        (u_ref, wbr_ref, wbi_ref, lr_ref, li_ref, wcr_ref, wci_ref, yf_ref, d_ref, gw_ref, gb_ref,
         o_ref, hr_s, hi_s, sr_s, si_s) = refs
    else:
        (u_ref, wbr_ref, wbi_ref, lr_ref, li_ref, wcr_ref, wci_ref,
         o_ref, hr_s, hi_s, sr_s, si_s) = refs

    @pl.when(pl.program_id(0) == 0)
    def _():
        sr_s[...] = jnp.zeros_like(sr_s)
        si_s[...] = jnp.zeros_like(si_s)

    u = u_ref[...]
    ub = u.astype(BF16)
    for j in range(S5_NB):
        uj = ub[:, j * S5_KB:(j + 1) * S5_KB]
        hr_s[:, j * S5_SB:(j + 1) * S5_SB] = _dot(uj, wbr_ref[j])
        hi_s[:, j * S5_SB:(j + 1) * S5_SB] = _dot(uj, wbi_ref[j])

    for s in range(S5_NB):
        sl = slice(s * S5_SB, (s + 1) * S5_SB)
        lr = jnp.broadcast_to(lr_ref[:, sl], (nb, S5_SB))
        li = jnp.broadcast_to(li_ref[:, sl], (nb, S5_SB))

        def body(k, carry, sl=sl, lr=lr, li=li):
            hr, hi = carry
            t = (tc - 1 - k) if reverse else k
            r0 = pl.multiple_of(t * nb, nb)
            nr = lr * hr - li * hi + hr_s[pl.ds(r0, nb), sl]
            ni = lr * hi + li * hr + hi_s[pl.ds(r0, nb), sl]
            hr_s[pl.ds(r0, nb), sl] = nr
            hi_s[pl.ds(r0, nb), sl] = ni
            return nr, ni

        hr, hi = lax.fori_loop(0, tc, body, (sr_s[:, sl], si_s[:, sl]), unroll=4)
        sr_s[:, sl] = hr
        si_s[:, sl] = hi

    for j in range(S5_NB):
        sl = slice(j * S5_SB, (j + 1) * S5_SB)
        co = slice(j * S5_KB, (j + 1) * S5_KB)
        y = _dot(hr_s[:, sl].astype(BF16), wcr_ref[j]) + _dot(hi_s[:, sl].astype(BF16), wci_ref[j])
        if finish:
            o_ref[:, co] = y + yf_ref[:, co] + u[:, co] * d_ref[:, co]
        else:
            o_ref[:, co] = y

    if finish:
        y = o_ref[...]
        y = 0.5 * y * (1.0 + lax.erf(y * (2.0 ** -0.5)))
        z = _dot(y.astype(BF16), gw_ref[...]) + gb_ref[...]
        o_ref[...] = y * jax.nn.sigmoid(z)


def s5_scan_call(u2, tabs, n_lat_steps, n_steps, nb, reverse, extra):
    tc = 32
    rows = tc * nb
    n_chunks = n_steps // tc
    n_lat_chunks = n_lat_steps // tc
    n_ctx_chunks = n_chunks - n_lat_chunks
    if reverse:
        def cmap(c):
            return (n_chunks - 1 - c, 0)
    else:
        def cmap(c):
            return (jnp.where(c < n_ctx_chunks, n_lat_chunks + c, c - n_ctx_chunks), 0)
    wbr, wbi, lr, li, wcr, wci = tabs
    full3 = lambda a: pl.BlockSpec(a.shape, lambda c: (0, 0, 0))
    full2 = lambda a: pl.BlockSpec(a.shape, lambda c: (0, 0))
    in_specs = [pl.BlockSpec((rows, GROUP_W), cmap), full3(wbr), full3(wbi), full2(lr), full2(li),
                full3(wcr), full3(wci)]
    args = [u2, wbr, wbi, lr, li, wcr, wci]
    finish = extra is not None
    if finish:
        yf, dsk, gw, gb = extra
        in_specs += [pl.BlockSpec((rows, GROUP_W), cmap), full2(dsk), full2(gw), full2(gb)]
        args += [yf, dsk, gw, gb]
    return pl.pallas_call(
        functools.partial(_s5_kernel, tc=tc, nb=nb, reverse=reverse, finish=finish),
        grid=(n_chunks,),
        in_specs=in_specs,
        out_specs=pl.BlockSpec((rows, GROUP_W), cmap),
        out_shape=jax.ShapeDtypeStruct(u2.shape, F32),
        scratch_shapes=[pltpu.VMEM((rows, S5_W), F32), pltpu.VMEM((rows, S5_W), F32),
                        pltpu.VMEM((nb, S5_W), F32), pltpu.VMEM((nb, S5_W), F32)],
        compiler_params=_params("arbitrary"),
        name="s5_bwd_finish" if finish else "s5_fwd",
    )(*args)


def _block_diag(x):
    nblk, g, r, c = x.shape
    eye = jnp.eye(g, dtype=x.dtype)
    return jnp.einsum('jgrc,gh->jgrhc', x, eye).reshape(nblk, g * r, g * c)


def s5_tables(lam_re, lam_im, log_dt, b_re, b_im, c_re, c_im):
    lam_re = jnp.minimum(lam_re.astype(F32), -1e-4)
    lam_im = lam_im.astype(F32)
    dt = jnp.exp(log_dt.astype(F32))[:, None]
    mag = jnp.exp(lam_re * dt)
    lb_re = mag * jnp.cos(lam_im * dt)
    lb_im = mag * jnp.sin(lam_im * dt)
    den = lam_re * lam_re + lam_im * lam_im
    f_re = ((lb_re - 1.0) * lam_re + lb_im * lam_im) / den
    f_im = (lb_im * lam_re - (lb_re - 1.0) * lam_im) / den
    b_re = b_re.astype(F32)
    b_im = b_im.astype(F32)
    bb_re = f_re[..., None] * b_re - f_im[..., None] * b_im
    bb_im = f_re[..., None] * b_im + f_im[..., None] * b_re
    gpb = S5_KB // S5_GROUP
    to_b = lambda x: _block_diag(
        jnp.swapaxes(x, 1, 2).reshape(S5_NB, gpb, S5_GROUP, S5_STATE)).astype(BF16)
    to_c = lambda x: _block_diag(
        jnp.swapaxes(x.astype(F32), 1, 2).reshape(S5_NB, gpb, S5_STATE, S5_GROUP)).astype(BF16)
    return (to_b(bb_re), to_b(bb_im), lb_re.reshape(1, S5_W), lb_im.reshape(1, S5_W),
            to_c(c_re), to_c(-c_im.astype(F32)))


def dft_tables(n):
    nn = 2 * n
    f = jnp.arange(n, dtype=jnp.int32)[:, None]
    t = jnp.arange(n, dtype=jnp.int32)[None, :]
    q = 1 << (int(math.log2(n)) // 2)

    def factor(tt):
        ang = ((f * tt[None, :]) % nn).astype(F32) * (2.0 * math.pi / nn)
        return jnp.cos(ang), jnp.sin(ang)

    c1, s1 = factor(jnp.arange(n // q, dtype=jnp.int32) * q)
    c0, s0 = factor(jnp.arange(q, dtype=jnp.int32))
    cos = (c1[:, :, None] * c0[:, None, :] - s1[:, :, None] * s0[:, None, :]).reshape(n, n)
    sin = (s1[:, :, None] * c0[:, None, :] + c1[:, :, None] * s0[:, None, :]).reshape(n, n)
    nyq = jnp.where(t % 2 == 0, 1.0, -1.0).astype(F32)
    msin = jnp.where(f == 0, nyq, -sin)
    fwd = jnp.concatenate([cos, msin], axis=0)
    wf = jnp.where(jnp.arange(n) == 0, 1.0 / nn, 2.0 / nn).astype(F32)
    return fwd, wf


def _filtdft_kernel(fh_ref, fl_ref, hh_ref, hl_ref, o_ref):
    fh = fh_ref[...]
    o_ref[...] = _dot(fh, hh_ref[...]) + _dot(fh, hl_ref[...]) + _dot(fl_ref[...], hh_ref[...])


def filter_dft(fwd_hi, fwd_lo, h):
    n2, n = fwd_hi.shape
    c = h.shape[1]
    tr = min(256, n2)
    tcn = 512
    hh, hl = _split_bf16(h)
    return pl.pallas_call(
        _filtdft_kernel,
        grid=(n2 // tr, c // tcn),
        in_specs=[pl.BlockSpec((tr, n), lambda i, j: (i, 0)),
                  pl.BlockSpec((tr, n), lambda i, j: (i, 0)),
                  pl.BlockSpec((n, tcn), lambda i, j: (0, j)),
                  pl.BlockSpec((n, tcn), lambda i, j: (0, j))],
        out_specs=pl.BlockSpec((tr, tcn), lambda i, j: (i, j)),
        out_shape=jax.ShapeDtypeStruct((n2, c), F32),
        compiler_params=_params("arbitrary", "arbitrary"),
        name="filter_dft",
    )(fwd_hi, fwd_lo, hh, hl)


def hyena_filter_response(n, fwd_hi, fwd_lo, wf, f_w1, f_b1, f_w2, f_b2, f_w3, f_freq, log_decay):
    t = jnp.linspace(0.0, 1.0, n, dtype=F32)[:, None]
    w = 2.0 * math.pi * jnp.arange(n, dtype=F32)[:, None] / n
    bands = jnp.linspace(1e-4, HY_BANDS - 1, HY_BANDS, dtype=F32)[None, :]
    z = jnp.concatenate([t, jnp.cos(bands * w), -jnp.sin(bands * w)], axis=-1)
    freq = f_freq.astype(F32)
    hid = jnp.sin(freq[0] * (jnp.dot(z, f_w1.astype(F32), precision=HIGHEST) + f_b1.astype(F32)))
    hid = jnp.sin(freq[1] * (jnp.dot(hid, f_w2.astype(F32), precision=HIGHEST) + f_b2.astype(F32)))
    h = jnp.dot(hid, f_w3.astype(F32), precision=HIGHEST).reshape(n, 2, HY_ORDER, GROUP_W)
    h = h * jnp.exp(-t[:, :, None, None] * jnp.exp(log_decay.astype(F32)))
    cw = HY_ORDER * GROUP_W
    h_fwd = h[:, 0].reshape(n, cw)
    h_bwd = h[:, 1].reshape(n, cw).at[0].set(0.0)
    resp = filter_dft(fwd_hi, fwd_lo, jnp.concatenate([h_fwd, h_bwd], axis=1))
    a_re, a_im = resp[:n, :cw], resp[n:, :cw]
    b_re, b_im = resp[:n, cw:], resp[n:, cw:]
    k_re = a_re + b_re
    first = (jnp.arange(n) == 0)[:, None]
    k_im = jnp.where(first, a_im + b_im, a_im - b_im)
    k_re = (k_re * wf[:, None]).reshape(n, HY_ORDER, GROUP_W).transpose(1, 0, 2)
    k_im = (k_im * wf[:, None]).reshape(n, HY_ORDER, GROUP_W).transpose(1, 0, 2)
    return k_re, k_im


def _hyena_kernel(z0_ref, x1_ref, x2_ref, cw_ref, cb_ref, fw_ref, inv_ref, kr_ref, ki_ref, bias_ref, *rest):
    o_ref, zb_s, zf_s, acc_s = rest[-4:]
    o = pl.program_id(1)
    f = pl.program_id(2)
    nf = pl.num_programs(2)
    c = GROUP_W

    def stream(ref, s):
        x = ref[0]
        row = lax.broadcasted_iota(jnp.int32, (x.shape[0], 1), 0)
        cols = slice(s * c, (s + 1) * c)
        return _conv3(x, cw_ref[:, cols], row == 0, row == x.shape[0] - 1) + cb_ref[:, cols]

    @pl.when((o == 0) & (f == 0))
    def _():
        z0 = stream(z0_ref, 0)
        zf_s[...] = z0
        zb_s[...] = z0.astype(BF16)

    @pl.when(f == 0)
    def _():
        acc_s[...] = jnp.zeros_like(acc_s)

    xf = _dot(fw_ref[0], zb_s[...])
    fb = xf.shape[0] // 2
    xr = xf[:fb]
    xi = xf[fb:]
    kr = kr_ref[0]
    ki = ki_ref[0]
    row = lax.broadcasted_iota(jnp.int32, (fb, 1), 0)
    packed = (row == 0) & (f == 0)
    yr = jnp.where(packed, xr * kr, xr * kr - xi * ki)
    yi = jnp.where(packed, xi * ki, xr * ki + xi * kr)
    yf = jnp.concatenate([yr, yi], axis=0).astype(BF16)
    acc_s[...] += _dot(inv_ref[0], yf)

    @pl.when(f == nf - 1)
    def _():
        bias = bias_ref[...]

        @pl.when(o == 0)
        def _():
            z1 = stream(x1_ref, 1) * (acc_s[...] + zf_s[...] * bias[0:1])
            zf_s[...] = z1
            zb_s[...] = z1.astype(BF16)

        @pl.when(o == 1)
        def _():
            o_ref[0] = stream(x2_ref, 2) * (acc_s[...] + zf_s[...] * bias[1:2])


HY_FREQ_BLOCK = 256


def dft_blocks(fwd_hi):
    n = fwd_hi.shape[1]
    fb = min(HY_FREQ_BLOCK, n)
    blocks = fwd_hi.reshape(2, n // fb, fb, n).transpose(1, 0, 2, 3).reshape(n // fb, 2 * fb, n)
    return blocks, blocks.transpose(0, 2, 1)


def hyena_call(p, n, row_blk, conv_w, conv_b, fwd_blk, inv_blk, k_re, k_im, bias, out_prev):
    b, t_total, _ = p.shape
    c = GROUP_W
    nf, fb2, _ = fwd_blk.shape
    fb = fb2 // 2
    once = pl.Buffered(1)
    in_specs = [pl.BlockSpec((1, n, c), lambda bb, o, f: (bb, row_blk, 0), pipeline_mode=once),
                pl.BlockSpec((1, n, c), lambda bb, o, f: (bb, row_blk, 1), pipeline_mode=once),
                pl.BlockSpec((1, n, c), lambda bb, o, f: (bb, row_blk, 2), pipeline_mode=once),
                pl.BlockSpec((SHORT_K, 3 * c), lambda bb, o, f: (0, 0)),
                pl.BlockSpec((1, 3 * c), lambda bb, o, f: (0, 0)),
                pl.BlockSpec((1, fb2, n), lambda bb, o, f: (f, 0, 0)),
                pl.BlockSpec((1, n, fb2), lambda bb, o, f: (f, 0, 0)),
                pl.BlockSpec((1, fb, c), lambda bb, o, f: (o, f, 0)),
                pl.BlockSpec((1, fb, c), lambda bb, o, f: (o, f, 0)),
                pl.BlockSpec((HY_ORDER, c), lambda bb, o, f: (0, 0))]
    args = [p, p, p, conv_w, conv_b.reshape(1, 3 * c), fwd_blk, inv_blk, k_re, k_im, bias]
    aliases = {}
    if out_prev is not None:
        in_specs.append(pl.BlockSpec(memory_space=pl.ANY))
        args.append(out_prev)
        aliases = {len(args) - 1: 0}
    return pl.pallas_call(
        _hyena_kernel,
        grid=(b, HY_ORDER, nf),
        in_specs=in_specs,
        out_specs=pl.BlockSpec((1, n, c), lambda bb, o, f: (bb, row_blk, 0)),
        out_shape=jax.ShapeDtypeStruct((b, t_total, c), F32),
        scratch_shapes=[pltpu.VMEM((n, c), BF16), pltpu.VMEM((n, c), F32), pltpu.VMEM((n, c), F32)],
        input_output_aliases=aliases,
        compiler_params=_params("arbitrary", "arbitrary", "arbitrary"),
        name="hyena",
    )(*args)


def _rwkv_lora_kernel(p_ref, ww_ref, wa_ref, wg_ref, w0_ref, a0_ref, wp0_ref, wp1_ref, ap0_ref, ap1_ref,
                      g_ref):
    x = p_ref[0]
    wa_in = x[:, :LANE]
    th = jnp.tanh(wa_in).astype(BF16)
    lin = wa_in.astype(BF16)
    w0 = w0_ref[...]
    a0 = a0_ref[...]
    wp0_ref[...] = w0[0:1] + _dot(th, ww_ref[0])
    wp1_ref[...] = w0[1:2] + _dot(th, ww_ref[1])
    ap0_ref[...] = a0[0:1] + _dot(lin, wa_ref[0])
    ap1_ref[...] = a0[1:2] + _dot(lin, wa_ref[1])
    g_ref[...] = _dot(jax.nn.sigmoid(x[:, LANE:]).astype(BF16), wg_ref[...])


def rwkv_lora(p, w_up, a_up, g_up, w0, a0):
    b, t, _ = p.shape
    c = GROUP_W
    tm = TOKEN_TILE
    zeros = jnp.zeros((2, 64, c), F32)
    ww = jnp.concatenate([w_up.astype(F32), zeros], axis=1).astype(BF16)
    wa = jnp.concatenate([zeros, a_up.astype(F32)], axis=1).astype(BF16)
    out = jax.ShapeDtypeStruct((t, b * c), F32)
    ospec = pl.BlockSpec((tm, c), lambda bb, i: (i, bb))
    return pl.pallas_call(
        _rwkv_lora_kernel,
        grid=(b, t // tm),
        in_specs=[pl.BlockSpec((1, tm, RW_LORA_W), lambda bb, i: (bb, i, 3 * c // RW_LORA_W)),
                  pl.BlockSpec((2, LANE, c), lambda bb, i: (0, 0, 0)),
                  pl.BlockSpec((2, LANE, c), lambda bb, i: (0, 0, 0)),
                  pl.BlockSpec((LANE, c), lambda bb, i: (0, 0)),
                  pl.BlockSpec((2, c), lambda bb, i: (0, 0)),
                  pl.BlockSpec((2, c), lambda bb, i: (0, 0))],
        out_specs=[ospec] * 5,
        out_shape=[out] * 5,
        compiler_params=_params("arbitrary", "arbitrary"),
        name="rwkv_lora",
    )(p, ww, wa, g_up.astype(BF16), w0, a0)


def _rwkv_scan_kernel(wpf_ref, wpb_ref, apf_ref, apb_ref, kf_ref, kb_ref, vf_ref, vb_ref, rf_ref, rb_ref,
                      kkc_ref, kac_ref, yf_ref, yb_ref, s_ref, w_s, kk_s, b_s, ke_s, r_s, *, tb):
    n = RW_HEAD

    @pl.when(pl.program_id(0) == 0)
    def _():
        s_ref[...] = jnp.zeros_like(s_ref)

    def both(f_ref, b_ref, t):
        return jnp.concatenate([f_ref[0, t], b_ref[0, tb - 1 - t]], axis=0).T

    def prepare(t, slot):
        k = both(kf_ref, kb_ref, t)
        a = jax.nn.sigmoid(both(apf_ref, apb_ref, t))
        kk = k * kkc_ref[...]
        nrm = jnp.sqrt(jnp.sum(kk * kk, axis=0, keepdims=True))
        kk = kk / jnp.maximum(nrm, 1e-12)
        wl = -jax.nn.softplus(-both(wpf_ref, wpb_ref, t)) - 0.5
        w_s[slot] = jnp.exp(-jnp.exp(wl))
        kk_s[slot] = kk
        b_s[slot] = kk * a
        ke_s[slot] = k * (1.0 + (a - 1.0) * kac_ref[...])
        r_s[slot] = both(rf_ref, rb_ref, t)

    prepare(0, 0)
    sa0 = jnp.zeros((n, LANE), F32)
    for j in range(n):
        sa0 = sa0 + s_ref[j] * kk_s[0, j:j + 1, :]

    def step(t, sa):
        slot = t % 2
        nxt = 1 - slot
        prepare(jnp.minimum(t + 1, tb - 1), nxt)
        v = both(vf_ref, vb_ref, t)
        y = jnp.zeros((n, LANE), F32)
        sa_next = jnp.zeros((n, LANE), F32)
        for j in range(n):
            row = pl.ds(j, 1)
            s = s_ref[j] * w_s[slot, row, :] - sa * b_s[slot, row, :] + v * ke_s[slot, row, :]
            s_ref[j] = s
            y = y + s * r_s[slot, row, :]
            sa_next = sa_next + s * kk_s[nxt, row, :]
        yt = y.T
        yf_ref[t] = yt[:n]
        yb_ref[tb - 1 - t] = yt[n:]
        return sa_next

    lax.fori_loop(0, tb, step, sa0)


def rwkv_scan(wp0, wp1, ap0, ap1, rkv, kkc, kac, n_lat):
    t = wp0.shape[0]
    tb = 32
    n = RW_HEAD
    n_blk = t // tb
    n_lat_blk = n_lat // tb
    n_ctx_blk = n_blk - n_lat_blk
    tile = lambda a: a.reshape(1, t, -1, n)
    rkv4 = rkv.reshape(3, t, -1, n)
    nc = rkv4.shape[2]

    def fblk(c):
        return jnp.where(c < n_ctx_blk, n_lat_blk + c, c - n_ctx_blk)

    def rblk(c):
        return n_blk - 1 - c

    def pair(q):
        return [pl.BlockSpec((1, tb, nc, n), lambda c: (q, fblk(c), 0, 0)),
                pl.BlockSpec((1, tb, nc, n), lambda c: (q, rblk(c), 0, 0))]

    cst = pl.BlockSpec((n, 2 * nc), lambda c: (0, 0))
    buf = pltpu.VMEM((2, n, 2 * nc), F32)
    yf, yb = pl.pallas_call(
        functools.partial(_rwkv_scan_kernel, tb=tb),
        grid=(n_blk,),
        in_specs=[pair(0)[0], pair(0)[1]] * 2 + pair(1) + pair(2) + pair(0) + [cst] * 2,
        out_specs=[pl.BlockSpec((tb, nc, n), lambda c: (fblk(c), 0, 0)),
                   pl.BlockSpec((tb, nc, n), lambda c: (rblk(c), 0, 0))],
        out_shape=[jax.ShapeDtypeStruct((t, nc, n), F32)] * 2,
        scratch_shapes=[pltpu.VMEM((n, n, 2 * nc), F32), buf, buf, buf, buf, buf],
        compiler_params=_params("arbitrary"),
        name="rwkv_scan",
    )(tile(wp0), tile(wp1), tile(ap0), tile(ap1), rkv4, rkv4, rkv4, rkv4, rkv4, rkv4, kkc, kac)
    return yf.reshape(t, nc * n), yb.reshape(t, nc * n)


def _rwkv_post_kernel(yf_ref, yb_ref, r_ref, k_ref, v_ref, ap0_ref, ap1_ref, g_ref, m_ref, ka_ref, rk_ref,
                      lg_ref, lb_ref, o_ref):
    m = m_ref[...]

    def head_mean(x):
        hi, lo = _split_bf16(x)
        return _dot(hi, m) + _dot(lo, m)

    y = yf_ref[...] + yb_ref[...]
    d = y - head_mean(y)
    var = head_mean(d * d)
    yn = d * lax.rsqrt(var + RW_LN_EPS) * lg_ref[...] + lb_ref[...]
    a_sum = jax.nn.sigmoid(ap0_ref[...]) + jax.nn.sigmoid(ap1_ref[...])
    k_sum = k_ref[0] * (2.0 + (a_sum - 2.0) * ka_ref[...])
    bonus = head_mean(r_ref[0] * k_sum * rk_ref[...]) * float(RW_HEAD) * v_ref[0]
    o_ref[...] = (yn + bonus) * g_ref[...]


def rwkv_post(yf, yb, rkv, ap0, ap1, gate, k_a, r_k, ln_g, ln_b, n_rows, b):
    t = yf.shape[0]
    c = GROUP_W
    tm = TOKEN_TILE
    hm = jnp.kron(jnp.eye(RW_HEADS, dtype=F32), jnp.full((RW_HEAD, RW_HEAD), 1.0 / RW_HEAD, F32)).astype(BF16)
    tok = pl.BlockSpec((tm, c), lambda bb, i: (i, bb))
    vec = pl.BlockSpec((1, c), lambda bb, i: (0, 0))
    rkv_spec = lambda q: pl.BlockSpec((1, tm, c), lambda bb, i: (q, i, bb))
    return pl.pallas_call(
        _rwkv_post_kernel,
        grid=(b, n_rows // tm),
        in_specs=[tok, tok, rkv_spec(0), rkv_spec(1), rkv_spec(2), tok, tok, tok,
                  pl.BlockSpec((c, c), lambda bb, i: (0, 0)), vec, vec, vec, vec],
        out_specs=tok,
        out_shape=jax.ShapeDtypeStruct((t, b * c), F32),
        compiler_params=_params("arbitrary", "arbitrary"),
        name="rwkv_post",
    )(yf, yb, rkv, rkv, rkv, ap0, ap1, gate, hm, k_a.reshape(1, c), r_k.reshape(1, c),
      ln_g.reshape(1, c), ln_b.reshape(1, c))


def _chain_const(x, b):
    return jnp.tile(x.astype(F32).reshape(RW_HEADS, RW_HEAD).T, (1, 2 * b))


def _rope(x, cos, sin):
    lane = lax.broadcasted_iota(jnp.int32, (1, LANE), 1)
    first = (lane % (2 * ROPE_FREQS)) < ROPE_FREQS
    partner = jnp.where(first, pltpu.roll(x, LANE - ROPE_FREQS, 1), pltpu.roll(x, ROPE_FREQS, 1))
    return x * cos + partner * sin


def _attn_tile(q, k, v, lam):
    lane = lax.broadcasted_iota(jnp.int32, (1, LANE), 1)
    m0 = lane < DA_HEAD
    q0 = jnp.where(m0, q, 0.0).astype(BF16)
    q1 = jnp.where(m0, 0.0, q).astype(BF16)
    dn = (((1,), (1,)), ((), ()))
    s0 = lax.dot_general(q0, k, dn, preferred_element_type=F32)
    s1 = lax.dot_general(q1, k, dn, preferred_element_type=F32)
    p0 = jnp.exp(s0 - jnp.max(s0, axis=-1, keepdims=True))
    p1 = jnp.exp(s1 - jnp.max(s1, axis=-1, keepdims=True))
    w = p0 / jnp.sum(p0, axis=-1, keepdims=True) - lam * (p1 / jnp.sum(p1, axis=-1, keepdims=True))
    return _dot(w.astype(BF16), v)


def _attn_kernel(q_ref, k_ref, v_ref, cq_ref, sq_ref, ck_ref, sk_ref, lp_ref, g_ref, o_ref,
                 kr_s, vb_s, *, n_lat, n_lat_tiles, lam_init):
    i = pl.program_id(2)

    @pl.when(i == 0)
    def _():
        kr_s[...] = _rope(k_ref[0], ck_ref[...], sk_ref[...]).astype(BF16)
        vb_s[...] = v_ref[0].astype(BF16)

    lp = lp_ref[...]
    lam = (jnp.exp(jnp.sum(lp[0:1] * lp[1:2], axis=-1, keepdims=True))
           - jnp.exp(jnp.sum(lp[2:3] * lp[3:4], axis=-1, keepdims=True)) + lam_init)
    q = _rope(q_ref[0], cq_ref[...], sq_ref[...]) * DA_SCALE

    def finish(o):
        on = o * lax.rsqrt(jnp.mean(o * o, axis=-1, keepdims=True) + DA_SUBLN_EPS)
        o_ref[0] = on * g_ref[...] * (1.0 - lam_init)

    @pl.when(i < n_lat_tiles)
    def _():
        finish(_attn_tile(q, kr_s[...], vb_s[...], lam))

    @pl.when(i >= n_lat_tiles)
    def _():
        finish(_attn_tile(q, kr_s[n_lat:], vb_s[n_lat:], lam))


def diff_attention(p, col0, cosf, sins, lam_p, subln_g, lam_init, n_lat, n_rows):
    b, t, _ = p.shape
    tq = TOKEN_TILE
    off = col0 // LANE
    hq = DA_HEADS
    return pl.pallas_call(
        functools.partial(_attn_kernel, n_lat=n_lat, n_lat_tiles=n_lat // tq, lam_init=lam_init),
        grid=(b, hq, n_rows // tq),
        in_specs=[pl.BlockSpec((1, tq, LANE), lambda bb, h, i: (bb, i, off + h)),
                  pl.BlockSpec((1, t, LANE), lambda bb, h, i: (bb, 0, off + hq + h)),
                  pl.BlockSpec((1, t, LANE), lambda bb, h, i: (bb, 0, off + 2 * hq + h)),
                  pl.BlockSpec((tq, LANE), lambda bb, h, i: (i, 0)),
                  pl.BlockSpec((tq, LANE), lambda bb, h, i: (i, 0)),
                  pl.BlockSpec((t, LANE), lambda bb, h, i: (0, 0)),
                  pl.BlockSpec((t, LANE), lambda bb, h, i: (0, 0)),
                  pl.BlockSpec((4, DA_HEAD), lambda bb, h, i: (0, 0)),
                  pl.BlockSpec((1, LANE), lambda bb, h, i: (0, 0))],
        out_specs=pl.BlockSpec((1, tq, LANE), lambda bb, h, i: (bb, i, h)),
        out_shape=jax.ShapeDtypeStruct((b, t, hq * DA_V), F32),
        scratch_shapes=[pltpu.VMEM((t, LANE), BF16), pltpu.VMEM((t, LANE), BF16)],
        compiler_params=_params("arbitrary", "arbitrary", "arbitrary"),
        name="diff_attention",
    )(p, p, p, cosf, sins, cosf, sins, lam_p, subln_g.reshape(1, DA_V))


def rope_tables(n_lat, n_ctx):
    rows = n_lat // GRID_W
    row = jnp.repeat(jnp.arange(rows, dtype=F32), GRID_W)
    col = jnp.tile(jnp.arange(GRID_W, dtype=F32), rows)
    inv = ROPE_BASE ** (-jnp.arange(ROPE_FREQS, dtype=F32) / ROPE_FREQS)
    ang = jnp.stack([row[:, None] * inv, col[:, None] * inv], axis=1)
    cos, sin = jnp.cos(ang), jnp.sin(ang)
    cosf = jnp.concatenate([cos, cos], axis=-1).reshape(n_lat, DA_HEAD)
    sins = jnp.concatenate([-sin, sin], axis=-1).reshape(n_lat, DA_HEAD)
    cosf = jnp.concatenate([jnp.tile(cosf, (1, 2)), jnp.ones((n_ctx, LANE), F32)], axis=0)
    sins = jnp.concatenate([jnp.tile(sins, (1, 2)), jnp.zeros((n_ctx, LANE), F32)], axis=0)
    return cosf, sins


def _even_mixers(xs, mods, keep_ctx, n_lat, norm1_g, w_in, s5p, hyp, dft):
    b, t, _ = xs.shape
    n_ctx = t - n_lat
    u_t, p_h = in_proj(xs, norm1_g, mods, w_in.astype(BF16), n_lat, GROUP_W)

    (lam_re, lam_im, log_dt, b_re, b_im, c_re, c_im, d_skip, glu_w, glu_b) = s5p
    u2 = u_t.reshape(t * b, GROUP_W)
    tabs = [s5_tables(lam_re[d], lam_im[d], log_dt[d], b_re[d], b_im[d], c_re[d], c_im[d]) for d in range(2)]
    y_f = s5_scan_call(u2, tabs[0], n_lat, t, b, False, None)
    a_t = s5_scan_call(u2, tabs[1], n_lat, t, b, True,
                       (y_f, d_skip.reshape(1, GROUP_W), glu_w.astype(BF16), glu_b.reshape(1, GROUP_W)))

    (conv_w, conv_b, f_w1, f_b1, f_w2, f_b2, f_w3, f_freq, log_decay, bias) = hyp
    fh_l, fl_l, (fw_l, inv_l), wf_l = dft[0]
    kr, ki = hyena_filter_response(n_lat, fh_l, fl_l, wf_l, f_w1, f_b1, f_w2, f_b2, f_w3, f_freq, log_decay)
    b_m = hyena_call(p_h, n_lat, 0, conv_w, conv_b, fw_l, inv_l, kr, ki, bias, None)
    if keep_ctx:
        fh_c, fl_c, (fw_c, inv_c), wf_c = dft[1]
        kr, ki = hyena_filter_response(n_ctx, fh_c, fl_c, wf_c, f_w1, f_b1, f_w2, f_b2, f_w3, f_freq,
                                       log_decay)
        b_m = hyena_call(p_h, n_ctx, n_lat // n_ctx, conv_w, conv_b, fw_c, inv_c, kr, ki, bias, b_m)
    return a_t.reshape(t, b * GROUP_W), b_m


def _odd_mixers(xs, mods, keep_ctx, n_lat, lam_init, norm1_g, w_in, rwp, dap, rope):
    b, t, _ = xs.shape
    n_rows = t if keep_ctx else n_lat
    (p,) = in_proj(xs, norm1_g, mods, w_in.astype(BF16), n_lat, 0)

    (conv_w, w0, w_up, a0, a_up, g_up, k_k, k_a, r_k, ln_g, ln_b) = rwp
    rkv = short_conv(p, conv_w, n_lat)
    wp0, wp1, ap0, ap1, gate = rwkv_lora(p, w_up, a_up, g_up, w0, a0)
    yf, yb = rwkv_scan(wp0, wp1, ap0, ap1, rkv, _chain_const(k_k, b), _chain_const(k_a, b), n_lat)
    a_m = rwkv_post(yf, yb, rkv, ap0, ap1, gate, k_a, r_k, ln_g, ln_b, n_rows, b)

    lam_p, subln_g = dap
    b_m = diff_attention(p, RW_IN, rope[0], rope[1], lam_p, subln_g, lam_init, n_lat, n_rows)
    return a_m, b_m


def kernel(x, c, ctx, c_ctx, ada_w, ada_b, norm1_g, norm2_g, mlp_w1, mlp_w2, final_g, ev_w_in, ev_w_out, s5_lam_re, s5_lam_im, s5_log_dt, s5_b_re, s5_b_im, s5_c_re, s5_c_im, s5_d, s5_glu_w, s5_glu_b, hy_conv_w, hy_conv_b, hy_f_w1, hy_f_b1, hy_f_w2, hy_f_b2, hy_f_w3, hy_f_freq, hy_log_decay, hy_bias, od_w_in, od_w_out, rw_conv_w, rw_w0, rw_w_up, rw_a0, rw_a_up, rw_g_up, rw_k_k, rw_k_a, rw_r_k, rw_ln_g, rw_ln_b, da_lam, da_subln_g):
    b, n_lat, d = x.shape
    n_ctx = ctx.shape[1]
    assert d == D_MODEL and b == SUBLANE
    assert n_lat % TOKEN_TILE == 0 and n_ctx % TOKEN_TILE == 0 and n_lat % n_ctx == 0
    xs = jnp.concatenate([x.astype(F32), ctx.astype(F32)], axis=1)

    c_rows = jnp.zeros((16, d), F32).at[:b].set(c.astype(F32)).at[b].set(c_ctx.astype(F32))
    depth = ada_w.shape[0]
    mods_all = ada_mods(c_rows, ada_w, ada_b).reshape(depth, 16, 6, d)

    rope = rope_tables(n_lat, n_ctx)
    dft = []
    for n in (n_lat, n_ctx):
        fwd, wf = dft_tables(n)
        hi, lo = _split_bf16(fwd)
        dft.append((hi, lo, dft_blocks(hi), wf))

    for l in range(depth):
        keep_ctx = l < depth - 1
        i = l // 2
        mods = mods_all[l]
        if l % 2 == 0:
            s5p = (s5_lam_re[i], s5_lam_im[i], s5_log_dt[i], s5_b_re[i], s5_b_im[i], s5_c_re[i], s5_c_im[i],
                   s5_d[i], s5_glu_w[i], s5_glu_b[i])
            hyp = (hy_conv_w[i], hy_conv_b[i], hy_f_w1[i], hy_f_b1[i], hy_f_w2[i], hy_f_b2[i], hy_f_w3[i],
                   hy_f_freq[i], hy_log_decay[i], hy_bias[i])
            a_m, b_m = _even_mixers(xs, mods, keep_ctx, n_lat, norm1_g[l], ev_w_in[i], s5p, hyp, dft)
            w_out = ev_w_out[i]
        else:
            rwp = (rw_conv_w[i], rw_w0[i], rw_w_up[i], rw_a0[i], rw_a_up[i], rw_g_up[i], rw_k_k[i],
                   rw_k_a[i], rw_r_k[i], rw_ln_g[i], rw_ln_b[i])
            lam_init = 0.8 - 0.6 * math.exp(-0.3 * l)
            a_m, b_m = _odd_mixers(xs, mods, keep_ctx, n_lat, lam_init, norm1_g[l], od_w_in[i], rwp,
                                   (da_lam[i], da_subln_g[i]), rope)
            w_out = od_w_out[i]
        n_rows = n_lat + n_ctx if keep_ctx else n_lat
        xs = mix_mlp(xs, a_m, b_m, w_out.astype(BF16), norm2_g[l], mods, mlp_w1[l].astype(BF16),
                     mlp_w2[l].astype(BF16), final_g, n_lat, n_rows, l == depth - 1)
    return xs
```

```python
import functools
import math

import jax
import jax.numpy as jnp
from jax import lax
from jax.experimental import pallas as pl
from jax.experimental.pallas import tpu as pltpu

F32 = jnp.float32
BF16 = jnp.bfloat16
HIGHEST = lax.Precision.HIGHEST

D_MODEL = 1024
DEPTH = 4
GRID_W = 64
MLP_HIDDEN = 4 * D_MODEL
GROUP_W = D_MODEL // 2
EPS = 1e-6

S5_GROUP = 16
S5_GROUPS = GROUP_W // S5_GROUP
S5_STATE = 64
S5_W = S5_GROUPS * S5_STATE

HY_ORDER = 2
HY_EMB = 33
HY_BANDS = (HY_EMB - 1) // 2
SHORT_K = 3

RW_HEAD = 64
RW_HEADS = GROUP_W // RW_HEAD
RW_LORA_W = 256
RW_LN_EPS = 64e-5
RW_IN = 3 * GROUP_W + RW_LORA_W

DA_HEADS = 4
DA_HEAD = 64
DA_V = 2 * DA_HEAD
DA_SCALE = DA_HEAD ** -0.5
DA_SUBLN_EPS = 1e-5
ROPE_BASE = 10000.0
ROPE_FREQS = DA_HEAD // 4

LANE = 128
SUBLANE = 8
TOKEN_TILE = 256
VMEM_LIMIT = 48 * 1024 * 1024


def _params(*sem):
    return pltpu.CompilerParams(dimension_semantics=sem, vmem_limit_bytes=VMEM_LIMIT)


def _split_bf16(x):
    hi = x.astype(BF16)
    lo = (x - hi.astype(F32)).astype(BF16)
    return hi, lo


def _dot(a, b):
    return jnp.dot(a, b, preferred_element_type=F32)


def _ada_kernel(c_ref, w_ref, b_ref, o_ref):
    c = c_ref[...]
    s = c * jax.nn.sigmoid(c)
    o_ref[0] = jnp.dot(s, w_ref[0], preferred_element_type=F32, precision=HIGHEST) + b_ref[0]


def ada_mods(c_rows, ada_w, ada_b):
    depth, d, n = ada_w.shape
    tn = 1024
    return pl.pallas_call(
        _ada_kernel,
        grid=(depth, n // tn),
        in_specs=[pl.BlockSpec((16, d), lambda l, j: (0, 0)),
                  pl.BlockSpec((1, d, tn), lambda l, j: (l, 0, j)),
                  pl.BlockSpec((1, 1, tn), lambda l, j: (l, 0, j))],
        out_specs=pl.BlockSpec((1, 16, tn), lambda l, j: (l, 0, j)),
        out_shape=jax.ShapeDtypeStruct((depth, 16, n), F32),
        compiler_params=_params("arbitrary", "arbitrary"),
        name="ada_mods",
    )(c_rows, ada_w, ada_b.reshape(depth, 1, n))


def _norm_mod(x, g, ml, mc, is_ctx, k):
    ms = jnp.mean(x * x, axis=-1, keepdims=True)
    xn = x * lax.rsqrt(ms + EPS) * g
    m = jnp.where(is_ctx, mc, ml)
    return xn * (1.0 + m[k + 1:k + 2]) + m[k:k + 1]


def _inproj_kernel(x_ref, g_ref, ml_ref, mc_ref, w_ref, *o_refs, n_lat_tiles, n_tmaj):
    i = pl.program_id(1)
    h = _norm_mod(x_ref[0], g_ref[...], ml_ref[0], mc_ref[0], i >= n_lat_tiles, 0)
    r = _dot(h.astype(BF16), w_ref[...])
    if n_tmaj:
        o_refs[0][...] = r[:, :n_tmaj]
        o_refs[1][0] = r[:, n_tmaj:]
    else:
        o_refs[0][0] = r


def in_proj(x, g, mods, w, n_lat, n_tmaj):
    b, t, d = x.shape
    n = w.shape[1]
    tm = TOKEN_TILE
    if n_tmaj:
        out_shape = [jax.ShapeDtypeStruct((t, b * n_tmaj), F32), jax.ShapeDtypeStruct((b, t, n - n_tmaj), F32)]
        out_specs = [pl.BlockSpec((tm, n_tmaj), lambda bb, i: (i, bb)),
                     pl.BlockSpec((1, tm, n - n_tmaj), lambda bb, i: (bb, i, 0))]
    else:
        out_shape = [jax.ShapeDtypeStruct((b, t, n), F32)]
        out_specs = [pl.BlockSpec((1, tm, n), lambda bb, i: (bb, i, 0))]
    return pl.pallas_call(
        functools.partial(_inproj_kernel, n_lat_tiles=n_lat // tm, n_tmaj=n_tmaj),
        grid=(b, t // tm),
        in_specs=[pl.BlockSpec((1, tm, d), lambda bb, i: (bb, i, 0)),
                  pl.BlockSpec((1, d), lambda bb, i: (0, 0)),
                  pl.BlockSpec((1, 6, d), lambda bb, i: (bb, 0, 0)),
                  pl.BlockSpec((1, 6, d), lambda bb, i: (b, 0, 0)),
                  pl.BlockSpec((d, n), lambda bb, i: (0, 0), pipeline_mode=pl.Buffered(1))],
        out_specs=out_specs,
        out_shape=out_shape,
        compiler_params=_params("arbitrary", "arbitrary"),
        name="in_proj",
    )(x, g.reshape(1, d), mods, mods, w)


MLP_CHUNK = 1024


def _mix_mlp_kernel(x_ref, a_ref, b_ref, wa_ref, wb_ref, g_ref, ml_ref, mc_ref, w1_ref, w2_ref, fg_ref,
                    o_ref, acc_ref, *, n_lat_tiles, a_tmaj, final):
    i = pl.program_id(1)
    is_ctx = i >= n_lat_tiles
    m = jnp.where(is_ctx, mc_ref[0], ml_ref[0])
    a = a_ref[...] if a_tmaj else a_ref[0]
    mix = _dot(a.astype(BF16), wa_ref[...]) + _dot(b_ref[0].astype(BF16), wb_ref[...])
    x = x_ref[0] + m[2:3] * mix
    h = _norm_mod(x, g_ref[...], ml_ref[0], mc_ref[0], is_ctx, 3).astype(BF16)
    for k in range(w1_ref.shape[1] // MLP_CHUNK):
        ks = slice(k * MLP_CHUNK, (k + 1) * MLP_CHUNK)
        hid = jnp.maximum(_dot(h, w1_ref[:, ks]), 0.0)
        part = _dot((hid * hid).astype(BF16), w2_ref[ks, :])
        if k == 0:
            acc_ref[...] = part
        else:
            acc_ref[...] += part
    y = x + m[5:6] * acc_ref[...]
    if final:
        ms = jnp.mean(y * y, axis=-1, keepdims=True)
        y = y * lax.rsqrt(ms + EPS) * fg_ref[...]
    o_ref[0] = y


def mix_mlp(x, a, bm, w_out, g, mods, w1, w2, final_g, n_lat, n_rows, a_tmaj, final):
    b, t, d = x.shape
    wd = bm.shape[-1]
    hid = w1.shape[1]
    tm = TOKEN_TILE
    once = pl.Buffered(1)
    if a_tmaj:
        a_spec = pl.BlockSpec((tm, wd), lambda bb, i: (i, bb))
    else:
        a_spec = pl.BlockSpec((1, tm, wd), lambda bb, i: (bb, i, 0))
    return pl.pallas_call(
        functools.partial(_mix_mlp_kernel, n_lat_tiles=n_lat // tm, a_tmaj=a_tmaj, final=final),
        grid=(b, n_rows // tm),
        in_specs=[pl.BlockSpec((1, tm, d), lambda bb, i: (bb, i, 0)),
                  a_spec,
                  pl.BlockSpec((1, tm, wd), lambda bb, i: (bb, i, 0)),
                  pl.BlockSpec((wd, d), lambda bb, i: (0, 0), pipeline_mode=once),
                  pl.BlockSpec((wd, d), lambda bb, i: (1, 0), pipeline_mode=once),
                  pl.BlockSpec((1, d), lambda bb, i: (0, 0)),
                  pl.BlockSpec((1, 6, d), lambda bb, i: (bb, 0, 0)),
                  pl.BlockSpec((1, 6, d), lambda bb, i: (b, 0, 0)),
                  pl.BlockSpec((d, hid), lambda bb, i: (0, 0), pipeline_mode=once),
                  pl.BlockSpec((hid, d), lambda bb, i: (0, 0), pipeline_mode=once),
                  pl.BlockSpec((1, d), lambda bb, i: (0, 0))],
        out_specs=pl.BlockSpec((1, tm, d), lambda bb, i: (bb, i, 0)),
        out_shape=jax.ShapeDtypeStruct((b, n_rows, d), F32),
        scratch_shapes=[pltpu.VMEM((tm, d), F32)],
        compiler_params=_params("arbitrary", "arbitrary"),
        name="mix_mlp",
    )(x, a, bm, w_out, w_out, g.reshape(1, d), mods, mods, w1, w2, final_g.reshape(1, d))


def _conv3(x, w, first, last):
    rows = x.shape[0]
    xm = jnp.where(first, 0.0, pltpu.roll(x, 1, 0))
    xp = jnp.where(last, 0.0, pltpu.roll(x, rows - 1, 0))
    return xm * w[0:1] + x * w[1:2] + xp * w[2:3]


def _shortconv_kernel(x_ref, w_ref, o_ref, *, n_lat):
    x = x_ref[0]
    t = x.shape[0]
    row = lax.broadcasted_iota(jnp.int32, (t, 1), 0)
    first = (row == 0) | (row == n_lat)
    last = (row == n_lat - 1) | (row == t - 1)
    y = _conv3(x, w_ref[...], first, last)
    o_ref[0, 0, 0] = y
    o_ref[0, 1, 0] = y


def short_conv(p, w, n_lat):
    b, t, _ = p.shape
    c = w.shape[1]
    cb = GROUP_W
    return pl.pallas_call(
        functools.partial(_shortconv_kernel, n_lat=n_lat),
        grid=(b, c // cb),
        in_specs=[pl.BlockSpec((1, t, cb), lambda bb, j: (bb, 0, j)),
                  pl.BlockSpec((SHORT_K, cb), lambda bb, j: (0, j))],
        out_specs=pl.BlockSpec((1, 2, 1, t, cb), lambda bb, j: (j, 0, bb, 0, 0)),
        out_shape=jax.ShapeDtypeStruct((c // cb, 2, b, t, cb), F32),
        compiler_params=_params("arbitrary", "arbitrary"),
        name="short_conv",
    )(p, w)


S5_KB = LANE
S5_NB = GROUP_W // S5_KB
S5_SB = S5_W // S5_NB


def _s5_kernel(*refs, tc, nb, reverse, finish):
    if finish:
        (u_ref, wbr_ref, wbi_ref, lr_ref, li_ref, wcr_ref, wci_ref, yf_ref, d_ref, gw_ref, gb_ref,
         o_ref, hr_s, hi_s, sr_s, si_s) = refs
    else:
        (u_ref, wbr_ref, wbi_ref, lr_ref, li_ref, wcr_ref, wci_ref,
         o_ref, hr_s, hi_s, sr_s, si_s) = refs

    @pl.when(pl.program_id(0) == 0)
    def _():
        sr_s[...] = jnp.zeros_like(sr_s)
        si_s[...] = jnp.zeros_like(si_s)

    u = u_ref[...]
    ub = u.astype(BF16)
    for j in range(S5_NB):
        uj = ub[:, j * S5_KB:(j + 1) * S5_KB]
        hr_s[:, j * S5_SB:(j + 1) * S5_SB] = _dot(uj, wbr_ref[j])
        hi_s[:, j * S5_SB:(j + 1) * S5_SB] = _dot(uj, wbi_ref[j])

    for s in range(S5_NB):
        sl = slice(s * S5_SB, (s + 1) * S5_SB)
        lr = jnp.broadcast_to(lr_ref[:, sl], (nb, S5_SB))
        li = jnp.broadcast_to(li_ref[:, sl], (nb, S5_SB))

        def body(k, carry, sl=sl, lr=lr, li=li):
            hr, hi = carry
            t = (tc - 1 - k) if reverse else k
            r0 = pl.multiple_of(t * nb, nb)
            nr = lr * hr - li * hi + hr_s[pl.ds(r0, nb), sl]
            ni = lr * hi + li * hr + hi_s[pl.ds(r0, nb), sl]
            hr_s[pl.ds(r0, nb), sl] = nr
            hi_s[pl.ds(r0, nb), sl] = ni
            return nr, ni

        hr, hi = lax.fori_loop(0, tc, body, (sr_s[:, sl], si_s[:, sl]), unroll=4)
        sr_s[:, sl] = hr
        si_s[:, sl] = hi

    for j in range(S5_NB):
        sl = slice(j * S5_SB, (j + 1) * S5_SB)
        co = slice(j * S5_KB, (j + 1) * S5_KB)
        y = _dot(hr_s[:, sl].astype(BF16), wcr_ref[j]) + _dot(hi_s[:, sl].astype(BF16), wci_ref[j])
        if finish:
            o_ref[:, co] = y + yf_ref[:, co] + u[:, co] * d_ref[:, co]
        else:
            o_ref[:, co] = y

    if finish:
        y = o_ref[...]
        y = 0.5 * y * (1.0 + lax.erf(y * (2.0 ** -0.5)))
        z = _dot(y.astype(BF16), gw_ref[...]) + gb_ref[...]
        o_ref[...] = y * jax.nn.sigmoid(z)


def s5_scan_call(u2, tabs, n_lat_steps, n_steps, nb, reverse, extra):
    tc = 32
    rows = tc * nb
    n_chunks = n_steps // tc
    n_lat_chunks = n_lat_steps // tc
    n_ctx_chunks = n_chunks - n_lat_chunks
    if reverse:
        def cmap(c):
            return (n_chunks - 1 - c, 0)
    else:
        def cmap(c):
            return (jnp.where(c < n_ctx_chunks, n_lat_chunks + c, c - n_ctx_chunks), 0)
    wbr, wbi, lr, li, wcr, wci = tabs
    full3 = lambda a: pl.BlockSpec(a.shape, lambda c: (0, 0, 0))
    full2 = lambda a: pl.BlockSpec(a.shape, lambda c: (0, 0))
    in_specs = [pl.BlockSpec((rows, GROUP_W), cmap), full3(wbr), full3(wbi), full2(lr), full2(li),
                full3(wcr), full3(wci)]
    args = [u2, wbr, wbi, lr, li, wcr, wci]
    finish = extra is not None
    if finish:
        yf, dsk, gw, gb = extra
        in_specs += [pl.BlockSpec((rows, GROUP_W), cmap), full2(dsk), full2(gw), full2(gb)]
        args += [yf, dsk, gw, gb]
    return pl.pallas_call(
        functools.partial(_s5_kernel, tc=tc, nb=nb, reverse=reverse, finish=finish),
        grid=(n_chunks,),
        in_specs=in_specs,
        out_specs=pl.BlockSpec((rows, GROUP_W), cmap),
        out_shape=jax.ShapeDtypeStruct(u2.shape, F32),
        scratch_shapes=[pltpu.VMEM((rows, S5_W), F32), pltpu.VMEM((rows, S5_W), F32),
                        pltpu.VMEM((nb, S5_W), F32), pltpu.VMEM((nb, S5_W), F32)],
        compiler_params=_params("arbitrary"),
        name="s5_bwd_finish" if finish else "s5_fwd",
    )(*args)


def _block_diag(x):
    nblk, g, r, c = x.shape
    eye = jnp.eye(g, dtype=x.dtype)
    return jnp.einsum('jgrc,gh->jgrhc', x, eye).reshape(nblk, g * r, g * c)


def s5_tables(lam_re, lam_im, log_dt, b_re, b_im, c_re, c_im):
    lam_re = jnp.minimum(lam_re.astype(F32), -1e-4)
    lam_im = lam_im.astype(F32)
    dt = jnp.exp(log_dt.astype(F32))[:, None]
    mag = jnp.exp(lam_re * dt)
    lb_re = mag * jnp.cos(lam_im * dt)
    lb_im = mag * jnp.sin(lam_im * dt)
    den = lam_re * lam_re + lam_im * lam_im
    f_re = ((lb_re - 1.0) * lam_re + lb_im * lam_im) / den
    f_im = (lb_im * lam_re - (lb_re - 1.0) * lam_im) / den
    b_re = b_re.astype(F32)
    b_im = b_im.astype(F32)
    bb_re = f_re[..., None] * b_re - f_im[..., None] * b_im
    bb_im = f_re[..., None] * b_im + f_im[..., None] * b_re
    gpb = S5_KB // S5_GROUP
    to_b = lambda x: _block_diag(
        jnp.swapaxes(x, 1, 2).reshape(S5_NB, gpb, S5_GROUP, S5_STATE)).astype(BF16)
    to_c = lambda x: _block_diag(
        jnp.swapaxes(x.astype(F32), 1, 2).reshape(S5_NB, gpb, S5_STATE, S5_GROUP)).astype(BF16)
    return (to_b(bb_re), to_b(bb_im), lb_re.reshape(1, S5_W), lb_im.reshape(1, S5_W),
            to_c(c_re), to_c(-c_im.astype(F32)))


def dft_tables(n):
    nn = 2 * n
    f = jnp.arange(n, dtype=jnp.int32)[:, None]
    t = jnp.arange(n, dtype=jnp.int32)[None, :]
    q = 1 << (int(math.log2(n)) // 2)

    def factor(tt):
        ang = ((f * tt[None, :]) % nn).astype(F32) * (2.0 * math.pi / nn)
        return jnp.cos(ang), jnp.sin(ang)

    c1, s1 = factor(jnp.arange(n // q, dtype=jnp.int32) * q)
    c0, s0 = factor(jnp.arange(q, dtype=jnp.int32))
    cos = (c1[:, :, None] * c0[:, None, :] - s1[:, :, None] * s0[:, None, :]).reshape(n, n)
    sin = (s1[:, :, None] * c0[:, None, :] + c1[:, :, None] * s0[:, None, :]).reshape(n, n)
    nyq = jnp.where(t % 2 == 0, 1.0, -1.0).astype(F32)
    msin = jnp.where(f == 0, nyq, -sin)
    fwd = jnp.concatenate([cos, msin], axis=0)
    wf = jnp.where(jnp.arange(n) == 0, 1.0 / nn, 2.0 / nn).astype(F32)
    return fwd, wf


def _filtdft_kernel(fh_ref, fl_ref, hh_ref, hl_ref, o_ref):
    fh = fh_ref[...]
    o_ref[...] = _dot(fh, hh_ref[...]) + _dot(fh, hl_ref[...]) + _dot(fl_ref[...], hh_ref[...])


def filter_dft(fwd_hi, fwd_lo, h):
    n2, n = fwd_hi.shape
    c = h.shape[1]
    tr = min(256, n2)
    tcn = 512
    hh, hl = _split_bf16(h)
    return pl.pallas_call(
        _filtdft_kernel,
        grid=(n2 // tr, c // tcn),
        in_specs=[pl.BlockSpec((tr, n), lambda i, j: (i, 0)),
                  pl.BlockSpec((tr, n), lambda i, j: (i, 0)),
                  pl.BlockSpec((n, tcn), lambda i, j: (0, j)),
                  pl.BlockSpec((n, tcn), lambda i, j: (0, j))],
        out_specs=pl.BlockSpec((tr, tcn), lambda i, j: (i, j)),
        out_shape=jax.ShapeDtypeStruct((n2, c), F32),
        compiler_params=_params("arbitrary", "arbitrary"),
        name="filter_dft",
    )(fwd_hi, fwd_lo, hh, hl)


def hyena_filter_response(n, fwd_hi, fwd_lo, wf, f_w1, f_b1, f_w2, f_b2, f_w3, f_freq, log_decay):
    t = jnp.linspace(0.0, 1.0, n, dtype=F32)[:, None]
    w = 2.0 * math.pi * jnp.arange(n, dtype=F32)[:, None] / n
    bands = jnp.linspace(1e-4, HY_BANDS - 1, HY_BANDS, dtype=F32)[None, :]
    z = jnp.concatenate([t, jnp.cos(bands * w), -jnp.sin(bands * w)], axis=-1)
    freq = f_freq.astype(F32)
    hid = jnp.sin(freq[0] * (jnp.dot(z, f_w1.astype(F32), precision=HIGHEST) + f_b1.astype(F32)))
    hid = jnp.sin(freq[1] * (jnp.dot(hid, f_w2.astype(F32), precision=HIGHEST) + f_b2.astype(F32)))
    h = jnp.dot(hid, f_w3.astype(F32), precision=HIGHEST).reshape(n, 2, HY_ORDER, GROUP_W)
    h = h * jnp.exp(-t[:, :, None, None] * jnp.exp(log_decay.astype(F32)))
    cw = HY_ORDER * GROUP_W
    h_fwd = h[:, 0].reshape(n, cw)
    h_bwd = h[:, 1].reshape(n, cw).at[0].set(0.0)
    resp = filter_dft(fwd_hi, fwd_lo, jnp.concatenate([h_fwd, h_bwd], axis=1))
    a_re, a_im = resp[:n, :cw], resp[n:, :cw]
    b_re, b_im = resp[:n, cw:], resp[n:, cw:]
    k_re = a_re + b_re
    first = (jnp.arange(n) == 0)[:, None]
    k_im = jnp.where(first, a_im + b_im, a_im - b_im)
    k_re = (k_re * wf[:, None]).reshape(n, HY_ORDER, GROUP_W).transpose(1, 0, 2)
    k_im = (k_im * wf[:, None]).reshape(n, HY_ORDER, GROUP_W).transpose(1, 0, 2)
    return k_re, k_im


def _hyena_kernel(z0_ref, x1_ref, x2_ref, cw_ref, cb_ref, fw_ref, inv_ref, kr_ref, ki_ref, bias_ref, *rest):
    o_ref, zb_s, zf_s, acc_s = rest[-4:]
    o = pl.program_id(1)
    f = pl.program_id(2)
    nf = pl.num_programs(2)
    c = GROUP_W

    def stream(ref, s):
        x = ref[0]
        row = lax.broadcasted_iota(jnp.int32, (x.shape[0], 1), 0)
        cols = slice(s * c, (s + 1) * c)
        return _conv3(x, cw_ref[:, cols], row == 0, row == x.shape[0] - 1) + cb_ref[:, cols]

    @pl.when((o == 0) & (f == 0))
    def _():
        z0 = stream(z0_ref, 0)
        zf_s[...] = z0
        zb_s[...] = z0.astype(BF16)

    @pl.when(f == 0)
    def _():
        acc_s[...] = jnp.zeros_like(acc_s)

    xf = _dot(fw_ref[0], zb_s[...])
    fb = xf.shape[0] // 2
    xr = xf[:fb]
    xi = xf[fb:]
    kr = kr_ref[0]
    ki = ki_ref[0]
    row = lax.broadcasted_iota(jnp.int32, (fb, 1), 0)
    packed = (row == 0) & (f == 0)
    yr = jnp.where(packed, xr * kr, xr * kr - xi * ki)
    yi = jnp.where(packed, xi * ki, xr * ki + xi * kr)
    yf = jnp.concatenate([yr, yi], axis=0).astype(BF16)
    acc_s[...] += _dot(inv_ref[0], yf)

    @pl.when(f == nf - 1)
    def _():
        bias = bias_ref[...]

        @pl.when(o == 0)
        def _():
            z1 = stream(x1_ref, 1) * (acc_s[...] + zf_s[...] * bias[0:1])
            zf_s[...] = z1
            zb_s[...] = z1.astype(BF16)

        @pl.when(o == 1)
        def _():
            o_ref[0] = stream(x2_ref, 2) * (acc_s[...] + zf_s[...] * bias[1:2])


HY_FREQ_BLOCK = 256


def dft_blocks(fwd_hi):
    n = fwd_hi.shape[1]
    fb = min(HY_FREQ_BLOCK, n)
    blocks = fwd_hi.reshape(2, n // fb, fb, n).transpose(1, 0, 2, 3).reshape(n // fb, 2 * fb, n)
    return blocks, blocks.transpose(0, 2, 1)


def hyena_call(p, n, row_blk, conv_w, conv_b, fwd_blk, inv_blk, k_re, k_im, bias, out_prev):
    b, t_total, _ = p.shape
    c = GROUP_W
    nf, fb2, _ = fwd_blk.shape
    fb = fb2 // 2
    once = pl.Buffered(1)
    in_specs = [pl.BlockSpec((1, n, c), lambda bb, o, f: (bb, row_blk, 0), pipeline_mode=once),
                pl.BlockSpec((1, n, c), lambda bb, o, f: (bb, row_blk, 1), pipeline_mode=once),
                pl.BlockSpec((1, n, c), lambda bb, o, f: (bb, row_blk, 2), pipeline_mode=once),
                pl.BlockSpec((SHORT_K, 3 * c), lambda bb, o, f: (0, 0)),
                pl.BlockSpec((1, 3 * c), lambda bb, o, f: (0, 0)),
                pl.BlockSpec((1, fb2, n), lambda bb, o, f: (f, 0, 0)),
                pl.BlockSpec((1, n, fb2), lambda bb, o, f: (f, 0, 0)),
                pl.BlockSpec((1, fb, c), lambda bb, o, f: (o, f, 0)),
                pl.BlockSpec((1, fb, c), lambda bb, o, f: (o, f, 0)),
                pl.BlockSpec((HY_ORDER, c), lambda bb, o, f: (0, 0))]
    args = [p, p, p, conv_w, conv_b.reshape(1, 3 * c), fwd_blk, inv_blk, k_re, k_im, bias]
    aliases = {}
    if out_prev is not None:
        in_specs.append(pl.BlockSpec(memory_space=pl.ANY))
        args.append(out_prev)
        aliases = {len(args) - 1: 0}
    return pl.pallas_call(
        _hyena_kernel,
        grid=(b, HY_ORDER, nf),
        in_specs=in_specs,
        out_specs=pl.BlockSpec((1, n, c), lambda bb, o, f: (bb, row_blk, 0)),
        out_shape=jax.ShapeDtypeStruct((b, t_total, c), F32),
        scratch_shapes=[pltpu.VMEM((n, c), BF16), pltpu.VMEM((n, c), F32), pltpu.VMEM((n, c), F32)],
        input_output_aliases=aliases,
        compiler_params=_params("arbitrary", "arbitrary", "arbitrary"),
        name="hyena",
    )(*args)


def _rwkv_lora_kernel(p_ref, ww_ref, wa_ref, wg_ref, w0_ref, a0_ref, wp_ref, ap_ref, g_ref):
    x = p_ref[0]
    wa_in = x[:, :LANE]
    th = jnp.tanh(wa_in).astype(BF16)
    lin = wa_in.astype(BF16)
    w0 = w0_ref[...]
    a0 = a0_ref[...]
    for d in range(2):
        wp_ref[d, 0] = w0[d:d + 1] + _dot(th, ww_ref[d])
        ap_ref[d, 0] = a0[d:d + 1] + _dot(lin, wa_ref[d])
    g_ref[0] = _dot(jax.nn.sigmoid(x[:, LANE:]).astype(BF16), wg_ref[...])


def rwkv_lora(p, w_up, a_up, g_up, w0, a0):
    b, t, _ = p.shape
    c = GROUP_W
    tm = TOKEN_TILE
    zeros = jnp.zeros((2, 64, c), F32)
    ww = jnp.concatenate([w_up.astype(F32), zeros], axis=1).astype(BF16)
    wa = jnp.concatenate([zeros, a_up.astype(F32)], axis=1).astype(BF16)
    both = jax.ShapeDtypeStruct((2, b, t, c), F32)
    both_spec = pl.BlockSpec((2, 1, tm, c), lambda bb, i: (0, bb, i, 0))
    return pl.pallas_call(
        _rwkv_lora_kernel,
        grid=(b, t // tm),
        in_specs=[pl.BlockSpec((1, tm, RW_LORA_W), lambda bb, i: (bb, i, 3 * c // RW_LORA_W)),
                  pl.BlockSpec((2, LANE, c), lambda bb, i: (0, 0, 0)),
                  pl.BlockSpec((2, LANE, c), lambda bb, i: (0, 0, 0)),
                  pl.BlockSpec((LANE, c), lambda bb, i: (0, 0)),
                  pl.BlockSpec((2, c), lambda bb, i: (0, 0)),
                  pl.BlockSpec((2, c), lambda bb, i: (0, 0))],
        out_specs=[both_spec, both_spec, pl.BlockSpec((1, tm, c), lambda bb, i: (bb, i, 0))],
        out_shape=[both, both, jax.ShapeDtypeStruct((b, t, c), F32)],
        compiler_params=_params("arbitrary", "arbitrary"),
        name="rwkv_lora",
    )(p, ww, wa, g_up.astype(BF16), w0, a0)


def _rwkv_scan_kernel(wpf_ref, wpb_ref, apf_ref, apb_ref, kf_ref, kb_ref, vf_ref, vb_ref, rf_ref, rb_ref,
                      kkc_ref, kac_ref, yf_ref, yb_ref, s_ref, w_s, kk_s, b_s, ke_s, r_s, *, tb):
    n = RW_HEAD
    is_fwd = lax.broadcasted_iota(jnp.int32, (1, LANE), 1) < LANE // 2

    @pl.when(pl.program_id(0) == 0)
    def _():
        s_ref[...] = jnp.zeros_like(s_ref)

    def both(f_ref, b_ref, t):
        return jnp.where(is_fwd, f_ref[t], b_ref[tb - 1 - t])

    def prepare(t, slot):
        k = both(kf_ref, kb_ref, t)
        a = jax.nn.sigmoid(both(apf_ref, apb_ref, t))
        kk = k * kkc_ref[...]
        nrm = jnp.sqrt(jnp.sum(kk * kk, axis=0, keepdims=True))
        kk = kk / jnp.maximum(nrm, 1e-12)
        wl = -jax.nn.softplus(-both(wpf_ref, wpb_ref, t)) - 0.5
        w_s[slot] = jnp.exp(-jnp.exp(wl))
        kk_s[slot] = kk
        b_s[slot] = kk * a
        ke_s[slot] = k * (1.0 + (a - 1.0) * kac_ref[...])
        r_s[slot] = both(rf_ref, rb_ref, t)

    prepare(0, 0)
    sa0 = jnp.zeros((n, LANE), F32)
    for j in range(n):
        sa0 = sa0 + s_ref[j] * kk_s[0, j:j + 1, :]

    def step(t, sa):
        slot = t % 2
        nxt = 1 - slot
        prepare(jnp.minimum(t + 1, tb - 1), nxt)
        v = both(vf_ref, vb_ref, t)
        y = jnp.zeros((n, LANE), F32)
        sa_next = jnp.zeros((n, LANE), F32)
        for j in range(n):
            row = pl.ds(j, 1)
            s = s_ref[j] * w_s[slot, row, :] - sa * b_s[slot, row, :] + v * ke_s[slot, row, :]
            s_ref[j] = s
            y = y + s * r_s[slot, row, :]
            sa_next = sa_next + s * kk_s[nxt, row, :]
        yf_ref[t] = y
        yb_ref[tb - 1 - t] = y
        return sa_next

    lax.fori_loop(0, tb, step, sa0)


def rwkv_scan(wp, ap, k, v, r, kkc, kac, n_lat):
    t = k.shape[0]
    tb = 32
    n = RW_HEAD
    n_blk = t // tb
    n_lat_blk = n_lat // tb
    n_ctx_blk = n_blk - n_lat_blk

    def fblk(c):
        return jnp.where(c < n_ctx_blk, n_lat_blk + c, c - n_ctx_blk)

    def rblk(c):
        return n_blk - 1 - c

    fwd = pl.BlockSpec((tb, n, LANE), lambda c: (fblk(c), 0, 0))
    bwd = pl.BlockSpec((tb, n, LANE), lambda c: (rblk(c), 0, 0))
    cst = pl.BlockSpec((n, LANE), lambda c: (0, 0))
    buf = pltpu.VMEM((2, n, LANE), F32)
    out = jax.ShapeDtypeStruct((t, n, LANE), F32)
    return pl.pallas_call(
        functools.partial(_rwkv_scan_kernel, tb=tb),
        grid=(n_blk,),
        in_specs=[fwd, bwd] * 5 + [cst] * 2,
        out_specs=[fwd, bwd],
        out_shape=[out, out],
        scratch_shapes=[pltpu.VMEM((n, n, LANE), F32), buf, buf, buf, buf, buf],
        compiler_params=_params("arbitrary"),
        name="rwkv_scan",
    )(wp, wp, ap, ap, k, k, v, v, r, r, kkc, kac)


def _rwkv_post_kernel(y_ref, r_ref, k_ref, v_ref, ap_ref, g_ref, m_ref, ka_ref, rk_ref, lg_ref, lb_ref,
                      o_ref):
    m = m_ref[...]

    def head_mean(x):
        hi, lo = _split_bf16(x)
        return _dot(hi, m) + _dot(lo, m)

    y = y_ref[0]
    d = y - head_mean(y)
    var = head_mean(d * d)
    yn = d * lax.rsqrt(var + RW_LN_EPS) * lg_ref[...] + lb_ref[...]
    a_sum = jax.nn.sigmoid(ap_ref[0, 0]) + jax.nn.sigmoid(ap_ref[1, 0])
    k_sum = k_ref[0, 0, 0] * (2.0 + (a_sum - 2.0) * ka_ref[...])
    bonus = head_mean(r_ref[0, 0, 0] * k_sum * rk_ref[...]) * float(RW_HEAD) * v_ref[0, 0, 0]
    o_ref[0] = (yn + bonus) * g_ref[0]


def rwkv_post(y, rkv, ap, gate, k_a, r_k, ln_g, ln_b, n_rows):
    b, t, c = y.shape
    tm = TOKEN_TILE
    hm = jnp.kron(jnp.eye(RW_HEADS, dtype=F32), jnp.full((RW_HEAD, RW_HEAD), 1.0 / RW_HEAD, F32)).astype(BF16)
    tok = pl.BlockSpec((1, tm, c), lambda bb, i: (bb, i, 0))
    vec = pl.BlockSpec((1, c), lambda bb, i: (0, 0))
    rkv_spec = lambda q: pl.BlockSpec((1, 1, 1, tm, c), lambda bb, i: (q, 0, bb, i, 0))
    return pl.pallas_call(
        _rwkv_post_kernel,
        grid=(b, n_rows // tm),
        in_specs=[tok, rkv_spec(0), rkv_spec(1), rkv_spec(2),
                  pl.BlockSpec((2, 1, tm, c), lambda bb, i: (0, bb, i, 0)), tok,
                  pl.BlockSpec((c, c), lambda bb, i: (0, 0)), vec, vec, vec, vec],
        out_specs=tok,
        out_shape=jax.ShapeDtypeStruct((b, t, c), F32),
        compiler_params=_params("arbitrary", "arbitrary"),
        name="rwkv_post",
    )(y, rkv, rkv, rkv, ap, gate, hm, k_a.reshape(1, c), r_k.reshape(1, c),
      ln_g.reshape(1, c), ln_b.reshape(1, c))


def _to_scan_layout(a):
    d, b, t, _ = a.shape
    x = a.reshape(d * b, t, RW_HEADS, RW_HEAD).transpose(1, 3, 0, 2)
    return x.reshape(t, RW_HEAD, d * b * RW_HEADS)


def _from_scan_layout(yf, yb, b):
    t = yf.shape[0]
    half = b * RW_HEADS
    y = yf[..., :half] + yb[..., half:]
    return y.reshape(t, RW_HEAD, b, RW_HEADS).transpose(2, 0, 3, 1).reshape(b, t, GROUP_W)


def _chain_const(x, b):
    return jnp.tile(x.astype(F32).reshape(RW_HEADS, RW_HEAD).T, (1, 2 * b))


def _rope(x, cos, sin):
    lane = lax.broadcasted_iota(jnp.int32, (1, LANE), 1)
    first = (lane % (2 * ROPE_FREQS)) < ROPE_FREQS
    partner = jnp.where(first, pltpu.roll(x, LANE - ROPE_FREQS, 1), pltpu.roll(x, ROPE_FREQS, 1))
    return x * cos + partner * sin


def _attn_tile(q, k, v, lam):
    lane = lax.broadcasted_iota(jnp.int32, (1, LANE), 1)
    m0 = lane < DA_HEAD
    q0 = jnp.where(m0, q, 0.0).astype(BF16)
    q1 = jnp.where(m0, 0.0, q).astype(BF16)
    dn = (((1,), (1,)), ((), ()))
    s0 = lax.dot_general(q0, k, dn, preferred_element_type=F32)
    s1 = lax.dot_general(q1, k, dn, preferred_element_type=F32)
    p0 = jnp.exp2(s0 - jnp.max(s0, axis=-1, keepdims=True))
    p1 = jnp.exp2(s1 - jnp.max(s1, axis=-1, keepdims=True))
    c0 = 1.0 / jnp.sum(p0, axis=-1, keepdims=True)
    c1 = lam / jnp.sum(p1, axis=-1, keepdims=True)
    return _dot((p0 * c0 - p1 * c1).astype(BF16), v)


def _attn_kernel(q_ref, k_ref, v_ref, cq_ref, sq_ref, ck_ref, sk_ref, lp_ref, g_ref, o_ref,
                 kr_s, vb_s, *, n_lat, n_lat_tiles, lam_init):
    i = pl.program_id(2)

    @pl.when(i == 0)
    def _():
        kr_s[...] = _rope(k_ref[0], ck_ref[...], sk_ref[...]).astype(BF16)
        vb_s[...] = v_ref[0].astype(BF16)

    lp = lp_ref[...]
    lam = (jnp.exp(jnp.sum(lp[0:1] * lp[1:2], axis=-1, keepdims=True))
           - jnp.exp(jnp.sum(lp[2:3] * lp[3:4], axis=-1, keepdims=True)) + lam_init)
    q = _rope(q_ref[0], cq_ref[...], sq_ref[...]) * (DA_SCALE * math.log2(math.e))

    def finish(o):
        on = o * lax.rsqrt(jnp.mean(o * o, axis=-1, keepdims=True) + DA_SUBLN_EPS)
        o_ref[0] = on * g_ref[...] * (1.0 - lam_init)

    @pl.when(i < n_lat_tiles)
    def _():
        finish(_attn_tile(q, kr_s[...], vb_s[...], lam))

    @pl.when(i >= n_lat_tiles)
    def _():
        finish(_attn_tile(q, kr_s[n_lat:], vb_s[n_lat:], lam))


def diff_attention(p, col0, cosf, sins, lam_p, subln_g, lam_init, n_lat, n_rows):
    b, t, _ = p.shape
    tq = TOKEN_TILE
    off = col0 // LANE
    hq = DA_HEADS
    return pl.pallas_call(
        functools.partial(_attn_kernel, n_lat=n_lat, n_lat_tiles=n_lat // tq, lam_init=lam_init),
        grid=(b, hq, n_rows // tq),
        in_specs=[pl.BlockSpec((1, tq, LANE), lambda bb, h, i: (bb, i, off + h)),
                  pl.BlockSpec((1, t, LANE), lambda bb, h, i: (bb, 0, off + hq + h)),
                  pl.BlockSpec((1, t, LANE), lambda bb, h, i: (bb, 0, off + 2 * hq + h)),
                  pl.BlockSpec((tq, LANE), lambda bb, h, i: (i, 0)),
                  pl.BlockSpec((tq, LANE), lambda bb, h, i: (i, 0)),
                  pl.BlockSpec((t, LANE), lambda bb, h, i: (0, 0)),
                  pl.BlockSpec((t, LANE), lambda bb, h, i: (0, 0)),
                  pl.BlockSpec((4, DA_HEAD), lambda bb, h, i: (0, 0)),
                  pl.BlockSpec((1, LANE), lambda bb, h, i: (0, 0))],
        out_specs=pl.BlockSpec((1, tq, LANE), lambda bb, h, i: (bb, i, h)),
        out_shape=jax.ShapeDtypeStruct((b, t, hq * DA_V), F32),
        scratch_shapes=[pltpu.VMEM((t, LANE), BF16), pltpu.VMEM((t, LANE), BF16)],
        compiler_params=_params("arbitrary", "arbitrary", "arbitrary"),
        name="diff_attention",
    )(p, p, p, cosf, sins, cosf, sins, lam_p, subln_g.reshape(1, DA_V))


def rope_tables(n_lat, n_ctx):
    rows = n_lat // GRID_W
    row = jnp.repeat(jnp.arange(rows, dtype=F32), GRID_W)
    col = jnp.tile(jnp.arange(GRID_W, dtype=F32), rows)
    inv = ROPE_BASE ** (-jnp.arange(ROPE_FREQS, dtype=F32) / ROPE_FREQS)
    ang = jnp.stack([row[:, None] * inv, col[:, None] * inv], axis=1)
    cos, sin = jnp.cos(ang), jnp.sin(ang)
    cosf = jnp.concatenate([cos, cos], axis=-1).reshape(n_lat, DA_HEAD)
    sins = jnp.concatenate([-sin, sin], axis=-1).reshape(n_lat, DA_HEAD)
    cosf = jnp.concatenate([jnp.tile(cosf, (1, 2)), jnp.ones((n_ctx, LANE), F32)], axis=0)
    sins = jnp.concatenate([jnp.tile(sins, (1, 2)), jnp.zeros((n_ctx, LANE), F32)], axis=0)
    return cosf, sins


def _even_mixers(xs, mods, keep_ctx, n_lat, norm1_g, w_in, s5p, hyp, dft):
    b, t, _ = xs.shape
    n_ctx = t - n_lat
    u_t, p_h = in_proj(xs, norm1_g, mods, w_in.astype(BF16), n_lat, GROUP_W)

    (lam_re, lam_im, log_dt, b_re, b_im, c_re, c_im, d_skip, glu_w, glu_b) = s5p
    u2 = u_t.reshape(t * b, GROUP_W)
    tabs = [s5_tables(lam_re[d], lam_im[d], log_dt[d], b_re[d], b_im[d], c_re[d], c_im[d]) for d in range(2)]
    y_f = s5_scan_call(u2, tabs[0], n_lat, t, b, False, None)
    a_t = s5_scan_call(u2, tabs[1], n_lat, t, b, True,
                       (y_f, d_skip.reshape(1, GROUP_W), glu_w.astype(BF16), glu_b.reshape(1, GROUP_W)))

    (conv_w, conv_b, f_w1, f_b1, f_w2, f_b2, f_w3, f_freq, log_decay, bias) = hyp
    fh_l, fl_l, (fw_l, inv_l), wf_l = dft[0]
    kr, ki = hyena_filter_response(n_lat, fh_l, fl_l, wf_l, f_w1, f_b1, f_w2, f_b2, f_w3, f_freq, log_decay)
    b_m = hyena_call(p_h, n_lat, 0, conv_w, conv_b, fw_l, inv_l, kr, ki, bias, None)
    if keep_ctx:
        fh_c, fl_c, (fw_c, inv_c), wf_c = dft[1]
        kr, ki = hyena_filter_response(n_ctx, fh_c, fl_c, wf_c, f_w1, f_b1, f_w2, f_b2, f_w3, f_freq,
                                       log_decay)
        b_m = hyena_call(p_h, n_ctx, n_lat // n_ctx, conv_w, conv_b, fw_c, inv_c, kr, ki, bias, b_m)
    return a_t.reshape(t, b * GROUP_W), b_m


def _odd_mixers(xs, mods, keep_ctx, n_lat, lam_init, norm1_g, w_in, rwp, dap, rope):
    b, t, _ = xs.shape
    n_rows = t if keep_ctx else n_lat
    (p,) = in_proj(xs, norm1_g, mods, w_in.astype(BF16), n_lat, 0)

    (conv_w, w0, w_up, a0, a_up, g_up, k_k, k_a, r_k, ln_g, ln_b) = rwp
    rkv = short_conv(p, conv_w, n_lat)
    wp, ap, gate = rwkv_lora(p, w_up, a_up, g_up, w0, a0)
    yf, yb = rwkv_scan(_to_scan_layout(wp), _to_scan_layout(ap), _to_scan_layout(rkv[1]),
                       _to_scan_layout(rkv[2]), _to_scan_layout(rkv[0]),
                       _chain_const(k_k, b), _chain_const(k_a, b), n_lat)
    a_m = rwkv_post(_from_scan_layout(yf, yb, b), rkv, ap, gate, k_a, r_k, ln_g, ln_b, n_rows)

    lam_p, subln_g = dap
    b_m = diff_attention(p, RW_IN, rope[0], rope[1], lam_p, subln_g, lam_init, n_lat, n_rows)
    return a_m, b_m


def kernel(x, c, ctx, c_ctx, ada_w, ada_b, norm1_g, norm2_g, mlp_w1, mlp_w2, final_g, ev_w_in, ev_w_out, s5_lam_re, s5_lam_im, s5_log_dt, s5_b_re, s5_b_im, s5_c_re, s5_c_im, s5_d, s5_glu_w, s5_glu_b, hy_conv_w, hy_conv_b, hy_f_w1, hy_f_b1, hy_f_w2, hy_f_b2, hy_f_w3, hy_f_freq, hy_log_decay, hy_bias, od_w_in, od_w_out, rw_conv_w, rw_w0, rw_w_up, rw_a0, rw_a_up, rw_g_up, rw_k_k, rw_k_a, rw_r_k, rw_ln_g, rw_ln_b, da_lam, da_subln_g):
    b, n_lat, d = x.shape
    n_ctx = ctx.shape[1]
    assert d == D_MODEL and b == SUBLANE
    assert n_lat % TOKEN_TILE == 0 and n_ctx % TOKEN_TILE == 0 and n_lat % n_ctx == 0
    xs = jnp.concatenate([x.astype(F32), ctx.astype(F32)], axis=1)

    c_rows = jnp.zeros((16, d), F32).at[:b].set(c.astype(F32)).at[b].set(c_ctx.astype(F32))
    depth = ada_w.shape[0]
    mods_all = ada_mods(c_rows, ada_w, ada_b).reshape(depth, 16, 6, d)

    rope = rope_tables(n_lat, n_ctx)
    dft = []
    for n in (n_lat, n_ctx):
        fwd, wf = dft_tables(n)
        hi, lo = _split_bf16(fwd)
        dft.append((hi, lo, dft_blocks(hi), wf))

    for l in range(depth):
        keep_ctx = l < depth - 1
        i = l // 2
        mods = mods_all[l]
        if l % 2 == 0:
            s5p = (s5_lam_re[i], s5_lam_im[i], s5_log_dt[i], s5_b_re[i], s5_b_im[i], s5_c_re[i], s5_c_im[i],
                   s5_d[i], s5_glu_w[i], s5_glu_b[i])
            hyp = (hy_conv_w[i], hy_conv_b[i], hy_f_w1[i], hy_f_b1[i], hy_f_w2[i], hy_f_b2[i], hy_f_w3[i],
                   hy_f_freq[i], hy_log_decay[i], hy_bias[i])
            a_m, b_m = _even_mixers(xs, mods, keep_ctx, n_lat, norm1_g[l], ev_w_in[i], s5p, hyp, dft)
            w_out = ev_w_out[i]
        else:
            rwp = (rw_conv_w[i], rw_w0[i], rw_w_up[i], rw_a0[i], rw_a_up[i], rw_g_up[i], rw_k_k[i],
                   rw_k_a[i], rw_r_k[i], rw_ln_g[i], rw_ln_b[i])
            lam_init = 0.8 - 0.6 * math.exp(-0.3 * l)
            a_m, b_m = _odd_mixers(xs, mods, keep_ctx, n_lat, lam_init, norm1_g[l], od_w_in[i], rwp,
                                   (da_lam[i], da_subln_g[i]), rope)
            w_out = od_w_out[i]
        n_rows = n_lat + n_ctx if keep_ctx else n_lat
        xs = mix_mlp(xs, a_m, b_m, w_out.astype(BF16), norm2_g[l], mods, mlp_w1[l].astype(BF16),
                     mlp_w2[l].astype(BF16), final_g, n_lat, n_rows, l % 2 == 0, l == depth - 1)
    return xs
```

```python
import functools
import math

import jax
import jax.numpy as jnp
from jax import lax
from jax.experimental import pallas as pl
from jax.experimental.pallas import tpu as pltpu

F32 = jnp.float32
BF16 = jnp.bfloat16
HIGHEST = lax.Precision.HIGHEST

D_MODEL = 1024
DEPTH = 4
GRID_W = 64
MLP_HIDDEN = 4 * D_MODEL
GROUP_W = D_MODEL // 2
EPS = 1e-6

S5_GROUP = 16
S5_GROUPS = GROUP_W // S5_GROUP
S5_STATE = 64
S5_W = S5_GROUPS * S5_STATE

HY_ORDER = 2
HY_EMB = 33
HY_BANDS = (HY_EMB - 1) // 2
SHORT_K = 3

RW_HEAD = 64
RW_HEADS = GROUP_W // RW_HEAD
RW_LORA_W = 256
RW_LN_EPS = 64e-5
RW_IN = 3 * GROUP_W + RW_LORA_W

DA_HEADS = 4
DA_HEAD = 64
DA_V = 2 * DA_HEAD
DA_SCALE = DA_HEAD ** -0.5
DA_SUBLN_EPS = 1e-5
ROPE_BASE = 10000.0
ROPE_FREQS = DA_HEAD // 4

LANE = 128
SUBLANE = 8
TOKEN_TILE = 256
VMEM_LIMIT = 48 * 1024 * 1024


def _params(*sem):
    return pltpu.CompilerParams(dimension_semantics=sem, vmem_limit_bytes=VMEM_LIMIT)


def _split_bf16(x):
    hi = x.astype(BF16)
    lo = (x - hi.astype(F32)).astype(BF16)
    return hi, lo


def _dot(a, b):
    return jnp.dot(a, b, preferred_element_type=F32)


def _ada_kernel(c_ref, w_ref, b_ref, o_ref):
    c = c_ref[...]
    s = c * jax.nn.sigmoid(c)
    o_ref[0] = jnp.dot(s, w_ref[0], preferred_element_type=F32, precision=HIGHEST) + b_ref[0]


def ada_mods(c_rows, ada_w, ada_b):
    depth, d, n = ada_w.shape
    tn = 1024
    return pl.pallas_call(
        _ada_kernel,
        grid=(depth, n // tn),
        in_specs=[pl.BlockSpec((16, d), lambda l, j: (0, 0)),
                  pl.BlockSpec((1, d, tn), lambda l, j: (l, 0, j)),
                  pl.BlockSpec((1, 1, tn), lambda l, j: (l, 0, j))],
        out_specs=pl.BlockSpec((1, 16, tn), lambda l, j: (l, 0, j)),
        out_shape=jax.ShapeDtypeStruct((depth, 16, n), F32),
        compiler_params=_params("arbitrary", "arbitrary"),
        name="ada_mods",
    )(c_rows, ada_w, ada_b.reshape(depth, 1, n))


def _norm_mod(x, g, ml, mc, is_ctx, k):
    ms = jnp.mean(x * x, axis=-1, keepdims=True)
    xn = x * lax.rsqrt(ms + EPS) * g
    m = jnp.where(is_ctx, mc, ml)
    return xn * (1.0 + m[k + 1:k + 2]) + m[k:k + 1]


def _inproj_kernel(x_ref, g_ref, ml_ref, mc_ref, w_ref, *o_refs, n_lat_tiles, n_tmaj):
    i = pl.program_id(1)
    h = _norm_mod(x_ref[0], g_ref[...], ml_ref[0], mc_ref[0], i >= n_lat_tiles, 0)
    r = _dot(h.astype(BF16), w_ref[...])
    if n_tmaj:
        o_refs[0][...] = r[:, :n_tmaj]
        o_refs[1][0] = r[:, n_tmaj:]
    else:
        o_refs[0][0] = r


def in_proj(x, g, mods, w, n_lat, n_tmaj):
    b, t, d = x.shape
    n = w.shape[1]
    tm = TOKEN_TILE
    if n_tmaj:
        out_shape = [jax.ShapeDtypeStruct((t, b * n_tmaj), F32), jax.ShapeDtypeStruct((b, t, n - n_tmaj), F32)]
        out_specs = [pl.BlockSpec((tm, n_tmaj), lambda bb, i: (i, bb)),
                     pl.BlockSpec((1, tm, n - n_tmaj), lambda bb, i: (bb, i, 0))]
    else:
        out_shape = [jax.ShapeDtypeStruct((b, t, n), F32)]
        out_specs = [pl.BlockSpec((1, tm, n), lambda bb, i: (bb, i, 0))]
    return pl.pallas_call(
        functools.partial(_inproj_kernel, n_lat_tiles=n_lat // tm, n_tmaj=n_tmaj),
        grid=(b, t // tm),
        in_specs=[pl.BlockSpec((1, tm, d), lambda bb, i: (bb, i, 0)),
                  pl.BlockSpec((1, d), lambda bb, i: (0, 0)),
                  pl.BlockSpec((1, 6, d), lambda bb, i: (bb, 0, 0)),
                  pl.BlockSpec((1, 6, d), lambda bb, i: (b, 0, 0)),
                  pl.BlockSpec((d, n), lambda bb, i: (0, 0), pipeline_mode=pl.Buffered(1))],
        out_specs=out_specs,
        out_shape=out_shape,
        compiler_params=_params("arbitrary", "arbitrary"),
        name="in_proj",
    )(x, g.reshape(1, d), mods, mods, w)


MLP_CHUNK = 1024


def _mix_mlp_kernel(x_ref, a_ref, b_ref, bc_ref, wa_ref, wb_ref, g_ref, ml_ref, mc_ref, w1_ref, w2_ref, fg_ref,
                    o_ref, acc_ref, *, n_lat_tiles, a_tmaj, b_split, final):
    i = pl.program_id(1)
    is_ctx = i >= n_lat_tiles
    m = jnp.where(is_ctx, mc_ref[0], ml_ref[0])
    a = a_ref[...] if a_tmaj else a_ref[0]
    bmix = jnp.where(is_ctx, bc_ref[0], b_ref[0]) if b_split else b_ref[0]
    mix = _dot(a.astype(BF16), wa_ref[...]) + _dot(bmix.astype(BF16), wb_ref[...])
    x = x_ref[0] + m[2:3] * mix
    h = _norm_mod(x, g_ref[...], ml_ref[0], mc_ref[0], is_ctx, 3).astype(BF16)
    for k in range(w1_ref.shape[1] // MLP_CHUNK):
        ks = slice(k * MLP_CHUNK, (k + 1) * MLP_CHUNK)
        hid = jnp.maximum(_dot(h, w1_ref[:, ks]), 0.0)
        part = _dot((hid * hid).astype(BF16), w2_ref[ks, :])
        if k == 0:
            acc_ref[...] = part
        else:
            acc_ref[...] += part
    y = x + m[5:6] * acc_ref[...]
    if final:
        ms = jnp.mean(y * y, axis=-1, keepdims=True)
        y = y * lax.rsqrt(ms + EPS) * fg_ref[...]
    o_ref[0] = y


def mix_mlp(x, a, bm, bm_ctx, w_out, g, mods, w1, w2, final_g, n_lat, n_rows, a_tmaj, final):
    b, t, d = x.shape
    wd = bm.shape[-1]
    hid = w1.shape[1]
    tm = TOKEN_TILE
    nl = n_lat // tm
    once = pl.Buffered(1)
    if a_tmaj:
        a_spec = pl.BlockSpec((tm, wd), lambda bb, i: (i, bb))
    else:
        a_spec = pl.BlockSpec((1, tm, wd), lambda bb, i: (bb, i, 0))
    b_split = bm_ctx is not None
    if b_split:
        b_spec = pl.BlockSpec((1, tm, wd), lambda bb, i: (bb, jnp.minimum(i, nl - 1), 0))
        bc_spec = pl.BlockSpec((1, tm, wd), lambda bb, i: (bb, jnp.maximum(i - nl, 0), 0))
    else:
        b_spec = pl.BlockSpec((1, tm, wd), lambda bb, i: (bb, i, 0))
        bc_spec = pl.BlockSpec((1, tm, wd), lambda bb, i: (0, 0, 0))
        bm_ctx = bm
    return pl.pallas_call(
        functools.partial(_mix_mlp_kernel, n_lat_tiles=nl, a_tmaj=a_tmaj, b_split=b_split, final=final),
        grid=(b, n_rows // tm),
        in_specs=[pl.BlockSpec((1, tm, d), lambda bb, i: (bb, i, 0)),
                  a_spec,
                  b_spec,
                  bc_spec,
                  pl.BlockSpec((wd, d), lambda bb, i: (0, 0), pipeline_mode=once),
                  pl.BlockSpec((wd, d), lambda bb, i: (1, 0), pipeline_mode=once),
                  pl.BlockSpec((1, d), lambda bb, i: (0, 0)),
                  pl.BlockSpec((1, 6, d), lambda bb, i: (bb, 0, 0)),
                  pl.BlockSpec((1, 6, d), lambda bb, i: (b, 0, 0)),
                  pl.BlockSpec((d, hid), lambda bb, i: (0, 0), pipeline_mode=once),
                  pl.BlockSpec((hid, d), lambda bb, i: (0, 0), pipeline_mode=once),
                  pl.BlockSpec((1, d), lambda bb, i: (0, 0))],
        out_specs=pl.BlockSpec((1, tm, d), lambda bb, i: (bb, i, 0)),
        out_shape=jax.ShapeDtypeStruct((b, n_rows, d), F32),
        scratch_shapes=[pltpu.VMEM((tm, d), F32)],
        compiler_params=_params("arbitrary", "arbitrary"),
        name="mix_mlp",
    )(x, a, bm, bm_ctx, w_out, w_out, g.reshape(1, d), mods, mods, w1, w2, final_g.reshape(1, d))


def _conv3(x, w, first, last):
    rows = x.shape[0]
    xm = jnp.where(first, 0.0, pltpu.roll(x, 1, 0))
    xp = jnp.where(last, 0.0, pltpu.roll(x, rows - 1, 0))
    return xm * w[0:1] + x * w[1:2] + xp * w[2:3]


def _shortconv_kernel(x_ref, w_ref, o_ref, *, n_lat):
    x = x_ref[0]
    t = x.shape[0]
    row = lax.broadcasted_iota(jnp.int32, (t, 1), 0)
    first = (row == 0) | (row == n_lat)
    last = (row == n_lat - 1) | (row == t - 1)
    o_ref[0, 0] = _conv3(x, w_ref[...], first, last)


def short_conv(p, w, n_lat):
    b, t, _ = p.shape
    c = w.shape[1]
    cb = GROUP_W
    return pl.pallas_call(
        functools.partial(_shortconv_kernel, n_lat=n_lat),
        grid=(b, c // cb),
        in_specs=[pl.BlockSpec((1, t, cb), lambda bb, j: (bb, 0, j)),
                  pl.BlockSpec((SHORT_K, cb), lambda bb, j: (0, j))],
        out_specs=pl.BlockSpec((1, 1, t, cb), lambda bb, j: (j, bb, 0, 0)),
        out_shape=jax.ShapeDtypeStruct((c // cb, b, t, cb), F32),
        compiler_params=_params("arbitrary", "arbitrary"),
        name="short_conv",
    )(p, w)


S5_KB = LANE
S5_NB = GROUP_W // S5_KB
S5_SB = S5_W // S5_NB


def _s5_kernel(*refs, tc, nb, reverse, finish):
    if finish:
        (u_ref, wbr_ref, wbi_ref, lr_ref, li_ref, wcr_ref, wci_ref, yf_ref, d_ref, gw_ref, gb_ref,
         o_ref, hr_s, hi_s, sr_s, si_s) = refs
    else:
        (u_ref, wbr_ref, wbi_ref, lr_ref, li_ref, wcr_ref, wci_ref,
         o_ref, hr_s, hi_s, sr_s, si_s) = refs

    @pl.when(pl.program_id(0) == 0)
    def _():
        sr_s[...] = jnp.zeros_like(sr_s)
        si_s[...] = jnp.zeros_like(si_s)

    u = u_ref[...]
    ub = u.astype(BF16)
    for j in range(S5_NB):
        uj = ub[:, j * S5_KB:(j + 1) * S5_KB]
        hr_s[:, j * S5_SB:(j + 1) * S5_SB] = _dot(uj, wbr_ref[j])
        hi_s[:, j * S5_SB:(j + 1) * S5_SB] = _dot(uj, wbi_ref[j])

    for s in range(S5_NB):
        sl = slice(s * S5_SB, (s + 1) * S5_SB)
        lr = jnp.broadcast_to(lr_ref[:, sl], (nb, S5_SB))
        li = jnp.broadcast_to(li_ref[:, sl], (nb, S5_SB))

        def body(k, carry, sl=sl, lr=lr, li=li):
            hr, hi = carry
            t = (tc - 1 - k) if reverse else k
            r0 = pl.multiple_of(t * nb, nb)
            nr = lr * hr - li * hi + hr_s[pl.ds(r0, nb), sl]
            ni = lr * hi + li * hr + hi_s[pl.ds(r0, nb), sl]
            hr_s[pl.ds(r0, nb), sl] = nr
            hi_s[pl.ds(r0, nb), sl] = ni
            return nr, ni

        hr, hi = lax.fori_loop(0, tc, body, (sr_s[:, sl], si_s[:, sl]), unroll=4)
        sr_s[:, sl] = hr
        si_s[:, sl] = hi

    for j in range(S5_NB):
        sl = slice(j * S5_SB, (j + 1) * S5_SB)
        co = slice(j * S5_KB, (j + 1) * S5_KB)
        y = _dot(hr_s[:, sl].astype(BF16), wcr_ref[j]) + _dot(hi_s[:, sl].astype(BF16), wci_ref[j])
        if finish:
            o_ref[:, co] = y + yf_ref[:, co] + u[:, co] * d_ref[:, co]
        else:
            o_ref[:, co] = y

    if finish:
        y = o_ref[...]
        y = 0.5 * y * (1.0 + lax.erf(y * (2.0 ** -0.5)))
        z = _dot(y.astype(BF16), gw_ref[...]) + gb_ref[...]
        o_ref[...] = y * jax.nn.sigmoid(z)


def s5_scan_call(u2, tabs, n_lat_steps, n_steps, nb, reverse, extra):
    tc = 32
    rows = tc * nb
    n_chunks = n_steps // tc
    n_lat_chunks = n_lat_steps // tc
    n_ctx_chunks = n_chunks - n_lat_chunks
    if reverse:
        def cmap(c):
            return (n_chunks - 1 - c, 0)
    else:
        def cmap(c):
            return (jnp.where(c < n_ctx_chunks, n_lat_chunks + c, c - n_ctx_chunks), 0)
    wbr, wbi, lr, li, wcr, wci = tabs
    full3 = lambda a: pl.BlockSpec(a.shape, lambda c: (0, 0, 0))
    full2 = lambda a: pl.BlockSpec(a.shape, lambda c: (0, 0))
    in_specs = [pl.BlockSpec((rows, GROUP_W), cmap), full3(wbr), full3(wbi), full2(lr), full2(li),
                full3(wcr), full3(wci)]
    args = [u2, wbr, wbi, lr, li, wcr, wci]
    finish = extra is not None
    if finish:
        yf, dsk, gw, gb = extra
        in_specs += [pl.BlockSpec((rows, GROUP_W), cmap), full2(dsk), full2(gw), full2(gb)]
        args += [yf, dsk, gw, gb]
    return pl.pallas_call(
        functools.partial(_s5_kernel, tc=tc, nb=nb, reverse=reverse, finish=finish),
        grid=(n_chunks,),
        in_specs=in_specs,
        out_specs=pl.BlockSpec((rows, GROUP_W), cmap),
        out_shape=jax.ShapeDtypeStruct(u2.shape, F32),
        scratch_shapes=[pltpu.VMEM((rows, S5_W), F32), pltpu.VMEM((rows, S5_W), F32),
                        pltpu.VMEM((nb, S5_W), F32), pltpu.VMEM((nb, S5_W), F32)],
        compiler_params=_params("arbitrary"),
        name="s5_bwd_finish" if finish else "s5_fwd",
    )(*args)


def _block_diag(x):
    nblk, g, r, c = x.shape
    eye = jnp.eye(g, dtype=x.dtype)
    return jnp.einsum('jgrc,gh->jgrhc', x, eye).reshape(nblk, g * r, g * c)


def s5_tables(lam_re, lam_im, log_dt, b_re, b_im, c_re, c_im):
    lam_re = jnp.minimum(lam_re.astype(F32), -1e-4)
    lam_im = lam_im.astype(F32)
    dt = jnp.exp(log_dt.astype(F32))[:, None]
    mag = jnp.exp(lam_re * dt)
    lb_re = mag * jnp.cos(lam_im * dt)
    lb_im = mag * jnp.sin(lam_im * dt)
    den = lam_re * lam_re + lam_im * lam_im
    f_re = ((lb_re - 1.0) * lam_re + lb_im * lam_im) / den
    f_im = (lb_im * lam_re - (lb_re - 1.0) * lam_im) / den
    b_re = b_re.astype(F32)
    b_im = b_im.astype(F32)
    bb_re = f_re[..., None] * b_re - f_im[..., None] * b_im
    bb_im = f_re[..., None] * b_im + f_im[..., None] * b_re
    gpb = S5_KB // S5_GROUP
    to_b = lambda x: _block_diag(
        jnp.swapaxes(x, 1, 2).reshape(S5_NB, gpb, S5_GROUP, S5_STATE)).astype(BF16)
    to_c = lambda x: _block_diag(
        jnp.swapaxes(x.astype(F32), 1, 2).reshape(S5_NB, gpb, S5_STATE, S5_GROUP)).astype(BF16)
    return (to_b(bb_re), to_b(bb_im), lb_re.reshape(1, S5_W), lb_im.reshape(1, S5_W),
            to_c(c_re), to_c(-c_im.astype(F32)))


def dft_tables(n):
    nn = 2 * n
    f = jnp.arange(n, dtype=jnp.int32)[:, None]
    t = jnp.arange(n, dtype=jnp.int32)[None, :]
    q = 1 << (int(math.log2(n)) // 2)

    def factor(tt):
        ang = ((f * tt[None, :]) % nn).astype(F32) * (2.0 * math.pi / nn)
        return jnp.cos(ang), jnp.sin(ang)

    c1, s1 = factor(jnp.arange(n // q, dtype=jnp.int32) * q)
    c0, s0 = factor(jnp.arange(q, dtype=jnp.int32))
    cos = (c1[:, :, None] * c0[:, None, :] - s1[:, :, None] * s0[:, None, :]).reshape(n, n)
    sin = (s1[:, :, None] * c0[:, None, :] + c1[:, :, None] * s0[:, None, :]).reshape(n, n)
    nyq = jnp.where(t % 2 == 0, 1.0, -1.0).astype(F32)
    msin = jnp.where(f == 0, nyq, -sin)
    fwd = jnp.concatenate([cos, msin], axis=0)
    wf = jnp.where(jnp.arange(n) == 0, 1.0 / nn, 2.0 / nn).astype(F32)
    return fwd, wf


def _filtdft_kernel(fh_ref, fl_ref, hh_ref, hl_ref, o_ref):
    fh = fh_ref[...]
    o_ref[...] = _dot(fh, hh_ref[0]) + _dot(fh, hl_ref[0]) + _dot(fl_ref[...], hh_ref[0])


def filter_dft(fwd_hi, fwd_lo, h_sum, h_dif):
    n2, n = fwd_hi.shape
    c = h_sum.shape[1]
    tr = min(256, n)
    tcn = 512
    nrb = n // tr
    hh, hl = _split_bf16(jnp.stack([h_sum, h_dif]))
    frow = lambda i, j: (jnp.where(i == 2 * nrb, nrb, i), 0)
    hsel = lambda i, j: (jnp.where((i >= nrb) & (i < 2 * nrb), 1, 0), 0, j)
    return pl.pallas_call(
        _filtdft_kernel,
        grid=(2 * nrb + 1, c // tcn),
        in_specs=[pl.BlockSpec((tr, n), frow),
                  pl.BlockSpec((tr, n), frow),
                  pl.BlockSpec((1, n, tcn), hsel),
                  pl.BlockSpec((1, n, tcn), hsel)],
        out_specs=pl.BlockSpec((tr, tcn), lambda i, j: (i, j)),
        out_shape=jax.ShapeDtypeStruct((n2 + tr, c), F32),
        compiler_params=_params("arbitrary", "arbitrary"),
        name="filter_dft",
    )(fwd_hi, fwd_lo, hh, hl)


def hyena_filter_response(n, fwd_hi, fwd_lo, wf, f_w1, f_b1, f_w2, f_b2, f_w3, f_freq, log_decay):
    t = jnp.linspace(0.0, 1.0, n, dtype=F32)[:, None]
    w = 2.0 * math.pi * jnp.arange(n, dtype=F32)[:, None] / n
    bands = jnp.linspace(1e-4, HY_BANDS - 1, HY_BANDS, dtype=F32)[None, :]
    z = jnp.concatenate([t, jnp.cos(bands * w), -jnp.sin(bands * w)], axis=-1)
    freq = f_freq.astype(F32)
    hid = jnp.sin(freq[0] * (jnp.dot(z, f_w1.astype(F32), precision=HIGHEST) + f_b1.astype(F32)))
    hid = jnp.sin(freq[1] * (jnp.dot(hid, f_w2.astype(F32), precision=HIGHEST) + f_b2.astype(F32)))
    h = jnp.dot(hid, f_w3.astype(F32), precision=HIGHEST).reshape(n, 2, HY_ORDER, GROUP_W)
    h = h * jnp.exp(-t[:, :, None, None] * jnp.exp(log_decay.astype(F32)))
    cw = HY_ORDER * GROUP_W
    h_fwd = h[:, 0].reshape(n, cw)
    h_bwd = h[:, 1].reshape(n, cw).at[0].set(0.0)
    resp = filter_dft(fwd_hi, fwd_lo, h_fwd + h_bwd, h_fwd - h_bwd)
    k_re = resp[:n]
    first = (jnp.arange(n) == 0)[:, None]
    k_im = jnp.where(first, resp[2 * n:2 * n + 1], resp[n:2 * n])
    k_re = (k_re * wf[:, None]).reshape(n, HY_ORDER, GROUP_W).transpose(1, 0, 2)
    k_im = (k_im * wf[:, None]).reshape(n, HY_ORDER, GROUP_W).transpose(1, 0, 2)
    return k_re, k_im


def _hyena_kernel(z0_ref, x1_ref, x2_ref, cw_ref, cb_ref, fw_ref, inv_ref, kr_ref, ki_ref, bias_ref,
                  o_ref, zb_s, zf_s, acc_s):
    o = pl.program_id(1)
    f = pl.program_id(2)
    nf = pl.num_programs(2)
    c = GROUP_W

    def stream(ref, s):
        x = ref[0]
        row = lax.broadcasted_iota(jnp.int32, (x.shape[0], 1), 0)
        cols = slice(s * c, (s + 1) * c)
        return _conv3(x, cw_ref[:, cols], row == 0, row == x.shape[0] - 1) + cb_ref[:, cols]

    @pl.when((o == 0) & (f == 0))
    def _():
        z0 = stream(z0_ref, 0)
        zf_s[...] = z0
        zb_s[...] = z0.astype(BF16)

    @pl.when(f == 0)
    def _():
        acc_s[...] = jnp.zeros_like(acc_s)

    xf = _dot(fw_ref[0], zb_s[...])
    fb = xf.shape[0] // 2
    xr = xf[:fb]
    xi = xf[fb:]
    kr = kr_ref[0]
    ki = ki_ref[0]
    row = lax.broadcasted_iota(jnp.int32, (fb, 1), 0)
    packed = (row == 0) & (f == 0)
    yr = jnp.where(packed, xr * kr, xr * kr - xi * ki)
    yi = jnp.where(packed, xi * ki, xr * ki + xi * kr)
    yf = jnp.concatenate([yr, yi], axis=0).astype(BF16)
    acc_s[...] += _dot(inv_ref[0], yf)

    @pl.when(f == nf - 1)
    def _():
        bias = bias_ref[...]

        @pl.when(o == 0)
        def _():
            z1 = stream(x1_ref, 1) * (acc_s[...] + zf_s[...] * bias[0:1])
            zf_s[...] = z1
            zb_s[...] = z1.astype(BF16)

        @pl.when(o == 1)
        def _():
            o_ref[0] = stream(x2_ref, 2) * (acc_s[...] + zf_s[...] * bias[1:2])


HY_FREQ_BLOCK = 256


def dft_blocks(fwd_hi):
    n = fwd_hi.shape[1]
    fb = min(HY_FREQ_BLOCK, n)
    blocks = fwd_hi.reshape(2, n // fb, fb, n).transpose(1, 0, 2, 3).reshape(n // fb, 2 * fb, n)
    return blocks, blocks.transpose(0, 2, 1)


def hyena_call(p, n, row_blk, conv_w, conv_b, fwd_blk, inv_blk, k_re, k_im, bias):
    b = p.shape[0]
    c = GROUP_W
    nf, fb2, _ = fwd_blk.shape
    fb = fb2 // 2
    once = pl.Buffered(1)
    in_specs = [pl.BlockSpec((1, n, c), lambda bb, o, f: (bb, row_blk, 0), pipeline_mode=once),
                pl.BlockSpec((1, n, c), lambda bb, o, f: (bb, row_blk, 1), pipeline_mode=once),
                pl.BlockSpec((1, n, c), lambda bb, o, f: (bb, row_blk, 2), pipeline_mode=once),
                pl.BlockSpec((SHORT_K, 3 * c), lambda bb, o, f: (0, 0)),
                pl.BlockSpec((1, 3 * c), lambda bb, o, f: (0, 0)),
                pl.BlockSpec((1, fb2, n), lambda bb, o, f: (f, 0, 0)),
                pl.BlockSpec((1, n, fb2), lambda bb, o, f: (f, 0, 0)),
                pl.BlockSpec((1, fb, c), lambda bb, o, f: (o, f, 0)),
                pl.BlockSpec((1, fb, c), lambda bb, o, f: (o, f, 0)),
                pl.BlockSpec((HY_ORDER, c), lambda bb, o, f: (0, 0))]
    return pl.pallas_call(
        _hyena_kernel,
        grid=(b, HY_ORDER, nf),
        in_specs=in_specs,
        out_specs=pl.BlockSpec((1, n, c), lambda bb, o, f: (bb, 0, 0)),
        out_shape=jax.ShapeDtypeStruct((b, n, c), F32),
        scratch_shapes=[pltpu.VMEM((n, c), BF16), pltpu.VMEM((n, c), F32), pltpu.VMEM((n, c), F32)],
        compiler_params=_params("arbitrary", "arbitrary", "arbitrary"),
        name="hyena",
    )(p, p, p, conv_w, conv_b.reshape(1, 3 * c), fwd_blk, inv_blk, k_re, k_im, bias)


def _rwkv_lora_kernel(p_ref, ww_ref, wa_ref, wg_ref, w0_ref, a0_ref, wp_ref, ap_ref, g_ref):
    x = p_ref[0]
    wa_in = x[:, :LANE]
    th = jnp.tanh(wa_in).astype(BF16)
    lin = wa_in.astype(BF16)
    w0 = w0_ref[...]
    a0 = a0_ref[...]
    for d in range(2):
        wp_ref[d, 0] = w0[d:d + 1] + _dot(th, ww_ref[d])
        ap_ref[d, 0] = a0[d:d + 1] + _dot(lin, wa_ref[d])
    g_ref[0] = _dot(jax.nn.sigmoid(x[:, LANE:]).astype(BF16), wg_ref[...])


def rwkv_lora(p, w_up, a_up, g_up, w0, a0):
    b, t, _ = p.shape
    c = GROUP_W
    tm = TOKEN_TILE
    zeros = jnp.zeros((2, 64, c), F32)
    ww = jnp.concatenate([w_up.astype(F32), zeros], axis=1).astype(BF16)
    wa = jnp.concatenate([zeros, a_up.astype(F32)], axis=1).astype(BF16)
    both = jax.ShapeDtypeStruct((2, b, t, c), F32)
    both_spec = pl.BlockSpec((2, 1, tm, c), lambda bb, i: (0, bb, i, 0))
    return pl.pallas_call(
        _rwkv_lora_kernel,
        grid=(b, t // tm),
        in_specs=[pl.BlockSpec((1, tm, RW_LORA_W), lambda bb, i: (bb, i, 3 * c // RW_LORA_W)),
                  pl.BlockSpec((2, LANE, c), lambda bb, i: (0, 0, 0)),
                  pl.BlockSpec((2, LANE, c), lambda bb, i: (0, 0, 0)),
                  pl.BlockSpec((LANE, c), lambda bb, i: (0, 0)),
                  pl.BlockSpec((2, c), lambda bb, i: (0, 0)),
                  pl.BlockSpec((2, c), lambda bb, i: (0, 0))],
        out_specs=[both_spec, both_spec, pl.BlockSpec((1, tm, c), lambda bb, i: (bb, i, 0))],
        out_shape=[both, both, jax.ShapeDtypeStruct((b, t, c), F32)],
        compiler_params=_params("arbitrary", "arbitrary"),
        name="rwkv_lora",
    )(p, ww, wa, g_up.astype(BF16), w0, a0)


SCAN_UNROLL = 8


def _rwkv_scan_kernel(wpf_ref, wpb_ref, apf_ref, apb_ref, kf_ref, kb_ref, vf_ref, vb_ref, rf_ref, rb_ref,
                      kkc_ref, kac_ref, yf_ref, yb_ref, s_ref, w_s, kk_s, b_s, ke_s, r_s, *, tb):
    n = RW_HEAD

    @pl.when(pl.program_id(0) == 0)
    def _():
        s_ref[...] = jnp.zeros_like(s_ref)

    def both(f_ref, b_ref, t):
        return jnp.concatenate([f_ref[t], b_ref[tb - 1 - t]], axis=-1)

    def prepare(t, slot):
        k = both(kf_ref, kb_ref, t)
        a = jax.nn.sigmoid(both(apf_ref, apb_ref, t))
        kk = k * kkc_ref[...]
        nrm = jnp.sqrt(jnp.sum(kk * kk, axis=0, keepdims=True))
        kk = kk / jnp.maximum(nrm, 1e-12)
        wl = -jax.nn.softplus(-both(wpf_ref, wpb_ref, t)) - 0.5
        w_s[slot] = jnp.exp(-jnp.exp(wl))
        kk_s[slot] = kk
        b_s[slot] = kk * a
        ke_s[slot] = k * (1.0 + (a - 1.0) * kac_ref[...])
        r_s[slot] = both(rf_ref, rb_ref, t)

    prepare(0, 0)
    sa0 = jnp.zeros((n, LANE), F32)
    for j in range(n):
        sa0 = sa0 + s_ref[j] * kk_s[0, j:j + 1, :]

    def step(t, sa):
        slot = t % 2
        nxt = 1 - slot
        prepare(jnp.minimum(t + 1, tb - 1), nxt)
        v = both(vf_ref, vb_ref, t)

        def keys(j, carry):
            y, sa_next = carry
            row = pl.ds(j, 1)
            s = s_ref[j] * w_s[slot, row, :] - sa * b_s[slot, row, :] + v * ke_s[slot, row, :]
            s_ref[j] = s
            return y + s * r_s[slot, row, :], sa_next + s * kk_s[nxt, row, :]

        zero = jnp.zeros((n, LANE), F32)
        y, sa_next = lax.fori_loop(0, n, keys, (zero, zero), unroll=SCAN_UNROLL)
        yf_ref[t] = y
        yb_ref[tb - 1 - t] = y
        return sa_next

    lax.fori_loop(0, tb, step, sa0)


def rwkv_scan(wp0, wp1, ap0, ap1, k, v, r, kkc, kac, n_lat):
    t = k.shape[0]
    tb = 32
    n = RW_HEAD
    half = LANE // 2
    n_blk = t // tb
    n_lat_blk = n_lat // tb
    n_ctx_blk = n_blk - n_lat_blk

    def fblk(c):
        return jnp.where(c < n_ctx_blk, n_lat_blk + c, c - n_ctx_blk)

    def rblk(c):
        return n_blk - 1 - c

    fwd_in = pl.BlockSpec((tb, n, half), lambda c: (fblk(c), 0, 0))
    bwd_in = pl.BlockSpec((tb, n, half), lambda c: (rblk(c), 0, 0))
    cst = pl.BlockSpec((n, LANE), lambda c: (0, 0))
    buf = pltpu.VMEM((2, n, LANE), F32)
    out = jax.ShapeDtypeStruct((t, n, LANE), F32)
    return pl.pallas_call(
        functools.partial(_rwkv_scan_kernel, tb=tb),
        grid=(n_blk,),
        in_specs=[fwd_in, bwd_in] * 5 + [cst] * 2,
        out_specs=[pl.BlockSpec((tb, n, LANE), lambda c: (fblk(c), 0, 0)),
                   pl.BlockSpec((tb, n, LANE), lambda c: (rblk(c), 0, 0))],
        out_shape=[out, out],
        scratch_shapes=[pltpu.VMEM((n, n, LANE), F32), buf, buf, buf, buf, buf],
        compiler_params=_params("arbitrary"),
        name="rwkv_scan",
    )(wp0, wp1, ap0, ap1, k, k, v, v, r, r, kkc, kac)


def _rwkv_post_kernel(y_ref, r_ref, k_ref, v_ref, ap_ref, g_ref, m_ref, ka_ref, rk_ref, lg_ref, lb_ref,
                      o_ref):
    m = m_ref[...]

    def head_mean(x):
        hi, lo = _split_bf16(x)
        return _dot(hi, m) + _dot(lo, m)

    y = y_ref[0]
    d = y - head_mean(y)
    var = head_mean(d * d)
    yn = d * lax.rsqrt(var + RW_LN_EPS) * lg_ref[...] + lb_ref[...]
    a_sum = jax.nn.sigmoid(ap_ref[0, 0]) + jax.nn.sigmoid(ap_ref[1, 0])
    k_sum = k_ref[0, 0] * (2.0 + (a_sum - 2.0) * ka_ref[...])
    bonus = head_mean(r_ref[0, 0] * k_sum * rk_ref[...]) * float(RW_HEAD) * v_ref[0, 0]
    o_ref[0] = (yn + bonus) * g_ref[0]


def rwkv_post(y, rkv, ap, gate, k_a, r_k, ln_g, ln_b, n_rows):
    b, t, c = y.shape
    tm = TOKEN_TILE
    hm = jnp.kron(jnp.eye(RW_HEADS, dtype=F32), jnp.full((RW_HEAD, RW_HEAD), 1.0 / RW_HEAD, F32)).astype(BF16)
    tok = pl.BlockSpec((1, tm, c), lambda bb, i: (bb, i, 0))
    vec = pl.BlockSpec((1, c), lambda bb, i: (0, 0))
    rkv_spec = lambda q: pl.BlockSpec((1, 1, tm, c), lambda bb, i: (q, bb, i, 0))
    return pl.pallas_call(
        _rwkv_post_kernel,
        grid=(b, n_rows // tm),
        in_specs=[tok, rkv_spec(0), rkv_spec(1), rkv_spec(2),
                  pl.BlockSpec((2, 1, tm, c), lambda bb, i: (0, bb, i, 0)), tok,
                  pl.BlockSpec((c, c), lambda bb, i: (0, 0)), vec, vec, vec, vec],
        out_specs=tok,
        out_shape=jax.ShapeDtypeStruct((b, n_rows, c), F32),
        compiler_params=_params("arbitrary", "arbitrary"),
        name="rwkv_post",
    )(y, rkv, rkv, rkv, ap, gate, hm, k_a.reshape(1, c), r_k.reshape(1, c),
      ln_g.reshape(1, c), ln_b.reshape(1, c))


def _to_scan_layout(a):
    b, t, _ = a.shape
    return a.reshape(b, t, RW_HEADS, RW_HEAD).transpose(1, 3, 0, 2).reshape(t, RW_HEAD, b * RW_HEADS)


def _from_scan_layout(yf, yb, b):
    t = yf.shape[0]
    half = b * RW_HEADS
    y = yf[..., :half] + yb[..., half:]
    return y.reshape(t, RW_HEAD, b, RW_HEADS).transpose(2, 0, 3, 1).reshape(b, t, GROUP_W)


def _chain_const(x, b):
    return jnp.tile(x.astype(F32).reshape(RW_HEADS, RW_HEAD).T, (1, 2 * b))


def _rope(x, cos, sin):
    lane = lax.broadcasted_iota(jnp.int32, (1, LANE), 1)
    first = (lane % (2 * ROPE_FREQS)) < ROPE_FREQS
    partner = jnp.where(first, pltpu.roll(x, LANE - ROPE_FREQS, 1), pltpu.roll(x, ROPE_FREQS, 1))
    return x * cos + partner * sin


def _attn_tile(q, k, v, lam):
    lane = lax.broadcasted_iota(jnp.int32, (1, LANE), 1)
    m0 = lane < DA_HEAD
    q0 = jnp.where(m0, q, 0.0).astype(BF16)
    q1 = jnp.where(m0, 0.0, q).astype(BF16)
    dn = (((1,), (1,)), ((), ()))
    s0 = lax.dot_general(q0, k, dn, preferred_element_type=F32)
    s1 = lax.dot_general(q1, k, dn, preferred_element_type=F32)
    p0 = jnp.exp2(s0 - jnp.max(s0, axis=-1, keepdims=True))
    p1 = jnp.exp2(s1 - jnp.max(s1, axis=-1, keepdims=True))
    c0 = 1.0 / jnp.sum(p0, axis=-1, keepdims=True)
    c1 = lam / jnp.sum(p1, axis=-1, keepdims=True)
    return _dot((p0 * c0 - p1 * c1).astype(BF16), v)


def _attn_kernel(q_ref, k_ref, v_ref, cq_ref, sq_ref, ck_ref, sk_ref, lp_ref, g_ref, o_ref,
                 kr_s, vb_s, *, n_lat, n_lat_tiles, lam_init):
    i = pl.program_id(2)

    @pl.when(i == 0)
    def _():
        kr_s[...] = _rope(k_ref[0], ck_ref[...], sk_ref[...]).astype(BF16)
        vb_s[...] = v_ref[0].astype(BF16)

    lp = lp_ref[...]
    lam = (jnp.exp(jnp.sum(lp[0:1] * lp[1:2], axis=-1, keepdims=True))
           - jnp.exp(jnp.sum(lp[2:3] * lp[3:4], axis=-1, keepdims=True)) + lam_init)
    q = _rope(q_ref[0], cq_ref[...], sq_ref[...]) * (DA_SCALE * math.log2(math.e))

    def finish(o):
        on = o * lax.rsqrt(jnp.mean(o * o, axis=-1, keepdims=True) + DA_SUBLN_EPS)
        o_ref[0] = on * g_ref[...] * (1.0 - lam_init)

    @pl.when(i < n_lat_tiles)
    def _():
        finish(_attn_tile(q, kr_s[...], vb_s[...], lam))

    @pl.when(i >= n_lat_tiles)
    def _():
        finish(_attn_tile(q, kr_s[n_lat:], vb_s[n_lat:], lam))


def diff_attention(p, col0, cosf, sins, lam_p, subln_g, lam_init, n_lat, n_rows):
    b, t, _ = p.shape
    tq = TOKEN_TILE
    off = col0 // LANE
    hq = DA_HEADS
    return pl.pallas_call(
        functools.partial(_attn_kernel, n_lat=n_lat, n_lat_tiles=n_lat // tq, lam_init=lam_init),
        grid=(b, hq, n_rows // tq),
        in_specs=[pl.BlockSpec((1, tq, LANE), lambda bb, h, i: (bb, i, off + h)),
                  pl.BlockSpec((1, t, LANE), lambda bb, h, i: (bb, 0, off + hq + h)),
                  pl.BlockSpec((1, t, LANE), lambda bb, h, i: (bb, 0, off + 2 * hq + h)),
                  pl.BlockSpec((tq, LANE), lambda bb, h, i: (i, 0)),
                  pl.BlockSpec((tq, LANE), lambda bb, h, i: (i, 0)),
                  pl.BlockSpec((t, LANE), lambda bb, h, i: (0, 0)),
                  pl.BlockSpec((t, LANE), lambda bb, h, i: (0, 0)),
                  pl.BlockSpec((4, DA_HEAD), lambda bb, h, i: (0, 0)),
                  pl.BlockSpec((1, LANE), lambda bb, h, i: (0, 0))],
        out_specs=pl.BlockSpec((1, tq, LANE), lambda bb, h, i: (bb, i, h)),
        out_shape=jax.ShapeDtypeStruct((b, n_rows, hq * DA_V), F32),
        scratch_shapes=[pltpu.VMEM((t, LANE), BF16), pltpu.VMEM((t, LANE), BF16)],
        compiler_params=_params("arbitrary", "arbitrary", "arbitrary"),
        name="diff_attention",
    )(p, p, p, cosf, sins, cosf, sins, lam_p, subln_g.reshape(1, DA_V))


def rope_tables(n_lat, n_ctx):
    rows = n_lat // GRID_W
    row = jnp.repeat(jnp.arange(rows, dtype=F32), GRID_W)
    col = jnp.tile(jnp.arange(GRID_W, dtype=F32), rows)
    inv = ROPE_BASE ** (-jnp.arange(ROPE_FREQS, dtype=F32) / ROPE_FREQS)
    ang = jnp.stack([row[:, None] * inv, col[:, None] * inv], axis=1)
    cos, sin = jnp.cos(ang), jnp.sin(ang)
    cosf = jnp.concatenate([cos, cos], axis=-1).reshape(n_lat, DA_HEAD)
    sins = jnp.concatenate([-sin, sin], axis=-1).reshape(n_lat, DA_HEAD)
    cosf = jnp.concatenate([jnp.tile(cosf, (1, 2)), jnp.ones((n_ctx, LANE), F32)], axis=0)
    sins = jnp.concatenate([jnp.tile(sins, (1, 2)), jnp.zeros((n_ctx, LANE), F32)], axis=0)
    return cosf, sins


def _even_mixers(xs, mods, keep_ctx, n_lat, norm1_g, w_in, s5p, hyp, dft):
    b, t, _ = xs.shape
    n_ctx = t - n_lat
    u_t, p_h = in_proj(xs, norm1_g, mods, w_in.astype(BF16), n_lat, GROUP_W)

    (lam_re, lam_im, log_dt, b_re, b_im, c_re, c_im, d_skip, glu_w, glu_b) = s5p
    u2 = u_t.reshape(t * b, GROUP_W)
    tabs = [s5_tables(lam_re[d], lam_im[d], log_dt[d], b_re[d], b_im[d], c_re[d], c_im[d]) for d in range(2)]
    y_f = s5_scan_call(u2, tabs[0], n_lat, t, b, False, None)
    a_t = s5_scan_call(u2, tabs[1], n_lat, t, b, True,
                       (y_f, d_skip.reshape(1, GROUP_W), glu_w.astype(BF16), glu_b.reshape(1, GROUP_W)))

    (conv_w, conv_b, f_w1, f_b1, f_w2, f_b2, f_w3, f_freq, log_decay, bias) = hyp
    fh_l, fl_l, (fw_l, inv_l), wf_l = dft[0]
    kr, ki = hyena_filter_response(n_lat, fh_l, fl_l, wf_l, f_w1, f_b1, f_w2, f_b2, f_w3, f_freq, log_decay)
    b_l = hyena_call(p_h, n_lat, 0, conv_w, conv_b, fw_l, inv_l, kr, ki, bias)
    b_c = None
    if keep_ctx:
        fh_c, fl_c, (fw_c, inv_c), wf_c = dft[1]
        kr, ki = hyena_filter_response(n_ctx, fh_c, fl_c, wf_c, f_w1, f_b1, f_w2, f_b2, f_w3, f_freq,
                                       log_decay)
        b_c = hyena_call(p_h, n_ctx, n_lat // n_ctx, conv_w, conv_b, fw_c, inv_c, kr, ki, bias)
    return a_t.reshape(t, b * GROUP_W), b_l, b_c


def _odd_mixers(xs, mods, keep_ctx, n_lat, lam_init, norm1_g, w_in, rwp, dap, rope):
    b, t, _ = xs.shape
    n_rows = t if keep_ctx else n_lat
    (p,) = in_proj(xs, norm1_g, mods, w_in.astype(BF16), n_lat, 0)

    (conv_w, w0, w_up, a0, a_up, g_up, k_k, k_a, r_k, ln_g, ln_b) = rwp
    rkv = short_conv(p, conv_w, n_lat)
    wp, ap, gate = rwkv_lora(p, w_up, a_up, g_up, w0, a0)
    lay = _to_scan_layout
    yf, yb = rwkv_scan(lay(wp[0]), lay(wp[1]), lay(ap[0]), lay(ap[1]), lay(rkv[1]), lay(rkv[2]), lay(rkv[0]),
                       _chain_const(k_k, b), _chain_const(k_a, b), n_lat)
    a_m = rwkv_post(_from_scan_layout(yf, yb, b), rkv, ap, gate, k_a, r_k, ln_g, ln_b, n_rows)

    lam_p, subln_g = dap
    b_m = diff_attention(p, RW_IN, rope[0], rope[1], lam_p, subln_g, lam_init, n_lat, n_rows)
    return a_m, b_m, None


def kernel(x, c, ctx, c_ctx, ada_w, ada_b, norm1_g, norm2_g, mlp_w1, mlp_w2, final_g, ev_w_in, ev_w_out, s5_lam_re, s5_lam_im, s5_log_dt, s5_b_re, s5_b_im, s5_c_re, s5_c_im, s5_d, s5_glu_w, s5_glu_b, hy_conv_w, hy_conv_b, hy_f_w1, hy_f_b1, hy_f_w2, hy_f_b2, hy_f_w3, hy_f_freq, hy_log_decay, hy_bias, od_w_in, od_w_out, rw_conv_w, rw_w0, rw_w_up, rw_a0, rw_a_up, rw_g_up, rw_k_k, rw_k_a, rw_r_k, rw_ln_g, rw_ln_b, da_lam, da_subln_g):
    b, n_lat, d = x.shape
    n_ctx = ctx.shape[1]
    assert d == D_MODEL and b == SUBLANE
    assert n_lat % TOKEN_TILE == 0 and n_ctx % TOKEN_TILE == 0 and n_lat % n_ctx == 0
    xs = jnp.concatenate([x.astype(F32), ctx.astype(F32)], axis=1)

    c_rows = jnp.zeros((16, d), F32).at[:b].set(c.astype(F32)).at[b].set(c_ctx.astype(F32))
    depth = ada_w.shape[0]
    mods_all = ada_mods(c_rows, ada_w, ada_b).reshape(depth, 16, 6, d)

    rope = rope_tables(n_lat, n_ctx)
    dft = []
    for n in (n_lat, n_ctx):
        fwd, wf = dft_tables(n)
        hi, lo = _split_bf16(fwd)
        dft.append((hi, lo, dft_blocks(hi), wf))

    for l in range(depth):
        keep_ctx = l < depth - 1
        i = l // 2
        mods = mods_all[l]
        if l % 2 == 0:
            s5p = (s5_lam_re[i], s5_lam_im[i], s5_log_dt[i], s5_b_re[i], s5_b_im[i], s5_c_re[i], s5_c_im[i],
                   s5_d[i], s5_glu_w[i], s5_glu_b[i])
            hyp = (hy_conv_w[i], hy_conv_b[i], hy_f_w1[i], hy_f_b1[i], hy_f_w2[i], hy_f_b2[i], hy_f_w3[i],
                   hy_f_freq[i], hy_log_decay[i], hy_bias[i])
            a_m, b_m, b_c = _even_mixers(xs, mods, keep_ctx, n_lat, norm1_g[l], ev_w_in[i], s5p, hyp, dft)
            w_out = ev_w_out[i]
        else:
            rwp = (rw_conv_w[i], rw_w0[i], rw_w_up[i], rw_a0[i], rw_a_up[i], rw_g_up[i], rw_k_k[i],
                   rw_k_a[i], rw_r_k[i], rw_ln_g[i], rw_ln_b[i])
            lam_init = 0.8 - 0.6 * math.exp(-0.3 * l)
            a_m, b_m, b_c = _odd_mixers(xs, mods, keep_ctx, n_lat, lam_init, norm1_g[l], od_w_in[i], rwp,
                                        (da_lam[i], da_subln_g[i]), rope)
            w_out = od_w_out[i]
        n_rows = n_lat + n_ctx if keep_ctx else n_lat
        xs = mix_mlp(xs, a_m, b_m, b_c, w_out.astype(BF16), norm2_g[l], mods, mlp_w1[l].astype(BF16),
                     mlp_w2[l].astype(BF16), final_g, n_lat, n_rows, l % 2 == 0, l == depth - 1)
    return xs
```

```python
import functools
import math

import jax
import jax.numpy as jnp
from jax import lax
from jax.experimental import pallas as pl
from jax.experimental.pallas import tpu as pltpu

F32 = jnp.float32
BF16 = jnp.bfloat16
HIGHEST = lax.Precision.HIGHEST

D_MODEL = 1024
DEPTH = 4
GRID_W = 64
MLP_HIDDEN = 4 * D_MODEL
GROUP_W = D_MODEL // 2
EPS = 1e-6

S5_GROUP = 16
S5_GROUPS = GROUP_W // S5_GROUP
S5_STATE = 64
S5_W = S5_GROUPS * S5_STATE

HY_ORDER = 2
HY_EMB = 33
HY_BANDS = (HY_EMB - 1) // 2
SHORT_K = 3

RW_HEAD = 64
RW_HEADS = GROUP_W // RW_HEAD
RW_LORA_W = 256
RW_LN_EPS = 64e-5
RW_IN = 3 * GROUP_W + RW_LORA_W

DA_HEADS = 4
DA_HEAD = 64
DA_V = 2 * DA_HEAD
DA_SCALE = DA_HEAD ** -0.5
DA_SUBLN_EPS = 1e-5
ROPE_BASE = 10000.0
ROPE_FREQS = DA_HEAD // 4

LANE = 128
SUBLANE = 8
TOKEN_TILE = 256
VMEM_LIMIT = 48 * 1024 * 1024


def _params(*sem):
    return pltpu.CompilerParams(dimension_semantics=sem, vmem_limit_bytes=VMEM_LIMIT)


def _split_bf16(x):
    hi = x.astype(BF16)
    lo = (x - hi.astype(F32)).astype(BF16)
    return hi, lo


def _dot(a, b):
    return jnp.dot(a, b, preferred_element_type=F32)


def _ada_kernel(c_ref, w_ref, b_ref, o_ref):
    c = c_ref[...]
    s = c * jax.nn.sigmoid(c)
    o_ref[0] = jnp.dot(s, w_ref[0], preferred_element_type=F32, precision=HIGHEST) + b_ref[0]


def ada_mods(c_rows, ada_w, ada_b):
    depth, d, n = ada_w.shape
    tn = 1024
    return pl.pallas_call(
        _ada_kernel,
        grid=(depth, n // tn),
        in_specs=[pl.BlockSpec((16, d), lambda l, j: (0, 0)),
                  pl.BlockSpec((1, d, tn), lambda l, j: (l, 0, j)),
                  pl.BlockSpec((1, 1, tn), lambda l, j: (l, 0, j))],
        out_specs=pl.BlockSpec((1, 16, tn), lambda l, j: (l, 0, j)),
        out_shape=jax.ShapeDtypeStruct((depth, 16, n), F32),
        compiler_params=_params("arbitrary", "arbitrary"),
        name="ada_mods",
    )(c_rows, ada_w, ada_b.reshape(depth, 1, n))


def _norm_mod(x, g, ml, mc, is_ctx, k):
    ms = jnp.mean(x * x, axis=-1, keepdims=True)
    xn = x * lax.rsqrt(ms + EPS) * g
    m = jnp.where(is_ctx, mc, ml)
    return xn * (1.0 + m[k + 1:k + 2]) + m[k:k + 1]


def _inproj_kernel(x_ref, g_ref, ml_ref, mc_ref, w_ref, *o_refs, n_lat_tiles, n_tmaj):
    i = pl.program_id(1)
    h = _norm_mod(x_ref[0], g_ref[...], ml_ref[0], mc_ref[0], i >= n_lat_tiles, 0)
    r = _dot(h.astype(BF16), w_ref[...])
    if n_tmaj:
        o_refs[0][...] = r[:, :n_tmaj]
        o_refs[1][0] = r[:, n_tmaj:]
    else:
        o_refs[0][0] = r


def in_proj(x, g, mods, w, n_lat, n_tmaj):
    b, t, d = x.shape
    n = w.shape[1]
    tm = TOKEN_TILE
    if n_tmaj:
        out_shape = [jax.ShapeDtypeStruct((t, b * n_tmaj), F32), jax.ShapeDtypeStruct((b, t, n - n_tmaj), F32)]
        out_specs = [pl.BlockSpec((tm, n_tmaj), lambda bb, i: (i, bb)),
                     pl.BlockSpec((1, tm, n - n_tmaj), lambda bb, i: (bb, i, 0))]
    else:
        out_shape = [jax.ShapeDtypeStruct((b, t, n), F32)]
        out_specs = [pl.BlockSpec((1, tm, n), lambda bb, i: (bb, i, 0))]
    return pl.pallas_call(
        functools.partial(_inproj_kernel, n_lat_tiles=n_lat // tm, n_tmaj=n_tmaj),
        grid=(b, t // tm),
        in_specs=[pl.BlockSpec((1, tm, d), lambda bb, i: (bb, i, 0)),
                  pl.BlockSpec((1, d), lambda bb, i: (0, 0)),
                  pl.BlockSpec((1, 6, d), lambda bb, i: (bb, 0, 0)),
                  pl.BlockSpec((1, 6, d), lambda bb, i: (b, 0, 0)),
                  pl.BlockSpec((d, n), lambda bb, i: (0, 0), pipeline_mode=pl.Buffered(1))],
        out_specs=out_specs,
        out_shape=out_shape,
        compiler_params=_params("arbitrary", "arbitrary"),
        name="in_proj",
    )(x, g.reshape(1, d), mods, mods, w)


MLP_CHUNK = 1024


def _mix_mlp_kernel(x_ref, a_ref, b_ref, bc_ref, wa_ref, wb_ref, g_ref, ml_ref, mc_ref, w1_ref, w2_ref, fg_ref,
                    o_ref, acc_ref, *, n_lat_tiles, a_tmaj, b_split, final):
    i = pl.program_id(1)
    is_ctx = i >= n_lat_tiles
    m = jnp.where(is_ctx, mc_ref[0], ml_ref[0])
    a = a_ref[...] if a_tmaj else a_ref[0]
    bmix = jnp.where(is_ctx, bc_ref[0], b_ref[0]) if b_split else b_ref[0]
    mix = _dot(a.astype(BF16), wa_ref[...]) + _dot(bmix.astype(BF16), wb_ref[...])
    x = x_ref[0] + m[2:3] * mix
    h = _norm_mod(x, g_ref[...], ml_ref[0], mc_ref[0], is_ctx, 3).astype(BF16)
    for k in range(w1_ref.shape[1] // MLP_CHUNK):
        ks = slice(k * MLP_CHUNK, (k + 1) * MLP_CHUNK)
        hid = jnp.maximum(_dot(h, w1_ref[:, ks]), 0.0)
        part = _dot((hid * hid).astype(BF16), w2_ref[ks, :])
        if k == 0:
            acc_ref[...] = part
        else:
            acc_ref[...] += part
    y = x + m[5:6] * acc_ref[...]
    if final:
        ms = jnp.mean(y * y, axis=-1, keepdims=True)
        y = y * lax.rsqrt(ms + EPS) * fg_ref[...]
    o_ref[0] = y


def mix_mlp(x, a, bm, bm_ctx, w_out, g, mods, w1, w2, final_g, n_lat, n_rows, a_tmaj, final):
    b, t, d = x.shape
    wd = bm.shape[-1]
    hid = w1.shape[1]
    tm = TOKEN_TILE
    nl = n_lat // tm
    once = pl.Buffered(1)
    if a_tmaj:
        a_spec = pl.BlockSpec((tm, wd), lambda bb, i: (i, bb))
    else:
        a_spec = pl.BlockSpec((1, tm, wd), lambda bb, i: (bb, i, 0))
    b_split = bm_ctx is not None
    if b_split:
        b_spec = pl.BlockSpec((1, tm, wd), lambda bb, i: (bb, jnp.minimum(i, nl - 1), 0))
        bc_spec = pl.BlockSpec((1, tm, wd), lambda bb, i: (bb, jnp.maximum(i - nl, 0), 0))
    else:
        b_spec = pl.BlockSpec((1, tm, wd), lambda bb, i: (bb, i, 0))
        bc_spec = pl.BlockSpec((1, tm, wd), lambda bb, i: (0, 0, 0))
        bm_ctx = bm
    return pl.pallas_call(
        functools.partial(_mix_mlp_kernel, n_lat_tiles=nl, a_tmaj=a_tmaj, b_split=b_split, final=final),
        grid=(b, n_rows // tm),
        in_specs=[pl.BlockSpec((1, tm, d), lambda bb, i: (bb, i, 0)),
                  a_spec,
                  b_spec,
                  bc_spec,
                  pl.BlockSpec((wd, d), lambda bb, i: (0, 0), pipeline_mode=once),
                  pl.BlockSpec((wd, d), lambda bb, i: (1, 0), pipeline_mode=once),
                  pl.BlockSpec((1, d), lambda bb, i: (0, 0)),
                  pl.BlockSpec((1, 6, d), lambda bb, i: (bb, 0, 0)),
                  pl.BlockSpec((1, 6, d), lambda bb, i: (b, 0, 0)),
                  pl.BlockSpec((d, hid), lambda bb, i: (0, 0), pipeline_mode=once),
                  pl.BlockSpec((hid, d), lambda bb, i: (0, 0), pipeline_mode=once),
                  pl.BlockSpec((1, d), lambda bb, i: (0, 0))],
        out_specs=pl.BlockSpec((1, tm, d), lambda bb, i: (bb, i, 0)),
        out_shape=jax.ShapeDtypeStruct((b, n_rows, d), F32),
        scratch_shapes=[pltpu.VMEM((tm, d), F32)],
        compiler_params=_params("arbitrary", "arbitrary"),
        name="mix_mlp",
    )(x, a, bm, bm_ctx, w_out, w_out, g.reshape(1, d), mods, mods, w1, w2, final_g.reshape(1, d))


def _conv3(x, w, first, last):
    rows = x.shape[0]
    xm = jnp.where(first, 0.0, pltpu.roll(x, 1, 0))
    xp = jnp.where(last, 0.0, pltpu.roll(x, rows - 1, 0))
    return xm * w[0:1] + x * w[1:2] + xp * w[2:3]


def _shortconv_kernel(x_ref, w_ref, o_ref, *, n_lat):
    x = x_ref[0]
    t = x.shape[0]
    row = lax.broadcasted_iota(jnp.int32, (t, 1), 0)
    first = (row == 0) | (row == n_lat)
    last = (row == n_lat - 1) | (row == t - 1)
    o_ref[0, 0] = _conv3(x, w_ref[...], first, last)


def short_conv(p, w, n_lat):
    b, t, _ = p.shape
    c = w.shape[1]
    cb = GROUP_W
    return pl.pallas_call(
        functools.partial(_shortconv_kernel, n_lat=n_lat),
        grid=(b, c // cb),
        in_specs=[pl.BlockSpec((1, t, cb), lambda bb, j: (bb, 0, j)),
                  pl.BlockSpec((SHORT_K, cb), lambda bb, j: (0, j))],
        out_specs=pl.BlockSpec((1, 1, t, cb), lambda bb, j: (j, bb, 0, 0)),
        out_shape=jax.ShapeDtypeStruct((c // cb, b, t, cb), F32),
        compiler_params=_params("arbitrary", "arbitrary"),
        name="short_conv",
    )(p, w)


S5_KB = LANE
S5_NB = GROUP_W // S5_KB
S5_SB = S5_W // S5_NB


def _s5_kernel(*refs, tc, nb, reverse, finish):
    if finish:
        (u_ref, wbr_ref, wbi_ref, lr_ref, li_ref, wcr_ref, wci_ref, yf_ref, d_ref, gw_ref, gb_ref,
         o_ref, hr_s, hi_s, sr_s, si_s) = refs
    else:
        (u_ref, wbr_ref, wbi_ref, lr_ref, li_ref, wcr_ref, wci_ref,
         o_ref, hr_s, hi_s, sr_s, si_s) = refs

    @pl.when(pl.program_id(0) == 0)
    def _():
        sr_s[...] = jnp.zeros_like(sr_s)
        si_s[...] = jnp.zeros_like(si_s)

    u = u_ref[...]
    ub = u.astype(BF16)
    for j in range(S5_NB):
        uj = ub[:, j * S5_KB:(j + 1) * S5_KB]
        hr_s[:, j * S5_SB:(j + 1) * S5_SB] = _dot(uj, wbr_ref[j])
        hi_s[:, j * S5_SB:(j + 1) * S5_SB] = _dot(uj, wbi_ref[j])

    for s in range(S5_NB):
        sl = slice(s * S5_SB, (s + 1) * S5_SB)
        lr = jnp.broadcast_to(lr_ref[:, sl], (nb, S5_SB))
        li = jnp.broadcast_to(li_ref[:, sl], (nb, S5_SB))

        def body(k, carry, sl=sl, lr=lr, li=li):
            hr, hi = carry
            t = (tc - 1 - k) if reverse else k
            r0 = pl.multiple_of(t * nb, nb)
            nr = lr * hr - li * hi + hr_s[pl.ds(r0, nb), sl]
            ni = lr * hi + li * hr + hi_s[pl.ds(r0, nb), sl]
            hr_s[pl.ds(r0, nb), sl] = nr
            hi_s[pl.ds(r0, nb), sl] = ni
            return nr, ni

        hr, hi = lax.fori_loop(0, tc, body, (sr_s[:, sl], si_s[:, sl]), unroll=4)
        sr_s[:, sl] = hr
        si_s[:, sl] = hi

    for j in range(S5_NB):
        sl = slice(j * S5_SB, (j + 1) * S5_SB)
        co = slice(j * S5_KB, (j + 1) * S5_KB)
        y = _dot(hr_s[:, sl].astype(BF16), wcr_ref[j]) + _dot(hi_s[:, sl].astype(BF16), wci_ref[j])
        if finish:
            o_ref[:, co] = y + yf_ref[:, co] + u[:, co] * d_ref[:, co]
        else:
            o_ref[:, co] = y

    if finish:
        y = o_ref[...]
        y = 0.5 * y * (1.0 + lax.erf(y * (2.0 ** -0.5)))
        z = _dot(y.astype(BF16), gw_ref[...]) + gb_ref[...]
        o_ref[...] = y * jax.nn.sigmoid(z)


def s5_scan_call(u2, tabs, n_lat_steps, n_steps, nb, reverse, extra):
    tc = 32
    rows = tc * nb
    n_chunks = n_steps // tc
    n_lat_chunks = n_lat_steps // tc
    n_ctx_chunks = n_chunks - n_lat_chunks
    if reverse:
        def cmap(c):
            return (n_chunks - 1 - c, 0)
    else:
        def cmap(c):
            return (jnp.where(c < n_ctx_chunks, n_lat_chunks + c, c - n_ctx_chunks), 0)
    wbr, wbi, lr, li, wcr, wci = tabs
    full3 = lambda a: pl.BlockSpec(a.shape, lambda c: (0, 0, 0))
    full2 = lambda a: pl.BlockSpec(a.shape, lambda c: (0, 0))
    in_specs = [pl.BlockSpec((rows, GROUP_W), cmap), full3(wbr), full3(wbi), full2(lr), full2(li),
                full3(wcr), full3(wci)]
    args = [u2, wbr, wbi, lr, li, wcr, wci]
    finish = extra is not None
    if finish:
        yf, dsk, gw, gb = extra
        in_specs += [pl.BlockSpec((rows, GROUP_W), cmap), full2(dsk), full2(gw), full2(gb)]
        args += [yf, dsk, gw, gb]
    return pl.pallas_call(
        functools.partial(_s5_kernel, tc=tc, nb=nb, reverse=reverse, finish=finish),
        grid=(n_chunks,),
        in_specs=in_specs,
        out_specs=pl.BlockSpec((rows, GROUP_W), cmap),
        out_shape=jax.ShapeDtypeStruct(u2.shape, F32),
        scratch_shapes=[pltpu.VMEM((rows, S5_W), F32), pltpu.VMEM((rows, S5_W), F32),
                        pltpu.VMEM((nb, S5_W), F32), pltpu.VMEM((nb, S5_W), F32)],
        compiler_params=_params("arbitrary"),
        name="s5_bwd_finish" if finish else "s5_fwd",
    )(*args)


def _block_diag(x):
    nblk, g, r, c = x.shape
    eye = jnp.eye(g, dtype=x.dtype)
    return jnp.einsum('jgrc,gh->jgrhc', x, eye).reshape(nblk, g * r, g * c)


def s5_tables(lam_re, lam_im, log_dt, b_re, b_im, c_re, c_im):
    lam_re = jnp.minimum(lam_re.astype(F32), -1e-4)
    lam_im = lam_im.astype(F32)
    dt = jnp.exp(log_dt.astype(F32))[:, None]
    mag = jnp.exp(lam_re * dt)
    lb_re = mag * jnp.cos(lam_im * dt)
    lb_im = mag * jnp.sin(lam_im * dt)
    den = lam_re * lam_re + lam_im * lam_im
    f_re = ((lb_re - 1.0) * lam_re + lb_im * lam_im) / den
    f_im = (lb_im * lam_re - (lb_re - 1.0) * lam_im) / den
    b_re = b_re.astype(F32)
    b_im = b_im.astype(F32)
    bb_re = f_re[..., None] * b_re - f_im[..., None] * b_im
    bb_im = f_re[..., None] * b_im + f_im[..., None] * b_re
    gpb = S5_KB // S5_GROUP
    to_b = lambda x: _block_diag(
        jnp.swapaxes(x, 1, 2).reshape(S5_NB, gpb, S5_GROUP, S5_STATE)).astype(BF16)
    to_c = lambda x: _block_diag(
        jnp.swapaxes(x.astype(F32), 1, 2).reshape(S5_NB, gpb, S5_STATE, S5_GROUP)).astype(BF16)
    return (to_b(bb_re), to_b(bb_im), lb_re.reshape(1, S5_W), lb_im.reshape(1, S5_W),
            to_c(c_re), to_c(-c_im.astype(F32)))


def dft_tables(n):
    nn = 2 * n
    f = jnp.arange(n, dtype=jnp.int32)[:, None]
    t = jnp.arange(n, dtype=jnp.int32)[None, :]
    q = 1 << (int(math.log2(n)) // 2)

    def factor(tt):
        ang = ((f * tt[None, :]) % nn).astype(F32) * (2.0 * math.pi / nn)
        return jnp.cos(ang), jnp.sin(ang)

    c1, s1 = factor(jnp.arange(n // q, dtype=jnp.int32) * q)
    c0, s0 = factor(jnp.arange(q, dtype=jnp.int32))
    cos = (c1[:, :, None] * c0[:, None, :] - s1[:, :, None] * s0[:, None, :]).reshape(n, n)
    sin = (s1[:, :, None] * c0[:, None, :] + c1[:, :, None] * s0[:, None, :]).reshape(n, n)
    nyq = jnp.where(t % 2 == 0, 1.0, -1.0).astype(F32)
    msin = jnp.where(f == 0, nyq, -sin)
    fwd = jnp.concatenate([cos, msin], axis=0)
    wf = jnp.where(jnp.arange(n) == 0, 1.0 / nn, 2.0 / nn).astype(F32)
    return fwd, wf


def _filtdft_kernel(fh_ref, fl_ref, hh_ref, hl_ref, o_ref):
    fh = fh_ref[...]
    o_ref[...] = _dot(fh, hh_ref[0]) + _dot(fh, hl_ref[0]) + _dot(fl_ref[...], hh_ref[0])


def filter_dft(fwd_hi, fwd_lo, h_sum, h_dif):
    n2, n = fwd_hi.shape
    c = h_sum.shape[1]
    tr = min(256, n)
    tcn = 512
    nrb = n // tr
    hh, hl = _split_bf16(jnp.stack([h_sum, h_dif]))
    frow = lambda i, j: (jnp.where(i == 2 * nrb, nrb, i), 0)
    hsel = lambda i, j: (jnp.where((i >= nrb) & (i < 2 * nrb), 1, 0), 0, j)
    return pl.pallas_call(
        _filtdft_kernel,
        grid=(2 * nrb + 1, c // tcn),
        in_specs=[pl.BlockSpec((tr, n), frow),
                  pl.BlockSpec((tr, n), frow),
                  pl.BlockSpec((1, n, tcn), hsel),
                  pl.BlockSpec((1, n, tcn), hsel)],
        out_specs=pl.BlockSpec((tr, tcn), lambda i, j: (i, j)),
        out_shape=jax.ShapeDtypeStruct((n2 + tr, c), F32),
        compiler_params=_params("arbitrary", "arbitrary"),
        name="filter_dft",
    )(fwd_hi, fwd_lo, hh, hl)


def hyena_filter_response(n, fwd_hi, fwd_lo, wf, f_w1, f_b1, f_w2, f_b2, f_w3, f_freq, log_decay):
    t = jnp.linspace(0.0, 1.0, n, dtype=F32)[:, None]
    w = 2.0 * math.pi * jnp.arange(n, dtype=F32)[:, None] / n
    bands = jnp.linspace(1e-4, HY_BANDS - 1, HY_BANDS, dtype=F32)[None, :]
    z = jnp.concatenate([t, jnp.cos(bands * w), -jnp.sin(bands * w)], axis=-1)
    freq = f_freq.astype(F32)
    hid = jnp.sin(freq[0] * (jnp.dot(z, f_w1.astype(F32), precision=HIGHEST) + f_b1.astype(F32)))
    hid = jnp.sin(freq[1] * (jnp.dot(hid, f_w2.astype(F32), precision=HIGHEST) + f_b2.astype(F32)))
    h = jnp.dot(hid, f_w3.astype(F32), precision=HIGHEST).reshape(n, 2, HY_ORDER, GROUP_W)
    h = h * jnp.exp(-t[:, :, None, None] * jnp.exp(log_decay.astype(F32)))
    cw = HY_ORDER * GROUP_W
    h_fwd = h[:, 0].reshape(n, cw)
    h_bwd = h[:, 1].reshape(n, cw).at[0].set(0.0)
    resp = filter_dft(fwd_hi, fwd_lo, h_fwd + h_bwd, h_fwd - h_bwd)
    k_re = resp[:n]
    first = (jnp.arange(n) == 0)[:, None]
    k_im = jnp.where(first, resp[2 * n:2 * n + 1], resp[n:2 * n])
    k_re = (k_re * wf[:, None]).reshape(n, HY_ORDER, GROUP_W).transpose(1, 0, 2)
    k_im = (k_im * wf[:, None]).reshape(n, HY_ORDER, GROUP_W).transpose(1, 0, 2)
    return k_re, k_im


def _hyena_kernel(z0_ref, x1_ref, x2_ref, cw_ref, cb_ref, fw_ref, inv_ref, kr_ref, ki_ref, bias_ref,
                  o_ref, zb_s, zf_s, acc_s):
    o = pl.program_id(1)
    f = pl.program_id(2)
    nf = pl.num_programs(2)
    c = GROUP_W

    def stream(ref, s):
        x = ref[0]
        row = lax.broadcasted_iota(jnp.int32, (x.shape[0], 1), 0)
        cols = slice(s * c, (s + 1) * c)
        return _conv3(x, cw_ref[:, cols], row == 0, row == x.shape[0] - 1) + cb_ref[:, cols]

    @pl.when((o == 0) & (f == 0))
    def _():
        z0 = stream(z0_ref, 0)
        zf_s[...] = z0
        zb_s[...] = z0.astype(BF16)

    @pl.when(f == 0)
    def _():
        acc_s[...] = jnp.zeros_like(acc_s)

    xf = _dot(fw_ref[0], zb_s[...])
    fb = xf.shape[0] // 2
    xr = xf[:fb]
    xi = xf[fb:]
    kr = kr_ref[0]
    ki = ki_ref[0]
    row = lax.broadcasted_iota(jnp.int32, (fb, 1), 0)
    packed = (row == 0) & (f == 0)
    yr = jnp.where(packed, xr * kr, xr * kr - xi * ki)
    yi = jnp.where(packed, xi * ki, xr * ki + xi * kr)
    yf = jnp.concatenate([yr, yi], axis=0).astype(BF16)
    acc_s[...] += _dot(inv_ref[0], yf)

    @pl.when(f == nf - 1)
    def _():
        bias = bias_ref[...]

        @pl.when(o == 0)
        def _():
            z1 = stream(x1_ref, 1) * (acc_s[...] + zf_s[...] * bias[0:1])
            zf_s[...] = z1
            zb_s[...] = z1.astype(BF16)

        @pl.when(o == 1)
        def _():
            o_ref[0] = stream(x2_ref, 2) * (acc_s[...] + zf_s[...] * bias[1:2])


HY_FREQ_BLOCK = 256


def dft_blocks(fwd_hi):
    n = fwd_hi.shape[1]
    fb = min(HY_FREQ_BLOCK, n)
    blocks = fwd_hi.reshape(2, n // fb, fb, n).transpose(1, 0, 2, 3).reshape(n // fb, 2 * fb, n)
    return blocks, blocks.transpose(0, 2, 1)


def hyena_call(p, n, row_blk, conv_w, conv_b, fwd_blk, inv_blk, k_re, k_im, bias):
    b = p.shape[0]
    c = GROUP_W
    nf, fb2, _ = fwd_blk.shape
    fb = fb2 // 2
    once = pl.Buffered(1)
    in_specs = [pl.BlockSpec((1, n, c), lambda bb, o, f: (bb, row_blk, 0), pipeline_mode=once),
                pl.BlockSpec((1, n, c), lambda bb, o, f: (bb, row_blk, 1), pipeline_mode=once),
                pl.BlockSpec((1, n, c), lambda bb, o, f: (bb, row_blk, 2), pipeline_mode=once),
                pl.BlockSpec((SHORT_K, 3 * c), lambda bb, o, f: (0, 0)),
                pl.BlockSpec((1, 3 * c), lambda bb, o, f: (0, 0)),
                pl.BlockSpec((1, fb2, n), lambda bb, o, f: (f, 0, 0)),
                pl.BlockSpec((1, n, fb2), lambda bb, o, f: (f, 0, 0)),
                pl.BlockSpec((1, fb, c), lambda bb, o, f: (o, f, 0)),
                pl.BlockSpec((1, fb, c), lambda bb, o, f: (o, f, 0)),
                pl.BlockSpec((HY_ORDER, c), lambda bb, o, f: (0, 0))]
    return pl.pallas_call(
        _hyena_kernel,
        grid=(b, HY_ORDER, nf),
        in_specs=in_specs,
        out_specs=pl.BlockSpec((1, n, c), lambda bb, o, f: (bb, 0, 0)),
        out_shape=jax.ShapeDtypeStruct((b, n, c), F32),
        scratch_shapes=[pltpu.VMEM((n, c), BF16), pltpu.VMEM((n, c), F32), pltpu.VMEM((n, c), F32)],
        compiler_params=_params("arbitrary", "arbitrary", "arbitrary"),
        name="hyena",
    )(p, p, p, conv_w, conv_b.reshape(1, 3 * c), fwd_blk, inv_blk, k_re, k_im, bias)


def _rwkv_lora_kernel(p_ref, ww_ref, wa_ref, wg_ref, w0_ref, a0_ref, wp_ref, ap_ref, g_ref):
    x = p_ref[0]
    wa_in = x[:, :LANE]
    th = jnp.tanh(wa_in).astype(BF16)
    lin = wa_in.astype(BF16)
    w0 = w0_ref[...]
    a0 = a0_ref[...]
    for d in range(2):
        wl = -jax.nn.softplus(-(w0[d:d + 1] + _dot(th, ww_ref[d]))) - 0.5
        wp_ref[d, 0] = jnp.exp(-jnp.exp(wl))
        ap_ref[d, 0] = jax.nn.sigmoid(a0[d:d + 1] + _dot(lin, wa_ref[d]))
    g_ref[0] = _dot(jax.nn.sigmoid(x[:, LANE:]).astype(BF16), wg_ref[...])


def rwkv_lora(p, w_up, a_up, g_up, w0, a0):
    b, t, _ = p.shape
    c = GROUP_W
    tm = TOKEN_TILE
    zeros = jnp.zeros((2, 64, c), F32)
    ww = jnp.concatenate([w_up.astype(F32), zeros], axis=1).astype(BF16)
    wa = jnp.concatenate([zeros, a_up.astype(F32)], axis=1).astype(BF16)
    both = jax.ShapeDtypeStruct((2, b, t, c), F32)
    both_spec = pl.BlockSpec((2, 1, tm, c), lambda bb, i: (0, bb, i, 0))
    return pl.pallas_call(
        _rwkv_lora_kernel,
        grid=(b, t // tm),
        in_specs=[pl.BlockSpec((1, tm, RW_LORA_W), lambda bb, i: (bb, i, 3 * c // RW_LORA_W)),
                  pl.BlockSpec((2, LANE, c), lambda bb, i: (0, 0, 0)),
                  pl.BlockSpec((2, LANE, c), lambda bb, i: (0, 0, 0)),
                  pl.BlockSpec((LANE, c), lambda bb, i: (0, 0)),
                  pl.BlockSpec((2, c), lambda bb, i: (0, 0)),
                  pl.BlockSpec((2, c), lambda bb, i: (0, 0))],
        out_specs=[both_spec, both_spec, pl.BlockSpec((1, tm, c), lambda bb, i: (bb, i, 0))],
        out_shape=[both, both, jax.ShapeDtypeStruct((b, t, c), F32)],
        compiler_params=_params("arbitrary", "arbitrary"),
        name="rwkv_lora",
    )(p, ww, wa, g_up.astype(BF16), w0, a0)


SCAN_UNROLL = 8


def _rwkv_scan_kernel(wpf_ref, wpb_ref, apf_ref, apb_ref, kf_ref, kb_ref, vf_ref, vb_ref, rf_ref, rb_ref,
                      kkc_ref, kac_ref, yf_ref, yb_ref, s_ref, w_s, kk_s, b_s, ke_s, r_s, *, tb):
    n = RW_HEAD

    @pl.when(pl.program_id(0) == 0)
    def _():
        s_ref[...] = jnp.zeros_like(s_ref)

    def both(f_ref, b_ref, t):
        return jnp.concatenate([f_ref[0, t], b_ref[0, tb - 1 - t]], axis=-1)

    def prepare(t, slot):
        k = both(kf_ref, kb_ref, t)
        a = both(apf_ref, apb_ref, t)
        kk = k * kkc_ref[...]
        nrm = jnp.sqrt(jnp.sum(kk * kk, axis=0, keepdims=True))
        kk = kk / jnp.maximum(nrm, 1e-12)
        w_s[slot] = both(wpf_ref, wpb_ref, t)
        kk_s[slot] = kk
        b_s[slot] = kk * a
        ke_s[slot] = k * (1.0 + (a - 1.0) * kac_ref[...])
        r_s[slot] = both(rf_ref, rb_ref, t)

    prepare(0, 0)
    sa0 = jnp.zeros((n, LANE), F32)
    for j in range(n):
        sa0 = sa0 + s_ref[j] * kk_s[0, j:j + 1, :]

    def step(t, sa):
        slot = t % 2
        nxt = 1 - slot
        prepare(jnp.minimum(t + 1, tb - 1), nxt)
        v = both(vf_ref, vb_ref, t)

        def keys(j, carry):
            y, sa_next = carry
            row = pl.ds(j, 1)
            s = s_ref[j] * w_s[slot, row, :] - sa * b_s[slot, row, :] + v * ke_s[slot, row, :]
            s_ref[j] = s
            return y + s * r_s[slot, row, :], sa_next + s * kk_s[nxt, row, :]

        zero = jnp.zeros((n, LANE), F32)
        y, sa_next = lax.fori_loop(0, n, keys, (zero, zero), unroll=SCAN_UNROLL)
        yf_ref[t] = y
        yb_ref[tb - 1 - t] = y
        return sa_next

    lax.fori_loop(0, tb, step, sa0)


def rwkv_scan(w, a, rkv, kkc, kac, n_lat):
    t = rkv.shape[1]
    tb = 32
    n = RW_HEAD
    half = LANE // 2
    n_blk = t // tb
    n_lat_blk = n_lat // tb
    n_ctx_blk = n_blk - n_lat_blk

    def fblk(c):
        return jnp.where(c < n_ctx_blk, n_lat_blk + c, c - n_ctx_blk)

    def rblk(c):
        return n_blk - 1 - c

    def pair(qf, qb):
        return [pl.BlockSpec((1, tb, n, half), lambda c: (qf, fblk(c), 0, 0)),
                pl.BlockSpec((1, tb, n, half), lambda c: (qb, rblk(c), 0, 0))]

    cst = pl.BlockSpec((n, LANE), lambda c: (0, 0))
    buf = pltpu.VMEM((2, n, LANE), F32)
    out = jax.ShapeDtypeStruct((t, n, LANE), F32)
    return pl.pallas_call(
        functools.partial(_rwkv_scan_kernel, tb=tb),
        grid=(n_blk,),
        in_specs=pair(0, 1) + pair(0, 1) + pair(1, 1) + pair(2, 2) + pair(0, 0) + [cst] * 2,
        out_specs=[pl.BlockSpec((tb, n, LANE), lambda c: (fblk(c), 0, 0)),
                   pl.BlockSpec((tb, n, LANE), lambda c: (rblk(c), 0, 0))],
        out_shape=[out, out],
        scratch_shapes=[pltpu.VMEM((n, n, LANE), F32), buf, buf, buf, buf, buf],
        compiler_params=_params("arbitrary"),
        name="rwkv_scan",
    )(w, w, a, a, rkv, rkv, rkv, rkv, rkv, rkv, kkc, kac)


def _rwkv_post_kernel(y_ref, r_ref, k_ref, v_ref, ap_ref, g_ref, m_ref, ka_ref, rk_ref, lg_ref, lb_ref,
                      o_ref):
    m = m_ref[...]

    def head_mean(x):
        hi, lo = _split_bf16(x)
        return _dot(hi, m) + _dot(lo, m)

    y = y_ref[0]
    d = y - head_mean(y)
    var = head_mean(d * d)
    yn = d * lax.rsqrt(var + RW_LN_EPS) * lg_ref[...] + lb_ref[...]
    a_sum = ap_ref[0, 0] + ap_ref[1, 0]
    k_sum = k_ref[0, 0] * (2.0 + (a_sum - 2.0) * ka_ref[...])
    bonus = head_mean(r_ref[0, 0] * k_sum * rk_ref[...]) * float(RW_HEAD) * v_ref[0, 0]
    o_ref[0] = (yn + bonus) * g_ref[0]


def rwkv_post(y, rkv, ap, gate, k_a, r_k, ln_g, ln_b, n_rows):
    b, t, c = y.shape
    tm = TOKEN_TILE
    hm = jnp.kron(jnp.eye(RW_HEADS, dtype=F32), jnp.full((RW_HEAD, RW_HEAD), 1.0 / RW_HEAD, F32)).astype(BF16)
    tok = pl.BlockSpec((1, tm, c), lambda bb, i: (bb, i, 0))
    vec = pl.BlockSpec((1, c), lambda bb, i: (0, 0))
    rkv_spec = lambda q: pl.BlockSpec((1, 1, tm, c), lambda bb, i: (q, bb, i, 0))
    return pl.pallas_call(
        _rwkv_post_kernel,
        grid=(b, n_rows // tm),
        in_specs=[tok, rkv_spec(0), rkv_spec(1), rkv_spec(2),
                  pl.BlockSpec((2, 1, tm, c), lambda bb, i: (0, bb, i, 0)), tok,
                  pl.BlockSpec((c, c), lambda bb, i: (0, 0)), vec, vec, vec, vec],
        out_specs=tok,
        out_shape=jax.ShapeDtypeStruct((b, n_rows, c), F32),
        compiler_params=_params("arbitrary", "arbitrary"),
        name="rwkv_post",
    )(y, rkv, rkv, rkv, ap, gate, hm, k_a.reshape(1, c), r_k.reshape(1, c),
      ln_g.reshape(1, c), ln_b.reshape(1, c))


def _to_scan_layout(a):
    s, b, t, _ = a.shape
    x = a.reshape(s, b, t, RW_HEADS, RW_HEAD).transpose(0, 2, 4, 1, 3)
    return x.reshape(s, t, RW_HEAD, b * RW_HEADS)


def _from_scan_layout(yf, yb, b):
    t = yf.shape[0]
    half = b * RW_HEADS
    y = yf[..., :half] + yb[..., half:]
    return y.reshape(t, RW_HEAD, b, RW_HEADS).transpose(2, 0, 3, 1).reshape(b, t, GROUP_W)


def _chain_const(x, b):
    return jnp.tile(x.astype(F32).reshape(RW_HEADS, RW_HEAD).T, (1, 2 * b))


def _rope(x, cos, sin):
    lane = lax.broadcasted_iota(jnp.int32, (1, LANE), 1)
    first = (lane % (2 * ROPE_FREQS)) < ROPE_FREQS
    partner = jnp.where(first, pltpu.roll(x, LANE - ROPE_FREQS, 1), pltpu.roll(x, ROPE_FREQS, 1))
    return x * cos + partner * sin


def _attn_tile(q, k, v, lam):
    lane = lax.broadcasted_iota(jnp.int32, (1, LANE), 1)
    m0 = lane < DA_HEAD
    q0 = jnp.where(m0, q, 0.0).astype(BF16)
    q1 = jnp.where(m0, 0.0, q).astype(BF16)
    dn = (((1,), (1,)), ((), ()))
    s0 = lax.dot_general(q0, k, dn, preferred_element_type=F32)
    s1 = lax.dot_general(q1, k, dn, preferred_element_type=F32)
    p0 = jnp.exp2(s0 - jnp.max(s0, axis=-1, keepdims=True))
    p1 = jnp.exp2(s1 - jnp.max(s1, axis=-1, keepdims=True))
    c0 = 1.0 / jnp.sum(p0, axis=-1, keepdims=True)
    c1 = lam / jnp.sum(p1, axis=-1, keepdims=True)
    return _dot((p0 * c0 - p1 * c1).astype(BF16), v)


def _attn_kernel(q_ref, k_ref, v_ref, cq_ref, sq_ref, ck_ref, sk_ref, lp_ref, g_ref, o_ref,
                 kr_s, vb_s, *, n_lat, n_lat_tiles, lam_init):
    i = pl.program_id(2)

    @pl.when(i == 0)
    def _():
        kr_s[...] = _rope(k_ref[0], ck_ref[...], sk_ref[...]).astype(BF16)
        vb_s[...] = v_ref[0].astype(BF16)

    lp = lp_ref[...]
    lam = (jnp.exp(jnp.sum(lp[0:1] * lp[1:2], axis=-1, keepdims=True))
           - jnp.exp(jnp.sum(lp[2:3] * lp[3:4], axis=-1, keepdims=True)) + lam_init)
    q = _rope(q_ref[0], cq_ref[...], sq_ref[...]) * (DA_SCALE * math.log2(math.e))

    def finish(o):
        on = o * lax.rsqrt(jnp.mean(o * o, axis=-1, keepdims=True) + DA_SUBLN_EPS)
        o_ref[0] = on * g_ref[...] * (1.0 - lam_init)

    @pl.when(i < n_lat_tiles)
    def _():
        finish(_attn_tile(q, kr_s[...], vb_s[...], lam))

    @pl.when(i >= n_lat_tiles)
    def _():
        finish(_attn_tile(q, kr_s[n_lat:], vb_s[n_lat:], lam))


def diff_attention(p, col0, cosf, sins, lam_p, subln_g, lam_init, n_lat, n_rows):
    b, t, _ = p.shape
    tq = TOKEN_TILE
    off = col0 // LANE
    hq = DA_HEADS
    return pl.pallas_call(
        functools.partial(_attn_kernel, n_lat=n_lat, n_lat_tiles=n_lat // tq, lam_init=lam_init),
        grid=(b, hq, n_rows // tq),
        in_specs=[pl.BlockSpec((1, tq, LANE), lambda bb, h, i: (bb, i, off + h)),
                  pl.BlockSpec((1, t, LANE), lambda bb, h, i: (bb, 0, off + hq + h)),
                  pl.BlockSpec((1, t, LANE), lambda bb, h, i: (bb, 0, off + 2 * hq + h)),
                  pl.BlockSpec((tq, LANE), lambda bb, h, i: (i, 0)),
                  pl.BlockSpec((tq, LANE), lambda bb, h, i: (i, 0)),
                  pl.BlockSpec((t, LANE), lambda bb, h, i: (0, 0)),
                  pl.BlockSpec((t, LANE), lambda bb, h, i: (0, 0)),
                  pl.BlockSpec((4, DA_HEAD), lambda bb, h, i: (0, 0)),
                  pl.BlockSpec((1, LANE), lambda bb, h, i: (0, 0))],
        out_specs=pl.BlockSpec((1, tq, LANE), lambda bb, h, i: (bb, i, h)),
        out_shape=jax.ShapeDtypeStruct((b, n_rows, hq * DA_V), F32),
        scratch_shapes=[pltpu.VMEM((t, LANE), BF16), pltpu.VMEM((t, LANE), BF16)],
        compiler_params=_params("arbitrary", "arbitrary", "arbitrary"),
        name="diff_attention",
    )(p, p, p, cosf, sins, cosf, sins, lam_p, subln_g.reshape(1, DA_V))


def rope_tables(n_lat, n_ctx):
    rows = n_lat // GRID_W
    row = jnp.repeat(jnp.arange(rows, dtype=F32), GRID_W)
    col = jnp.tile(jnp.arange(GRID_W, dtype=F32), rows)
    inv = ROPE_BASE ** (-jnp.arange(ROPE_FREQS, dtype=F32) / ROPE_FREQS)
    ang = jnp.stack([row[:, None] * inv, col[:, None] * inv], axis=1)
    cos, sin = jnp.cos(ang), jnp.sin(ang)
    cosf = jnp.concatenate([cos, cos], axis=-1).reshape(n_lat, DA_HEAD)
    sins = jnp.concatenate([-sin, sin], axis=-1).reshape(n_lat, DA_HEAD)
    cosf = jnp.concatenate([jnp.tile(cosf, (1, 2)), jnp.ones((n_ctx, LANE), F32)], axis=0)
    sins = jnp.concatenate([jnp.tile(sins, (1, 2)), jnp.zeros((n_ctx, LANE), F32)], axis=0)
    return cosf, sins


def _even_mixers(xs, mods, keep_ctx, n_lat, norm1_g, w_in, s5p, hyp, dft):
    b, t, _ = xs.shape
    n_ctx = t - n_lat
    u_t, p_h = in_proj(xs, norm1_g, mods, w_in.astype(BF16), n_lat, GROUP_W)

    (lam_re, lam_im, log_dt, b_re, b_im, c_re, c_im, d_skip, glu_w, glu_b) = s5p
    u2 = u_t.reshape(t * b, GROUP_W)
    tabs = [s5_tables(lam_re[d], lam_im[d], log_dt[d], b_re[d], b_im[d], c_re[d], c_im[d]) for d in range(2)]
    y_f = s5_scan_call(u2, tabs[0], n_lat, t, b, False, None)
    a_t = s5_scan_call(u2, tabs[1], n_lat, t, b, True,
                       (y_f, d_skip.reshape(1, GROUP_W), glu_w.astype(BF16), glu_b.reshape(1, GROUP_W)))

    (conv_w, conv_b, f_w1, f_b1, f_w2, f_b2, f_w3, f_freq, log_decay, bias) = hyp
    fh_l, fl_l, (fw_l, inv_l), wf_l = dft[0]
    kr, ki = hyena_filter_response(n_lat, fh_l, fl_l, wf_l, f_w1, f_b1, f_w2, f_b2, f_w3, f_freq, log_decay)
    b_l = hyena_call(p_h, n_lat, 0, conv_w, conv_b, fw_l, inv_l, kr, ki, bias)
    b_c = None
    if keep_ctx:
        fh_c, fl_c, (fw_c, inv_c), wf_c = dft[1]
        kr, ki = hyena_filter_response(n_ctx, fh_c, fl_c, wf_c, f_w1, f_b1, f_w2, f_b2, f_w3, f_freq,
                                       log_decay)
        b_c = hyena_call(p_h, n_ctx, n_lat // n_ctx, conv_w, conv_b, fw_c, inv_c, kr, ki, bias)
    return a_t.reshape(t, b * GROUP_W), b_l, b_c


def _odd_mixers(xs, mods, keep_ctx, n_lat, lam_init, norm1_g, w_in, rwp, dap, rope):
    b, t, _ = xs.shape
    n_rows = t if keep_ctx else n_lat
    (p,) = in_proj(xs, norm1_g, mods, w_in.astype(BF16), n_lat, 0)

    (conv_w, w0, w_up, a0, a_up, g_up, k_k, k_a, r_k, ln_g, ln_b) = rwp
    rkv = short_conv(p, conv_w, n_lat)
    decay, a_lr, gate = rwkv_lora(p, w_up, a_up, g_up, w0, a0)
    yf, yb = rwkv_scan(_to_scan_layout(decay), _to_scan_layout(a_lr), _to_scan_layout(rkv),
                       _chain_const(k_k, b), _chain_const(k_a, b), n_lat)
    a_m = rwkv_post(_from_scan_layout(yf, yb, b), rkv, a_lr, gate, k_a, r_k, ln_g, ln_b, n_rows)

    lam_p, subln_g = dap
    b_m = diff_attention(p, RW_IN, rope[0], rope[1], lam_p, subln_g, lam_init, n_lat, n_rows)
    return a_m, b_m, None


def kernel(x, c, ctx, c_ctx, ada_w, ada_b, norm1_g, norm2_g, mlp_w1, mlp_w2, final_g, ev_w_in, ev_w_out, s5_lam_re, s5_lam_im, s5_log_dt, s5_b_re, s5_b_im, s5_c_re, s5_c_im, s5_d, s5_glu_w, s5_glu_b, hy_conv_w, hy_conv_b, hy_f_w1, hy_f_b1, hy_f_w2, hy_f_b2, hy_f_w3, hy_f_freq, hy_log_decay, hy_bias, od_w_in, od_w_out, rw_conv_w, rw_w0, rw_w_up, rw_a0, rw_a_up, rw_g_up, rw_k_k, rw_k_a, rw_r_k, rw_ln_g, rw_ln_b, da_lam, da_subln_g):
    b, n_lat, d = x.shape
    n_ctx = ctx.shape[1]
    assert d == D_MODEL and b == SUBLANE
    assert n_lat % TOKEN_TILE == 0 and n_ctx % TOKEN_TILE == 0 and n_lat % n_ctx == 0
    xs = jnp.concatenate([x.astype(F32), ctx.astype(F32)], axis=1)

    c_rows = jnp.zeros((16, d), F32).at[:b].set(c.astype(F32)).at[b].set(c_ctx.astype(F32))
    depth = ada_w.shape[0]
    mods_all = ada_mods(c_rows, ada_w, ada_b).reshape(depth, 16, 6, d)

    rope = rope_tables(n_lat, n_ctx)
    dft = []
    for n in (n_lat, n_ctx):
        fwd, wf = dft_tables(n)
        hi, lo = _split_bf16(fwd)
        dft.append((hi, lo, dft_blocks(hi), wf))

    for l in range(depth):
        keep_ctx = l < depth - 1
        i = l // 2
        mods = mods_all[l]
        if l % 2 == 0:
            s5p = (s5_lam_re[i], s5_lam_im[i], s5_log_dt[i], s5_b_re[i], s5_b_im[i], s5_c_re[i], s5_c_im[i],
                   s5_d[i], s5_glu_w[i], s5_glu_b[i])
            hyp = (hy_conv_w[i], hy_conv_b[i], hy_f_w1[i], hy_f_b1[i], hy_f_w2[i], hy_f_b2[i], hy_f_w3[i],
                   hy_f_freq[i], hy_log_decay[i], hy_bias[i])
            a_m, b_m, b_c = _even_mixers(xs, mods, keep_ctx, n_lat, norm1_g[l], ev_w_in[i], s5p, hyp, dft)
            w_out = ev_w_out[i]
        else:
            rwp = (rw_conv_w[i], rw_w0[i], rw_w_up[i], rw_a0[i], rw_a_up[i], rw_g_up[i], rw_k_k[i],
                   rw_k_a[i], rw_r_k[i], rw_ln_g[i], rw_ln_b[i])
            lam_init = 0.8 - 0.6 * math.exp(-0.3 * l)
            a_m, b_m, b_c = _odd_mixers(xs, mods, keep_ctx, n_lat, lam_init, norm1_g[l], od_w_in[i], rwp,
                                        (da_lam[i], da_subln_g[i]), rope)
            w_out = od_w_out[i]
        n_rows = n_lat + n_ctx if keep_ctx else n_lat
        xs = mix_mlp(xs, a_m, b_m, b_c, w_out.astype(BF16), norm2_g[l], mods, mlp_w1[l].astype(BF16),
                     mlp_w2[l].astype(BF16), final_g, n_lat, n_rows, l % 2 == 0, l == depth - 1)
    return xs
```

```python
import functools
import math

import jax
import jax.numpy as jnp
from jax import lax
from jax.experimental import pallas as pl
from jax.experimental.pallas import tpu as pltpu

F32 = jnp.float32
BF16 = jnp.bfloat16
HIGHEST = lax.Precision.HIGHEST

D_MODEL = 1024
DEPTH = 4
GRID_W = 64
MLP_HIDDEN = 4 * D_MODEL
GROUP_W = D_MODEL // 2
EPS = 1e-6

S5_GROUP = 16
S5_GROUPS = GROUP_W // S5_GROUP
S5_STATE = 64
S5_W = S5_GROUPS * S5_STATE

HY_ORDER = 2
HY_EMB = 33
HY_BANDS = (HY_EMB - 1) // 2
SHORT_K = 3

RW_HEAD = 64
RW_HEADS = GROUP_W // RW_HEAD
RW_LORA_W = 256
RW_LN_EPS = 64e-5
RW_IN = 3 * GROUP_W + RW_LORA_W

DA_HEADS = 4
DA_HEAD = 64
DA_V = 2 * DA_HEAD
DA_SCALE = DA_HEAD ** -0.5
DA_SUBLN_EPS = 1e-5
ROPE_BASE = 10000.0
ROPE_FREQS = DA_HEAD // 4

LANE = 128
SUBLANE = 8
TOKEN_TILE = 256
VMEM_LIMIT = 48 * 1024 * 1024


def _params(*sem):
    return pltpu.CompilerParams(dimension_semantics=sem, vmem_limit_bytes=VMEM_LIMIT)


def _split_bf16(x):
    hi = x.astype(BF16)
    lo = (x - hi.astype(F32)).astype(BF16)
    return hi, lo


def _dot(a, b):
    return jnp.dot(a, b, preferred_element_type=F32)


def _ada_kernel(c_ref, w_ref, b_ref, o_ref):
    c = c_ref[...]
    s = c * jax.nn.sigmoid(c)
    o_ref[0] = jnp.dot(s, w_ref[0], preferred_element_type=F32, precision=HIGHEST) + b_ref[0]


def ada_mods(c_rows, ada_w, ada_b):
    depth, d, n = ada_w.shape
    tn = 1024
    return pl.pallas_call(
        _ada_kernel,
        grid=(depth, n // tn),
        in_specs=[pl.BlockSpec((16, d), lambda l, j: (0, 0)),
                  pl.BlockSpec((1, d, tn), lambda l, j: (l, 0, j)),
                  pl.BlockSpec((1, 1, tn), lambda l, j: (l, 0, j))],
        out_specs=pl.BlockSpec((1, 16, tn), lambda l, j: (l, 0, j)),
        out_shape=jax.ShapeDtypeStruct((depth, 16, n), F32),
        compiler_params=_params("arbitrary", "arbitrary"),
        name="ada_mods",
    )(c_rows, ada_w, ada_b.reshape(depth, 1, n))


def _norm_mod(x, g, ml, mc, is_ctx, k):
    ms = jnp.mean(x * x, axis=-1, keepdims=True)
    xn = x * lax.rsqrt(ms + EPS) * g
    m = jnp.where(is_ctx, mc, ml)
    return xn * (1.0 + m[k + 1:k + 2]) + m[k:k + 1]


def _inproj_kernel(x_ref, g_ref, ml_ref, mc_ref, w_ref, *o_refs, n_lat_tiles, n_tmaj):
    i = pl.program_id(1)
    h = _norm_mod(x_ref[0], g_ref[...], ml_ref[0], mc_ref[0], i >= n_lat_tiles, 0)
    r = _dot(h.astype(BF16), w_ref[...])
    if n_tmaj:
        o_refs[0][...] = r[:, :n_tmaj]
        o_refs[1][0] = r[:, n_tmaj:]
    else:
        o_refs[0][0] = r


def in_proj(x, g, mods, w, n_lat, n_tmaj):
    b, t, d = x.shape
    n = w.shape[1]
    tm = TOKEN_TILE
    if n_tmaj:
        out_shape = [jax.ShapeDtypeStruct((t, b * n_tmaj), F32), jax.ShapeDtypeStruct((b, t, n - n_tmaj), F32)]
        out_specs = [pl.BlockSpec((tm, n_tmaj), lambda bb, i: (i, bb)),
                     pl.BlockSpec((1, tm, n - n_tmaj), lambda bb, i: (bb, i, 0))]
    else:
        out_shape = [jax.ShapeDtypeStruct((b, t, n), F32)]
        out_specs = [pl.BlockSpec((1, tm, n), lambda bb, i: (bb, i, 0))]
    return pl.pallas_call(
        functools.partial(_inproj_kernel, n_lat_tiles=n_lat // tm, n_tmaj=n_tmaj),
        grid=(b, t // tm),
        in_specs=[pl.BlockSpec((1, tm, d), lambda bb, i: (bb, i, 0)),
                  pl.BlockSpec((1, d), lambda bb, i: (0, 0)),
                  pl.BlockSpec((1, 6, d), lambda bb, i: (bb, 0, 0)),
                  pl.BlockSpec((1, 6, d), lambda bb, i: (b, 0, 0)),
                  pl.BlockSpec((d, n), lambda bb, i: (0, 0), pipeline_mode=pl.Buffered(1))],
        out_specs=out_specs,
        out_shape=out_shape,
        compiler_params=_params("arbitrary", "arbitrary"),
        name="in_proj",
    )(x, g.reshape(1, d), mods, mods, w)


MLP_CHUNK = 1024


def _mix_mlp_kernel(x_ref, a_ref, b_ref, bc_ref, wa_ref, wb_ref, g_ref, ml_ref, mc_ref, w1_ref, w2_ref, fg_ref,
                    o_ref, acc_ref, *, n_lat_tiles, a_tmaj, b_split, final):
    i = pl.program_id(1)
    is_ctx = i >= n_lat_tiles
    m = jnp.where(is_ctx, mc_ref[0], ml_ref[0])
    a = a_ref[...] if a_tmaj else a_ref[0]
    bmix = jnp.where(is_ctx, bc_ref[0], b_ref[0]) if b_split else b_ref[0]
    mix = _dot(a.astype(BF16), wa_ref[...]) + _dot(bmix.astype(BF16), wb_ref[...])
    x = x_ref[0] + m[2:3] * mix
    h = _norm_mod(x, g_ref[...], ml_ref[0], mc_ref[0], is_ctx, 3).astype(BF16)
    for k in range(w1_ref.shape[1] // MLP_CHUNK):
        ks = slice(k * MLP_CHUNK, (k + 1) * MLP_CHUNK)
        hid = jnp.maximum(_dot(h, w1_ref[:, ks]), 0.0)
        part = _dot((hid * hid).astype(BF16), w2_ref[ks, :])
        if k == 0:
            acc_ref[...] = part
        else:
            acc_ref[...] += part
    y = x + m[5:6] * acc_ref[...]
    if final:
        ms = jnp.mean(y * y, axis=-1, keepdims=True)
        y = y * lax.rsqrt(ms + EPS) * fg_ref[...]
    o_ref[0] = y


def mix_mlp(x, a, bm, bm_ctx, w_out, g, mods, w1, w2, final_g, n_lat, n_rows, a_tmaj, final):
    b, t, d = x.shape
    wd = bm.shape[-1]
    hid = w1.shape[1]
    tm = TOKEN_TILE
    nl = n_lat // tm
    once = pl.Buffered(1)
    if a_tmaj:
        a_spec = pl.BlockSpec((tm, wd), lambda bb, i: (i, bb))
    else:
        a_spec = pl.BlockSpec((1, tm, wd), lambda bb, i: (bb, i, 0))
    b_split = bm_ctx is not None
    if b_split:
        b_spec = pl.BlockSpec((1, tm, wd), lambda bb, i: (bb, jnp.minimum(i, nl - 1), 0))
        bc_spec = pl.BlockSpec((1, tm, wd), lambda bb, i: (bb, jnp.maximum(i - nl, 0), 0))
    else:
        b_spec = pl.BlockSpec((1, tm, wd), lambda bb, i: (bb, i, 0))
        bc_spec = pl.BlockSpec((1, tm, wd), lambda bb, i: (0, 0, 0))
        bm_ctx = bm
    return pl.pallas_call(
        functools.partial(_mix_mlp_kernel, n_lat_tiles=nl, a_tmaj=a_tmaj, b_split=b_split, final=final),
        grid=(b, n_rows // tm),
        in_specs=[pl.BlockSpec((1, tm, d), lambda bb, i: (bb, i, 0)),
                  a_spec,
                  b_spec,
                  bc_spec,
                  pl.BlockSpec((wd, d), lambda bb, i: (0, 0), pipeline_mode=once),
                  pl.BlockSpec((wd, d), lambda bb, i: (1, 0), pipeline_mode=once),
                  pl.BlockSpec((1, d), lambda bb, i: (0, 0)),
                  pl.BlockSpec((1, 6, d), lambda bb, i: (bb, 0, 0)),
                  pl.BlockSpec((1, 6, d), lambda bb, i: (b, 0, 0)),
                  pl.BlockSpec((d, hid), lambda bb, i: (0, 0), pipeline_mode=once),
                  pl.BlockSpec((hid, d), lambda bb, i: (0, 0), pipeline_mode=once),
                  pl.BlockSpec((1, d), lambda bb, i: (0, 0))],
        out_specs=pl.BlockSpec((1, tm, d), lambda bb, i: (bb, i, 0)),
        out_shape=jax.ShapeDtypeStruct((b, n_rows, d), F32),
        scratch_shapes=[pltpu.VMEM((tm, d), F32)],
        compiler_params=_params("arbitrary", "arbitrary"),
        name="mix_mlp",
    )(x, a, bm, bm_ctx, w_out, w_out, g.reshape(1, d), mods, mods, w1, w2, final_g.reshape(1, d))


def _conv3(x, w, first, last):
    rows = x.shape[0]
    xm = jnp.where(first, 0.0, pltpu.roll(x, 1, 0))
    xp = jnp.where(last, 0.0, pltpu.roll(x, rows - 1, 0))
    return xm * w[0:1] + x * w[1:2] + xp * w[2:3]


def _shortconv_kernel(x_ref, w_ref, o_ref, *, n_lat):
    x = x_ref[0]
    t = x.shape[0]
    row = lax.broadcasted_iota(jnp.int32, (t, 1), 0)
    first = (row == 0) | (row == n_lat)
    last = (row == n_lat - 1) | (row == t - 1)
    o_ref[0, 0] = _conv3(x, w_ref[...], first, last)


def short_conv(p, w, n_lat):
    b, t, _ = p.shape
    c = w.shape[1]
    cb = GROUP_W
    return pl.pallas_call(
        functools.partial(_shortconv_kernel, n_lat=n_lat),
        grid=(b, c // cb),
        in_specs=[pl.BlockSpec((1, t, cb), lambda bb, j: (bb, 0, j)),
                  pl.BlockSpec((SHORT_K, cb), lambda bb, j: (0, j))],
        out_specs=pl.BlockSpec((1, 1, t, cb), lambda bb, j: (j, bb, 0, 0)),
        out_shape=jax.ShapeDtypeStruct((c // cb, b, t, cb), F32),
        compiler_params=_params("arbitrary", "arbitrary"),
        name="short_conv",
    )(p, w)


S5_KB = LANE
S5_NB = GROUP_W // S5_KB
S5_SB = S5_W // S5_NB


def _s5_kernel(*refs, tc, nb, reverse, finish):
    if finish:
        (u_ref, wbr_ref, wbi_ref, lr_ref, li_ref, wcr_ref, wci_ref, yf_ref, d_ref, gw_ref, gb_ref,
         o_ref, hr_s, hi_s, sr_s, si_s) = refs
    else:
        (u_ref, wbr_ref, wbi_ref, lr_ref, li_ref, wcr_ref, wci_ref,
         o_ref, hr_s, hi_s, sr_s, si_s) = refs

    @pl.when(pl.program_id(0) == 0)
    def _():
        sr_s[...] = jnp.zeros_like(sr_s)
        si_s[...] = jnp.zeros_like(si_s)

    u = u_ref[...]
    ub = u.astype(BF16)
    for j in range(S5_NB):
        uj = ub[:, j * S5_KB:(j + 1) * S5_KB]
        hr_s[:, j * S5_SB:(j + 1) * S5_SB] = _dot(uj, wbr_ref[j])
        hi_s[:, j * S5_SB:(j + 1) * S5_SB] = _dot(uj, wbi_ref[j])

    for s in range(S5_NB):
        sl = slice(s * S5_SB, (s + 1) * S5_SB)
        lr = jnp.broadcast_to(lr_ref[:, sl], (nb, S5_SB))
        li = jnp.broadcast_to(li_ref[:, sl], (nb, S5_SB))

        def body(k, carry, sl=sl, lr=lr, li=li):
            hr, hi = carry
            t = (tc - 1 - k) if reverse else k
            r0 = pl.multiple_of(t * nb, nb)
            nr = lr * hr - li * hi + hr_s[pl.ds(r0, nb), sl]
            ni = lr * hi + li * hr + hi_s[pl.ds(r0, nb), sl]
            hr_s[pl.ds(r0, nb), sl] = nr
            hi_s[pl.ds(r0, nb), sl] = ni
            return nr, ni

        hr, hi = lax.fori_loop(0, tc, body, (sr_s[:, sl], si_s[:, sl]), unroll=4)
        sr_s[:, sl] = hr
        si_s[:, sl] = hi

    for j in range(S5_NB):
        sl = slice(j * S5_SB, (j + 1) * S5_SB)
        co = slice(j * S5_KB, (j + 1) * S5_KB)
        y = _dot(hr_s[:, sl].astype(BF16), wcr_ref[j]) + _dot(hi_s[:, sl].astype(BF16), wci_ref[j])
        if finish:
            o_ref[:, co] = y + yf_ref[:, co] + u[:, co] * d_ref[:, co]
        else:
            o_ref[:, co] = y

    if finish:
        y = o_ref[...]
        y = 0.5 * y * (1.0 + lax.erf(y * (2.0 ** -0.5)))
        z = _dot(y.astype(BF16), gw_ref[...]) + gb_ref[...]
        o_ref[...] = y * jax.nn.sigmoid(z)


def s5_scan_call(u2, tabs, n_lat_steps, n_steps, nb, reverse, extra):
    tc = 32
    rows = tc * nb
    n_chunks = n_steps // tc
    n_lat_chunks = n_lat_steps // tc
    n_ctx_chunks = n_chunks - n_lat_chunks
    if reverse:
        def cmap(c):
            return (n_chunks - 1 - c, 0)
    else:
        def cmap(c):
            return (jnp.where(c < n_ctx_chunks, n_lat_chunks + c, c - n_ctx_chunks), 0)
    wbr, wbi, lr, li, wcr, wci = tabs
    full3 = lambda a: pl.BlockSpec(a.shape, lambda c: (0, 0, 0))
    full2 = lambda a: pl.BlockSpec(a.shape, lambda c: (0, 0))
    in_specs = [pl.BlockSpec((rows, GROUP_W), cmap), full3(wbr), full3(wbi), full2(lr), full2(li),
                full3(wcr), full3(wci)]
    args = [u2, wbr, wbi, lr, li, wcr, wci]
    finish = extra is not None
    if finish:
        yf, dsk, gw, gb = extra
        in_specs += [pl.BlockSpec((rows, GROUP_W), cmap), full2(dsk), full2(gw), full2(gb)]
        args += [yf, dsk, gw, gb]
    return pl.pallas_call(
        functools.partial(_s5_kernel, tc=tc, nb=nb, reverse=reverse, finish=finish),
        grid=(n_chunks,),
        in_specs=in_specs,
        out_specs=pl.BlockSpec((rows, GROUP_W), cmap),
        out_shape=jax.ShapeDtypeStruct(u2.shape, F32),
        scratch_shapes=[pltpu.VMEM((rows, S5_W), F32), pltpu.VMEM((rows, S5_W), F32),
                        pltpu.VMEM((nb, S5_W), F32), pltpu.VMEM((nb, S5_W), F32)],
        compiler_params=_params("arbitrary"),
        name="s5_bwd_finish" if finish else "s5_fwd",
    )(*args)


def _block_diag(x):
    nblk, g, r, c = x.shape
    eye = jnp.eye(g, dtype=x.dtype)
    return jnp.einsum('jgrc,gh->jgrhc', x, eye).reshape(nblk, g * r, g * c)


def s5_tables(lam_re, lam_im, log_dt, b_re, b_im, c_re, c_im):
    lam_re = jnp.minimum(lam_re.astype(F32), -1e-4)
    lam_im = lam_im.astype(F32)
    dt = jnp.exp(log_dt.astype(F32))[:, None]
    mag = jnp.exp(lam_re * dt)
    lb_re = mag * jnp.cos(lam_im * dt)
    lb_im = mag * jnp.sin(lam_im * dt)
    den = lam_re * lam_re + lam_im * lam_im
    f_re = ((lb_re - 1.0) * lam_re + lb_im * lam_im) / den
    f_im = (lb_im * lam_re - (lb_re - 1.0) * lam_im) / den
    b_re = b_re.astype(F32)
    b_im = b_im.astype(F32)
    bb_re = f_re[..., None] * b_re - f_im[..., None] * b_im
    bb_im = f_re[..., None] * b_im + f_im[..., None] * b_re
    gpb = S5_KB // S5_GROUP
    to_b = lambda x: _block_diag(
        jnp.swapaxes(x, 1, 2).reshape(S5_NB, gpb, S5_GROUP, S5_STATE)).astype(BF16)
    to_c = lambda x: _block_diag(
        jnp.swapaxes(x.astype(F32), 1, 2).reshape(S5_NB, gpb, S5_STATE, S5_GROUP)).astype(BF16)
    return (to_b(bb_re), to_b(bb_im), lb_re.reshape(1, S5_W), lb_im.reshape(1, S5_W),
            to_c(c_re), to_c(-c_im.astype(F32)))


def dft_tables(n):
    nn = 2 * n
    f = jnp.arange(n, dtype=jnp.int32)[:, None]
    t = jnp.arange(n, dtype=jnp.int32)[None, :]
    q = 1 << (int(math.log2(n)) // 2)

    def factor(tt):
        ang = ((f * tt[None, :]) % nn).astype(F32) * (2.0 * math.pi / nn)
        return jnp.cos(ang), jnp.sin(ang)

    c1, s1 = factor(jnp.arange(n // q, dtype=jnp.int32) * q)
    c0, s0 = factor(jnp.arange(q, dtype=jnp.int32))
    cos = (c1[:, :, None] * c0[:, None, :] - s1[:, :, None] * s0[:, None, :]).reshape(n, n)
    sin = (s1[:, :, None] * c0[:, None, :] + c1[:, :, None] * s0[:, None, :]).reshape(n, n)
    nyq = jnp.where(t % 2 == 0, 1.0, -1.0).astype(F32)
    msin = jnp.where(f == 0, nyq, -sin)
    fwd = jnp.concatenate([cos, msin], axis=0)
    wf = jnp.where(jnp.arange(n) == 0, 1.0 / nn, 2.0 / nn).astype(F32)
    return fwd, wf


def _filtdft_kernel(fh_ref, fl_ref, hh_ref, hl_ref, o_ref):
    fh = fh_ref[...]
    o_ref[...] = _dot(fh, hh_ref[0]) + _dot(fh, hl_ref[0]) + _dot(fl_ref[...], hh_ref[0])


def filter_dft(fwd_hi, fwd_lo, h_sum, h_dif):
    n2, n = fwd_hi.shape
    c = h_sum.shape[1]
    tr = min(256, n)
    tcn = 512
    nrb = n // tr
    hh, hl = _split_bf16(jnp.stack([h_sum, h_dif]))
    frow = lambda i, j: (jnp.where(i == 2 * nrb, nrb, i), 0)
    hsel = lambda i, j: (jnp.where((i >= nrb) & (i < 2 * nrb), 1, 0), 0, j)
    return pl.pallas_call(
        _filtdft_kernel,
        grid=(2 * nrb + 1, c // tcn),
        in_specs=[pl.BlockSpec((tr, n), frow),
                  pl.BlockSpec((tr, n), frow),
                  pl.BlockSpec((1, n, tcn), hsel),
                  pl.BlockSpec((1, n, tcn), hsel)],
        out_specs=pl.BlockSpec((tr, tcn), lambda i, j: (i, j)),
        out_shape=jax.ShapeDtypeStruct((n2 + tr, c), F32),
        compiler_params=_params("arbitrary", "arbitrary"),
        name="filter_dft",
    )(fwd_hi, fwd_lo, hh, hl)


def hyena_filter_response(n, fwd_hi, fwd_lo, wf, f_w1, f_b1, f_w2, f_b2, f_w3, f_freq, log_decay):
    t = jnp.linspace(0.0, 1.0, n, dtype=F32)[:, None]
    w = 2.0 * math.pi * jnp.arange(n, dtype=F32)[:, None] / n
    bands = jnp.linspace(1e-4, HY_BANDS - 1, HY_BANDS, dtype=F32)[None, :]
    z = jnp.concatenate([t, jnp.cos(bands * w), -jnp.sin(bands * w)], axis=-1)
    freq = f_freq.astype(F32)
    hid = jnp.sin(freq[0] * (jnp.dot(z, f_w1.astype(F32), precision=HIGHEST) + f_b1.astype(F32)))
    hid = jnp.sin(freq[1] * (jnp.dot(hid, f_w2.astype(F32), precision=HIGHEST) + f_b2.astype(F32)))
    h = jnp.dot(hid, f_w3.astype(F32), precision=HIGHEST).reshape(n, 2, HY_ORDER, GROUP_W)
    h = h * jnp.exp(-t[:, :, None, None] * jnp.exp(log_decay.astype(F32)))
    cw = HY_ORDER * GROUP_W
    h_fwd = h[:, 0].reshape(n, cw)
    h_bwd = h[:, 1].reshape(n, cw).at[0].set(0.0)
    resp = filter_dft(fwd_hi, fwd_lo, h_fwd + h_bwd, h_fwd - h_bwd)
    k_re = resp[:n]
    first = (jnp.arange(n) == 0)[:, None]
    k_im = jnp.where(first, resp[2 * n:2 * n + 1], resp[n:2 * n])
    k_re = (k_re * wf[:, None]).reshape(n, HY_ORDER, GROUP_W).transpose(1, 0, 2)
    k_im = (k_im * wf[:, None]).reshape(n, HY_ORDER, GROUP_W).transpose(1, 0, 2)
    return k_re, k_im


def _hyena_kernel(z0_ref, x1_ref, x2_ref, cw_ref, cb_ref, fw_ref, inv_ref, kr_ref, ki_ref, bias_ref,
                  o_ref, zb_s, zf_s, acc_s):
    o = pl.program_id(1)
    f = pl.program_id(2)
    nf = pl.num_programs(2)
    c = GROUP_W

    def stream(ref, s):
        x = ref[0]
        row = lax.broadcasted_iota(jnp.int32, (x.shape[0], 1), 0)
        cols = slice(s * c, (s + 1) * c)
        return _conv3(x, cw_ref[:, cols], row == 0, row == x.shape[0] - 1) + cb_ref[:, cols]

    @pl.when((o == 0) & (f == 0))
    def _():
        z0 = stream(z0_ref, 0)
        zf_s[...] = z0
        zb_s[...] = z0.astype(BF16)

    @pl.when(f == 0)
    def _():
        acc_s[...] = jnp.zeros_like(acc_s)

    xf = _dot(fw_ref[0], zb_s[...])
    fb = xf.shape[0] // 2
    xr = xf[:fb]
    xi = xf[fb:]
    kr = kr_ref[0]
    ki = ki_ref[0]
    row = lax.broadcasted_iota(jnp.int32, (fb, 1), 0)
    packed = (row == 0) & (f == 0)
    yr = jnp.where(packed, xr * kr, xr * kr - xi * ki)
    yi = jnp.where(packed, xi * ki, xr * ki + xi * kr)
    yf = jnp.concatenate([yr, yi], axis=0).astype(BF16)
    acc_s[...] += _dot(inv_ref[0], yf)

    @pl.when(f == nf - 1)
    def _():
        bias = bias_ref[...]

        @pl.when(o == 0)
        def _():
            z1 = stream(x1_ref, 1) * (acc_s[...] + zf_s[...] * bias[0:1])
            zf_s[...] = z1
            zb_s[...] = z1.astype(BF16)

        @pl.when(o == 1)
        def _():
            o_ref[0] = stream(x2_ref, 2) * (acc_s[...] + zf_s[...] * bias[1:2])


HY_FREQ_BLOCK = 256


def dft_blocks(fwd_hi):
    n = fwd_hi.shape[1]
    fb = min(HY_FREQ_BLOCK, n)
    blocks = fwd_hi.reshape(2, n // fb, fb, n).transpose(1, 0, 2, 3).reshape(n // fb, 2 * fb, n)
    return blocks, blocks.transpose(0, 2, 1)


def hyena_call(p, n, row_blk, conv_w, conv_b, fwd_blk, inv_blk, k_re, k_im, bias):
    b = p.shape[0]
    c = GROUP_W
    nf, fb2, _ = fwd_blk.shape
    fb = fb2 // 2
    once = pl.Buffered(1)
    in_specs = [pl.BlockSpec((1, n, c), lambda bb, o, f: (bb, row_blk, 0), pipeline_mode=once),
                pl.BlockSpec((1, n, c), lambda bb, o, f: (bb, row_blk, 1), pipeline_mode=once),
                pl.BlockSpec((1, n, c), lambda bb, o, f: (bb, row_blk, 2), pipeline_mode=once),
                pl.BlockSpec((SHORT_K, 3 * c), lambda bb, o, f: (0, 0)),
                pl.BlockSpec((1, 3 * c), lambda bb, o, f: (0, 0)),
                pl.BlockSpec((1, fb2, n), lambda bb, o, f: (f, 0, 0)),
                pl.BlockSpec((1, n, fb2), lambda bb, o, f: (f, 0, 0)),
                pl.BlockSpec((1, fb, c), lambda bb, o, f: (o, f, 0)),
                pl.BlockSpec((1, fb, c), lambda bb, o, f: (o, f, 0)),
                pl.BlockSpec((HY_ORDER, c), lambda bb, o, f: (0, 0))]
    return pl.pallas_call(
        _hyena_kernel,
        grid=(b, HY_ORDER, nf),
        in_specs=in_specs,
        out_specs=pl.BlockSpec((1, n, c), lambda bb, o, f: (bb, 0, 0)),
        out_shape=jax.ShapeDtypeStruct((b, n, c), F32),
        scratch_shapes=[pltpu.VMEM((n, c), BF16), pltpu.VMEM((n, c), F32), pltpu.VMEM((n, c), F32)],
        compiler_params=_params("arbitrary", "arbitrary", "arbitrary"),
        name="hyena",
    )(p, p, p, conv_w, conv_b.reshape(1, 3 * c), fwd_blk, inv_blk, k_re, k_im, bias)


def _rwkv_lora_kernel(p_ref, ww_ref, wa_ref, wg_ref, w0_ref, a0_ref, wp_ref, ap_ref, g_ref):
    x = p_ref[0]
    wa_in = x[:, :LANE]
    th = jnp.tanh(wa_in).astype(BF16)
    lin = wa_in.astype(BF16)
    w0 = w0_ref[...]
    a0 = a0_ref[...]
    for d in range(2):
        wl = -jax.nn.softplus(-(w0[d:d + 1] + _dot(th, ww_ref[d]))) - 0.5
        wp_ref[d, 0] = jnp.exp(-jnp.exp(wl))
        ap_ref[d, 0] = jax.nn.sigmoid(a0[d:d + 1] + _dot(lin, wa_ref[d]))
    g_ref[0] = _dot(jax.nn.sigmoid(x[:, LANE:]).astype(BF16), wg_ref[...])


def rwkv_lora(p, w_up, a_up, g_up, w0, a0):
    b, t, _ = p.shape
    c = GROUP_W
    tm = TOKEN_TILE
    zeros = jnp.zeros((2, 64, c), F32)
    ww = jnp.concatenate([w_up.astype(F32), zeros], axis=1).astype(BF16)
    wa = jnp.concatenate([zeros, a_up.astype(F32)], axis=1).astype(BF16)
    both = jax.ShapeDtypeStruct((2, b, t, c), F32)
    both_spec = pl.BlockSpec((2, 1, tm, c), lambda bb, i: (0, bb, i, 0))
    return pl.pallas_call(
        _rwkv_lora_kernel,
        grid=(b, t // tm),
        in_specs=[pl.BlockSpec((1, tm, RW_LORA_W), lambda bb, i: (bb, i, 3 * c // RW_LORA_W)),
                  pl.BlockSpec((2, LANE, c), lambda bb, i: (0, 0, 0)),
                  pl.BlockSpec((2, LANE, c), lambda bb, i: (0, 0, 0)),
                  pl.BlockSpec((LANE, c), lambda bb, i: (0, 0)),
                  pl.BlockSpec((2, c), lambda bb, i: (0, 0)),
                  pl.BlockSpec((2, c), lambda bb, i: (0, 0))],
        out_specs=[both_spec, both_spec, pl.BlockSpec((1, tm, c), lambda bb, i: (bb, i, 0))],
        out_shape=[both, both, jax.ShapeDtypeStruct((b, t, c), F32)],
        compiler_params=_params("arbitrary", "arbitrary"),
        name="rwkv_lora",
    )(p, ww, wa, g_up.astype(BF16), w0, a0)


SCAN_UNROLL = 8


def _rwkv_scan_kernel(wpf_ref, wpb_ref, apf_ref, apb_ref, kf_ref, kb_ref, vf_ref, vb_ref, rf_ref, rb_ref,
                      kkc_ref, kac_ref, yf_ref, yb_ref, s_ref, w_s, kk_s, b_s, ke_s, r_s, v_s, *, tb):
    n = RW_HEAD

    @pl.when(pl.program_id(0) == 0)
    def _():
        s_ref[...] = jnp.zeros_like(s_ref)

    def both(f_ref, b_ref, t):
        return jnp.concatenate([f_ref[0, t], b_ref[0, tb - 1 - t]], axis=-1)

    def prepare(t, carry):
        k = both(kf_ref, kb_ref, t)
        a = both(apf_ref, apb_ref, t)
        kk = k * kkc_ref[...]
        nrm = jnp.sqrt(jnp.sum(kk * kk, axis=0, keepdims=True))
        kk = kk / jnp.maximum(nrm, 1e-12)
        w_s[t] = both(wpf_ref, wpb_ref, t)
        kk_s[t] = kk
        b_s[t] = kk * a
        ke_s[t] = k * (1.0 + (a - 1.0) * kac_ref[...])
        r_s[t] = both(rf_ref, rb_ref, t)
        v_s[t] = both(vf_ref, vb_ref, t)
        return carry

    lax.fori_loop(0, tb, prepare, 0, unroll=4)
    sa0 = jnp.zeros((n, LANE), F32)
    for j in range(n):
        sa0 = sa0 + s_ref[j] * kk_s[0, j:j + 1, :]

    def step(t, sa):
        nxt = jnp.minimum(t + 1, tb - 1)
        v = v_s[t]

        def keys(j, carry):
            y, sa_next = carry
            row = pl.ds(j, 1)
            s = s_ref[j] * w_s[t, row, :] - sa * b_s[t, row, :] + v * ke_s[t, row, :]
            s_ref[j] = s
            return y + s * r_s[t, row, :], sa_next + s * kk_s[nxt, row, :]

        zero = jnp.zeros((n, LANE), F32)
        y, sa_next = lax.fori_loop(0, n, keys, (zero, zero), unroll=SCAN_UNROLL)
        yf_ref[t] = y
        yb_ref[tb - 1 - t] = y
        return sa_next

    lax.fori_loop(0, tb, step, sa0)


def rwkv_scan(w, a, rkv, kkc, kac, n_lat):
    t = rkv.shape[1]
    tb = 32
    n = RW_HEAD
    half = LANE // 2
    n_blk = t // tb
    n_lat_blk = n_lat // tb
    n_ctx_blk = n_blk - n_lat_blk

    def fblk(c):
        return jnp.where(c < n_ctx_blk, n_lat_blk + c, c - n_ctx_blk)

    def rblk(c):
        return n_blk - 1 - c

    def pair(qf, qb):
        return [pl.BlockSpec((1, tb, n, half), lambda c: (qf, fblk(c), 0, 0)),
                pl.BlockSpec((1, tb, n, half), lambda c: (qb, rblk(c), 0, 0))]

    cst = pl.BlockSpec((n, LANE), lambda c: (0, 0))
    buf = pltpu.VMEM((tb, n, LANE), F32)
    out = jax.ShapeDtypeStruct((t, n, LANE), F32)
    return pl.pallas_call(
        functools.partial(_rwkv_scan_kernel, tb=tb),
        grid=(n_blk,),
        in_specs=pair(0, 1) + pair(0, 1) + pair(1, 1) + pair(2, 2) + pair(0, 0) + [cst] * 2,
        out_specs=[pl.BlockSpec((tb, n, LANE), lambda c: (fblk(c), 0, 0)),
                   pl.BlockSpec((tb, n, LANE), lambda c: (rblk(c), 0, 0))],
        out_shape=[out, out],
        scratch_shapes=[pltpu.VMEM((n, n, LANE), F32), buf, buf, buf, buf, buf, buf],
        compiler_params=_params("arbitrary"),
        name="rwkv_scan",
    )(w, w, a, a, rkv, rkv, rkv, rkv, rkv, rkv, kkc, kac)


def _rwkv_post_kernel(y_ref, r_ref, k_ref, v_ref, ap_ref, g_ref, m_ref, ka_ref, rk_ref, lg_ref, lb_ref,
                      o_ref):
    m = m_ref[...]

    def head_mean(x):
        hi, lo = _split_bf16(x)
        return _dot(hi, m) + _dot(lo, m)

    y = y_ref[0]
    d = y - head_mean(y)
    var = head_mean(d * d)
    yn = d * lax.rsqrt(var + RW_LN_EPS) * lg_ref[...] + lb_ref[...]
    a_sum = ap_ref[0, 0] + ap_ref[1, 0]
    k_sum = k_ref[0, 0] * (2.0 + (a_sum - 2.0) * ka_ref[...])
    bonus = head_mean(r_ref[0, 0] * k_sum * rk_ref[...]) * float(RW_HEAD) * v_ref[0, 0]
    o_ref[0] = (yn + bonus) * g_ref[0]


def rwkv_post(y, rkv, ap, gate, k_a, r_k, ln_g, ln_b, n_rows):
    b, t, c = y.shape
    tm = TOKEN_TILE
    hm = jnp.kron(jnp.eye(RW_HEADS, dtype=F32), jnp.full((RW_HEAD, RW_HEAD), 1.0 / RW_HEAD, F32)).astype(BF16)
    tok = pl.BlockSpec((1, tm, c), lambda bb, i: (bb, i, 0))
    vec = pl.BlockSpec((1, c), lambda bb, i: (0, 0))
    rkv_spec = lambda q: pl.BlockSpec((1, 1, tm, c), lambda bb, i: (q, bb, i, 0))
    return pl.pallas_call(
        _rwkv_post_kernel,
        grid=(b, n_rows // tm),
        in_specs=[tok, rkv_spec(0), rkv_spec(1), rkv_spec(2),
                  pl.BlockSpec((2, 1, tm, c), lambda bb, i: (0, bb, i, 0)), tok,
                  pl.BlockSpec((c, c), lambda bb, i: (0, 0)), vec, vec, vec, vec],
        out_specs=tok,
        out_shape=jax.ShapeDtypeStruct((b, n_rows, c), F32),
        compiler_params=_params("arbitrary", "arbitrary"),
        name="rwkv_post",
    )(y, rkv, rkv, rkv, ap, gate, hm, k_a.reshape(1, c), r_k.reshape(1, c),
      ln_g.reshape(1, c), ln_b.reshape(1, c))


def _to_scan_layout(a):
    s, b, t, _ = a.shape
    x = a.reshape(s, b, t, RW_HEADS, RW_HEAD).transpose(0, 2, 4, 1, 3)
    return x.reshape(s, t, RW_HEAD, b * RW_HEADS)


def _from_scan_layout(yf, yb, b):
    t = yf.shape[0]
    half = b * RW_HEADS
    y = yf[..., :half] + yb[..., half:]
    return y.reshape(t, RW_HEAD, b, RW_HEADS).transpose(2, 0, 3, 1).reshape(b, t, GROUP_W)


def _chain_const(x, b):
    return jnp.tile(x.astype(F32).reshape(RW_HEADS, RW_HEAD).T, (1, 2 * b))


def _rope(x, cos, sin):
    lane = lax.broadcasted_iota(jnp.int32, (1, LANE), 1)
    first = (lane % (2 * ROPE_FREQS)) < ROPE_FREQS
    partner = jnp.where(first, pltpu.roll(x, LANE - ROPE_FREQS, 1), pltpu.roll(x, ROPE_FREQS, 1))
    return x * cos + partner * sin


def _attn_tile(q, k, v, lam):
    lane = lax.broadcasted_iota(jnp.int32, (1, LANE), 1)
    m0 = lane < DA_HEAD
    q0 = jnp.where(m0, q, 0.0).astype(BF16)
    q1 = jnp.where(m0, 0.0, q).astype(BF16)
    dn = (((1,), (1,)), ((), ()))
    s0 = lax.dot_general(q0, k, dn, preferred_element_type=F32)
    s1 = lax.dot_general(q1, k, dn, preferred_element_type=F32)
    p0 = jnp.exp2(s0 - jnp.max(s0, axis=-1, keepdims=True))
    p1 = jnp.exp2(s1 - jnp.max(s1, axis=-1, keepdims=True))
    c0 = (1.0 / jnp.sum(p0, axis=-1, keepdims=True)).astype(BF16)
    c1 = (lam / jnp.sum(p1, axis=-1, keepdims=True)).astype(BF16)
    return _dot(p0.astype(BF16) * c0 - p1.astype(BF16) * c1, v)


def _attn_kernel(q_ref, k_ref, v_ref, cq_ref, sq_ref, ck_ref, sk_ref, lp_ref, g_ref, o_ref,
                 kr_s, vb_s, *, n_lat, n_lat_tiles, lam_init):
    i = pl.program_id(2)

    @pl.when(i == 0)
    def _():
        kr_s[...] = _rope(k_ref[0], ck_ref[...], sk_ref[...]).astype(BF16)
        vb_s[...] = v_ref[0].astype(BF16)

    lp = lp_ref[...]
    lam = (jnp.exp(jnp.sum(lp[0:1] * lp[1:2], axis=-1, keepdims=True))
           - jnp.exp(jnp.sum(lp[2:3] * lp[3:4], axis=-1, keepdims=True)) + lam_init)
    q = _rope(q_ref[0], cq_ref[...], sq_ref[...]) * (DA_SCALE * math.log2(math.e))

    def finish(o):
        on = o * lax.rsqrt(jnp.mean(o * o, axis=-1, keepdims=True) + DA_SUBLN_EPS)
        o_ref[0] = on * g_ref[...] * (1.0 - lam_init)

    @pl.when(i < n_lat_tiles)
    def _():
        finish(_attn_tile(q, kr_s[...], vb_s[...], lam))

    @pl.when(i >= n_lat_tiles)
    def _():
        finish(_attn_tile(q, kr_s[n_lat:], vb_s[n_lat:], lam))


def diff_attention(p, col0, cosf, sins, lam_p, subln_g, lam_init, n_lat, n_rows):
    b, t, _ = p.shape
    tq = TOKEN_TILE
    off = col0 // LANE
    hq = DA_HEADS
    return pl.pallas_call(
        functools.partial(_attn_kernel, n_lat=n_lat, n_lat_tiles=n_lat // tq, lam_init=lam_init),
        grid=(b, hq, n_rows // tq),
        in_specs=[pl.BlockSpec((1, tq, LANE), lambda bb, h, i: (bb, i, off + h)),
                  pl.BlockSpec((1, t, LANE), lambda bb, h, i: (bb, 0, off + hq + h)),
                  pl.BlockSpec((1, t, LANE), lambda bb, h, i: (bb, 0, off + 2 * hq + h)),
                  pl.BlockSpec((tq, LANE), lambda bb, h, i: (i, 0)),
                  pl.BlockSpec((tq, LANE), lambda bb, h, i: (i, 0)),
                  pl.BlockSpec((t, LANE), lambda bb, h, i: (0, 0)),
                  pl.BlockSpec((t, LANE), lambda bb, h, i: (0, 0)),
                  pl.BlockSpec((4, DA_HEAD), lambda bb, h, i: (0, 0)),
                  pl.BlockSpec((1, LANE), lambda bb, h, i: (0, 0))],
        out_specs=pl.BlockSpec((1, tq, LANE), lambda bb, h, i: (bb, i, h)),
        out_shape=jax.ShapeDtypeStruct((b, n_rows, hq * DA_V), F32),
        scratch_shapes=[pltpu.VMEM((t, LANE), BF16), pltpu.VMEM((t, LANE), BF16)],
        compiler_params=_params("arbitrary", "arbitrary", "arbitrary"),
        name="diff_attention",
    )(p, p, p, cosf, sins, cosf, sins, lam_p, subln_g.reshape(1, DA_V))


def rope_tables(n_lat, n_ctx):
    rows = n_lat // GRID_W
    row = jnp.repeat(jnp.arange(rows, dtype=F32), GRID_W)
    col = jnp.tile(jnp.arange(GRID_W, dtype=F32), rows)
    inv = ROPE_BASE ** (-jnp.arange(ROPE_FREQS, dtype=F32) / ROPE_FREQS)
    ang = jnp.stack([row[:, None] * inv, col[:, None] * inv], axis=1)
    cos, sin = jnp.cos(ang), jnp.sin(ang)
    cosf = jnp.concatenate([cos, cos], axis=-1).reshape(n_lat, DA_HEAD)
    sins = jnp.concatenate([-sin, sin], axis=-1).reshape(n_lat, DA_HEAD)
    cosf = jnp.concatenate([jnp.tile(cosf, (1, 2)), jnp.ones((n_ctx, LANE), F32)], axis=0)
    sins = jnp.concatenate([jnp.tile(sins, (1, 2)), jnp.zeros((n_ctx, LANE), F32)], axis=0)
    return cosf, sins


def _even_mixers(xs, mods, keep_ctx, n_lat, norm1_g, w_in, s5p, hyp, dft):
    b, t, _ = xs.shape
    n_ctx = t - n_lat
    u_t, p_h = in_proj(xs, norm1_g, mods, w_in.astype(BF16), n_lat, GROUP_W)

    (lam_re, lam_im, log_dt, b_re, b_im, c_re, c_im, d_skip, glu_w, glu_b) = s5p
    u2 = u_t.reshape(t * b, GROUP_W)
    tabs = [s5_tables(lam_re[d], lam_im[d], log_dt[d], b_re[d], b_im[d], c_re[d], c_im[d]) for d in range(2)]
    y_f = s5_scan_call(u2, tabs[0], n_lat, t, b, False, None)
    a_t = s5_scan_call(u2, tabs[1], n_lat, t, b, True,
                       (y_f, d_skip.reshape(1, GROUP_W), glu_w.astype(BF16), glu_b.reshape(1, GROUP_W)))

    (conv_w, conv_b, f_w1, f_b1, f_w2, f_b2, f_w3, f_freq, log_decay, bias) = hyp
    fh_l, fl_l, (fw_l, inv_l), wf_l = dft[0]
    kr, ki = hyena_filter_response(n_lat, fh_l, fl_l, wf_l, f_w1, f_b1, f_w2, f_b2, f_w3, f_freq, log_decay)
    b_l = hyena_call(p_h, n_lat, 0, conv_w, conv_b, fw_l, inv_l, kr, ki, bias)
    b_c = None
    if keep_ctx:
        fh_c, fl_c, (fw_c, inv_c), wf_c = dft[1]
        kr, ki = hyena_filter_response(n_ctx, fh_c, fl_c, wf_c, f_w1, f_b1, f_w2, f_b2, f_w3, f_freq,
                                       log_decay)
        b_c = hyena_call(p_h, n_ctx, n_lat // n_ctx, conv_w, conv_b, fw_c, inv_c, kr, ki, bias)
    return a_t.reshape(t, b * GROUP_W), b_l, b_c


def _odd_mixers(xs, mods, keep_ctx, n_lat, lam_init, norm1_g, w_in, rwp, dap, rope):
    b, t, _ = xs.shape
    n_rows = t if keep_ctx else n_lat
    (p,) = in_proj(xs, norm1_g, mods, w_in.astype(BF16), n_lat, 0)

    (conv_w, w0, w_up, a0, a_up, g_up, k_k, k_a, r_k, ln_g, ln_b) = rwp
    rkv = short_conv(p, conv_w, n_lat)
    decay, a_lr, gate = rwkv_lora(p, w_up, a_up, g_up, w0, a0)
    yf, yb = rwkv_scan(_to_scan_layout(decay), _to_scan_layout(a_lr), _to_scan_layout(rkv),
                       _chain_const(k_k, b), _chain_const(k_a, b), n_lat)
    a_m = rwkv_post(_from_scan_layout(yf, yb, b), rkv, a_lr, gate, k_a, r_k, ln_g, ln_b, n_rows)

    lam_p, subln_g = dap
    b_m = diff_attention(p, RW_IN, rope[0], rope[1], lam_p, subln_g, lam_init, n_lat, n_rows)
    return a_m, b_m, None


def kernel(x, c, ctx, c_ctx, ada_w, ada_b, norm1_g, norm2_g, mlp_w1, mlp_w2, final_g, ev_w_in, ev_w_out, s5_lam_re, s5_lam_im, s5_log_dt, s5_b_re, s5_b_im, s5_c_re, s5_c_im, s5_d, s5_glu_w, s5_glu_b, hy_conv_w, hy_conv_b, hy_f_w1, hy_f_b1, hy_f_w2, hy_f_b2, hy_f_w3, hy_f_freq, hy_log_decay, hy_bias, od_w_in, od_w_out, rw_conv_w, rw_w0, rw_w_up, rw_a0, rw_a_up, rw_g_up, rw_k_k, rw_k_a, rw_r_k, rw_ln_g, rw_ln_b, da_lam, da_subln_g):
    b, n_lat, d = x.shape
    n_ctx = ctx.shape[1]
    assert d == D_MODEL and b == SUBLANE
    assert n_lat % TOKEN_TILE == 0 and n_ctx % TOKEN_TILE == 0 and n_lat % n_ctx == 0
    xs = jnp.concatenate([x.astype(F32), ctx.astype(F32)], axis=1)

    c_rows = jnp.zeros((16, d), F32).at[:b].set(c.astype(F32)).at[b].set(c_ctx.astype(F32))
    depth = ada_w.shape[0]
    mods_all = ada_mods(c_rows, ada_w, ada_b).reshape(depth, 16, 6, d)

    rope = rope_tables(n_lat, n_ctx)
    dft = []
    for n in (n_lat, n_ctx):
        fwd, wf = dft_tables(n)
        hi, lo = _split_bf16(fwd)
        dft.append((hi, lo, dft_blocks(hi), wf))

    for l in range(depth):
        keep_ctx = l < depth - 1
        i = l // 2
        mods = mods_all[l]
        if l % 2 == 0:
            s5p = (s5_lam_re[i], s5_lam_im[i], s5_log_dt[i], s5_b_re[i], s5_b_im[i], s5_c_re[i], s5_c_im[i],
                   s5_d[i], s5_glu_w[i], s5_glu_b[i])
            hyp = (hy_conv_w[i], hy_conv_b[i], hy_f_w1[i], hy_f_b1[i], hy_f_w2[i], hy_f_b2[i], hy_f_w3[i],
                   hy_f_freq[i], hy_log_decay[i], hy_bias[i])
            a_m, b_m, b_c = _even_mixers(xs, mods, keep_ctx, n_lat, norm1_g[l], ev_w_in[i], s5p, hyp, dft)
            w_out = ev_w_out[i]
        else:
            rwp = (rw_conv_w[i], rw_w0[i], rw_w_up[i], rw_a0[i], rw_a_up[i], rw_g_up[i], rw_k_k[i],
                   rw_k_a[i], rw_r_k[i], rw_ln_g[i], rw_ln_b[i])
            lam_init = 0.8 - 0.6 * math.exp(-0.3 * l)
            a_m, b_m, b_c = _odd_mixers(xs, mods, keep_ctx, n_lat, lam_init, norm1_g[l], od_w_in[i], rwp,
                                        (da_lam[i], da_subln_g[i]), rope)
            w_out = od_w_out[i]
        n_rows = n_lat + n_ctx if keep_ctx else n_lat
        xs = mix_mlp(xs, a_m, b_m, b_c, w_out.astype(BF16), norm2_g[l], mods, mlp_w1[l].astype(BF16),
                     mlp_w2[l].astype(BF16), final_g, n_lat, n_rows, l % 2 == 0, l == depth - 1)
    return xs
```

```python
import functools
import math

import jax
import jax.numpy as jnp
from jax import lax
from jax.experimental import pallas as pl
from jax.experimental.pallas import tpu as pltpu

F32 = jnp.float32
BF16 = jnp.bfloat16
HIGHEST = lax.Precision.HIGHEST

D_MODEL = 1024
DEPTH = 4
GRID_W = 64
MLP_HIDDEN = 4 * D_MODEL
GROUP_W = D_MODEL // 2
EPS = 1e-6

S5_GROUP = 16
S5_GROUPS = GROUP_W // S5_GROUP
S5_STATE = 64
S5_W = S5_GROUPS * S5_STATE

HY_ORDER = 2
HY_EMB = 33
HY_BANDS = (HY_EMB - 1) // 2
SHORT_K = 3

RW_HEAD = 64
RW_HEADS = GROUP_W // RW_HEAD
RW_LORA_W = 256
RW_LN_EPS = 64e-5
RW_IN = 3 * GROUP_W + RW_LORA_W

DA_HEADS = 4
DA_HEAD = 64
DA_V = 2 * DA_HEAD
DA_SCALE = DA_HEAD ** -0.5
DA_SUBLN_EPS = 1e-5
ROPE_BASE = 10000.0
ROPE_FREQS = DA_HEAD // 4

LANE = 128
SUBLANE = 8
TOKEN_TILE = 256
VMEM_LIMIT = 48 * 1024 * 1024


def _params(*sem):
    return pltpu.CompilerParams(dimension_semantics=sem, vmem_limit_bytes=VMEM_LIMIT)


def _split_bf16(x):
    hi = x.astype(BF16)
    lo = (x - hi.astype(F32)).astype(BF16)
    return hi, lo


def _dot(a, b):
    return jnp.dot(a, b, preferred_element_type=F32)


def _ada_kernel(c_ref, w_ref, b_ref, o_ref):
    c = c_ref[...]
    s = c * jax.nn.sigmoid(c)
    o_ref[0] = jnp.dot(s, w_ref[0], preferred_element_type=F32, precision=HIGHEST) + b_ref[0]


def ada_mods(c_rows, ada_w, ada_b):
    depth, d, n = ada_w.shape
    tn = 1024
    return pl.pallas_call(
        _ada_kernel,
        grid=(depth, n // tn),
        in_specs=[pl.BlockSpec((16, d), lambda l, j: (0, 0)),
                  pl.BlockSpec((1, d, tn), lambda l, j: (l, 0, j)),
                  pl.BlockSpec((1, 1, tn), lambda l, j: (l, 0, j))],
        out_specs=pl.BlockSpec((1, 16, tn), lambda l, j: (l, 0, j)),
        out_shape=jax.ShapeDtypeStruct((depth, 16, n), F32),
        compiler_params=_params("arbitrary", "arbitrary"),
        name="ada_mods",
    )(c_rows, ada_w, ada_b.reshape(depth, 1, n))


def _norm_mod(x, g, ml, mc, is_ctx, k):
    ms = jnp.mean(x * x, axis=-1, keepdims=True)
    xn = x * lax.rsqrt(ms + EPS) * g
    m = jnp.where(is_ctx, mc, ml)
    return xn * (1.0 + m[k + 1:k + 2]) + m[k:k + 1]


def _inproj_kernel(x_ref, g_ref, ml_ref, mc_ref, w_ref, *o_refs, n_lat_tiles, n_tmaj):
    i = pl.program_id(1)
    h = _norm_mod(x_ref[0], g_ref[...], ml_ref[0], mc_ref[0], i >= n_lat_tiles, 0)
    r = _dot(h.astype(BF16), w_ref[...])
    if n_tmaj:
        o_refs[0][...] = r[:, :n_tmaj]
        o_refs[1][0] = r[:, n_tmaj:]
    else:
        o_refs[0][0] = r


def in_proj(x, g, mods, w, n_lat, n_tmaj):
    b, t, d = x.shape
    n = w.shape[1]
    tm = TOKEN_TILE
    if n_tmaj:
        out_shape = [jax.ShapeDtypeStruct((t, b * n_tmaj), F32), jax.ShapeDtypeStruct((b, t, n - n_tmaj), F32)]
        out_specs = [pl.BlockSpec((tm, n_tmaj), lambda bb, i: (i, bb)),
                     pl.BlockSpec((1, tm, n - n_tmaj), lambda bb, i: (bb, i, 0))]
    else:
        out_shape = [jax.ShapeDtypeStruct((b, t, n), F32)]
        out_specs = [pl.BlockSpec((1, tm, n), lambda bb, i: (bb, i, 0))]
    return pl.pallas_call(
        functools.partial(_inproj_kernel, n_lat_tiles=n_lat // tm, n_tmaj=n_tmaj),
        grid=(b, t // tm),
        in_specs=[pl.BlockSpec((1, tm, d), lambda bb, i: (bb, i, 0)),
                  pl.BlockSpec((1, d), lambda bb, i: (0, 0)),
                  pl.BlockSpec((1, 6, d), lambda bb, i: (bb, 0, 0)),
                  pl.BlockSpec((1, 6, d), lambda bb, i: (b, 0, 0)),
                  pl.BlockSpec((d, n), lambda bb, i: (0, 0), pipeline_mode=pl.Buffered(1))],
        out_specs=out_specs,
        out_shape=out_shape,
        compiler_params=_params("arbitrary", "arbitrary"),
        name="in_proj",
    )(x, g.reshape(1, d), mods, mods, w)


MLP_CHUNK = 1024


def _mix_mlp_kernel(x_ref, a_ref, b_ref, bc_ref, wa_ref, wb_ref, g_ref, ml_ref, mc_ref, w1_ref, w2_ref, fg_ref,
                    o_ref, acc_ref, *, n_lat_tiles, a_tmaj, b_split, final):
    i = pl.program_id(1)
    is_ctx = i >= n_lat_tiles
    m = jnp.where(is_ctx, mc_ref[0], ml_ref[0])
    a = a_ref[...] if a_tmaj else a_ref[0]
    bmix = jnp.where(is_ctx, bc_ref[0], b_ref[0]) if b_split else b_ref[0]
    mix = _dot(a.astype(BF16), wa_ref[...]) + _dot(bmix.astype(BF16), wb_ref[...])
    x = x_ref[0] + m[2:3] * mix
    h = _norm_mod(x, g_ref[...], ml_ref[0], mc_ref[0], is_ctx, 3).astype(BF16)
    for k in range(w1_ref.shape[1] // MLP_CHUNK):
        ks = slice(k * MLP_CHUNK, (k + 1) * MLP_CHUNK)
        hid = jnp.maximum(_dot(h, w1_ref[:, ks]), 0.0)
        part = _dot((hid * hid).astype(BF16), w2_ref[ks, :])
        if k == 0:
            acc_ref[...] = part
        else:
            acc_ref[...] += part
    y = x + m[5:6] * acc_ref[...]
    if final:
        ms = jnp.mean(y * y, axis=-1, keepdims=True)
        y = y * lax.rsqrt(ms + EPS) * fg_ref[...]
    o_ref[0] = y


def mix_mlp(x, a, bm, bm_ctx, w_out, g, mods, w1, w2, final_g, n_lat, n_rows, a_tmaj, final):
    b, t, d = x.shape
    wd = bm.shape[-1]
    hid = w1.shape[1]
    tm = TOKEN_TILE
    nl = n_lat // tm
    once = pl.Buffered(1)
    if a_tmaj:
        a_spec = pl.BlockSpec((tm, wd), lambda bb, i: (i, bb))
    else:
        a_spec = pl.BlockSpec((1, tm, wd), lambda bb, i: (bb, i, 0))
    b_split = bm_ctx is not None
    if b_split:
        b_spec = pl.BlockSpec((1, tm, wd), lambda bb, i: (bb, jnp.minimum(i, nl - 1), 0))
        bc_spec = pl.BlockSpec((1, tm, wd), lambda bb, i: (bb, jnp.maximum(i - nl, 0), 0))
    else:
        b_spec = pl.BlockSpec((1, tm, wd), lambda bb, i: (bb, i, 0))
        bc_spec = pl.BlockSpec((1, tm, wd), lambda bb, i: (0, 0, 0))
        bm_ctx = bm
    return pl.pallas_call(
        functools.partial(_mix_mlp_kernel, n_lat_tiles=nl, a_tmaj=a_tmaj, b_split=b_split, final=final),
        grid=(b, n_rows // tm),
        in_specs=[pl.BlockSpec((1, tm, d), lambda bb, i: (bb, i, 0)),
                  a_spec,
                  b_spec,
                  bc_spec,
                  pl.BlockSpec((wd, d), lambda bb, i: (0, 0), pipeline_mode=once),
                  pl.BlockSpec((wd, d), lambda bb, i: (1, 0), pipeline_mode=once),
                  pl.BlockSpec((1, d), lambda bb, i: (0, 0)),
                  pl.BlockSpec((1, 6, d), lambda bb, i: (bb, 0, 0)),
                  pl.BlockSpec((1, 6, d), lambda bb, i: (b, 0, 0)),
                  pl.BlockSpec((d, hid), lambda bb, i: (0, 0), pipeline_mode=once),
                  pl.BlockSpec((hid, d), lambda bb, i: (0, 0), pipeline_mode=once),
                  pl.BlockSpec((1, d), lambda bb, i: (0, 0))],
        out_specs=pl.BlockSpec((1, tm, d), lambda bb, i: (bb, i, 0)),
        out_shape=jax.ShapeDtypeStruct((b, n_rows, d), F32),
        scratch_shapes=[pltpu.VMEM((tm, d), F32)],
        compiler_params=_params("arbitrary", "arbitrary"),
        name="mix_mlp",
    )(x, a, bm, bm_ctx, w_out, w_out, g.reshape(1, d), mods, mods, w1, w2, final_g.reshape(1, d))


def _conv3(x, w, first, last):
    rows = x.shape[0]
    xm = jnp.where(first, 0.0, pltpu.roll(x, 1, 0))
    xp = jnp.where(last, 0.0, pltpu.roll(x, rows - 1, 0))
    return xm * w[0:1] + x * w[1:2] + xp * w[2:3]


def _shortconv_kernel(x_ref, w_ref, o_ref, *, n_lat):
    x = x_ref[0]
    t = x.shape[0]
    row = lax.broadcasted_iota(jnp.int32, (t, 1), 0)
    first = (row == 0) | (row == n_lat)
    last = (row == n_lat - 1) | (row == t - 1)
    o_ref[0, 0] = _conv3(x, w_ref[...], first, last)


def short_conv(p, w, n_lat):
    b, t, _ = p.shape
    c = w.shape[1]
    cb = GROUP_W
    return pl.pallas_call(
        functools.partial(_shortconv_kernel, n_lat=n_lat),
        grid=(b, c // cb),
        in_specs=[pl.BlockSpec((1, t, cb), lambda bb, j: (bb, 0, j)),
                  pl.BlockSpec((SHORT_K, cb), lambda bb, j: (0, j))],
        out_specs=pl.BlockSpec((1, 1, t, cb), lambda bb, j: (j, bb, 0, 0)),
        out_shape=jax.ShapeDtypeStruct((c // cb, b, t, cb), F32),
        compiler_params=_params("arbitrary", "arbitrary"),
        name="short_conv",
    )(p, w)


S5_KB = LANE
S5_NB = GROUP_W // S5_KB
S5_SB = S5_W // S5_NB


def _s5_kernel(*refs, tc, nb, reverse, finish):
    if finish:
        (u_ref, wbr_ref, wbi_ref, lr_ref, li_ref, wcr_ref, wci_ref, yf_ref, d_ref, gw_ref, gb_ref,
         o_ref, hr_s, hi_s, sr_s, si_s) = refs
    else:
        (u_ref, wbr_ref, wbi_ref, lr_ref, li_ref, wcr_ref, wci_ref,
         o_ref, hr_s, hi_s, sr_s, si_s) = refs

    @pl.when(pl.program_id(0) == 0)
    def _():
        sr_s[...] = jnp.zeros_like(sr_s)
        si_s[...] = jnp.zeros_like(si_s)

    u = u_ref[...]
    ub = u.astype(BF16)
    for j in range(S5_NB):
        uj = ub[:, j * S5_KB:(j + 1) * S5_KB]
        hr_s[:, j * S5_SB:(j + 1) * S5_SB] = _dot(uj, wbr_ref[j])
        hi_s[:, j * S5_SB:(j + 1) * S5_SB] = _dot(uj, wbi_ref[j])

    for s in range(S5_NB):
        sl = slice(s * S5_SB, (s + 1) * S5_SB)
        lr = jnp.broadcast_to(lr_ref[:, sl], (nb, S5_SB))
        li = jnp.broadcast_to(li_ref[:, sl], (nb, S5_SB))

        def body(k, carry, sl=sl, lr=lr, li=li):
            hr, hi = carry
            t = (tc - 1 - k) if reverse else k
            r0 = pl.multiple_of(t * nb, nb)
            nr = lr * hr - li * hi + hr_s[pl.ds(r0, nb), sl]
            ni = lr * hi + li * hr + hi_s[pl.ds(r0, nb), sl]
            hr_s[pl.ds(r0, nb), sl] = nr
            hi_s[pl.ds(r0, nb), sl] = ni
            return nr, ni

        hr, hi = lax.fori_loop(0, tc, body, (sr_s[:, sl], si_s[:, sl]), unroll=4)
        sr_s[:, sl] = hr
        si_s[:, sl] = hi

    for j in range(S5_NB):
        sl = slice(j * S5_SB, (j + 1) * S5_SB)
        co = slice(j * S5_KB, (j + 1) * S5_KB)
        y = _dot(hr_s[:, sl].astype(BF16), wcr_ref[j]) + _dot(hi_s[:, sl].astype(BF16), wci_ref[j])
        if finish:
            o_ref[:, co] = y + yf_ref[:, co] + u[:, co] * d_ref[:, co]
        else:
            o_ref[:, co] = y

    if finish:
        y = o_ref[...]
        y = 0.5 * y * (1.0 + lax.erf(y * (2.0 ** -0.5)))
        z = _dot(y.astype(BF16), gw_ref[...]) + gb_ref[...]
        o_ref[...] = y * jax.nn.sigmoid(z)


def s5_scan_call(u2, tabs, n_lat_steps, n_steps, nb, reverse, extra):
    tc = 32
    rows = tc * nb
    n_chunks = n_steps // tc
    n_lat_chunks = n_lat_steps // tc
    n_ctx_chunks = n_chunks - n_lat_chunks
    if reverse:
        def cmap(c):
            return (n_chunks - 1 - c, 0)
    else:
        def cmap(c):
            return (jnp.where(c < n_ctx_chunks, n_lat_chunks + c, c - n_ctx_chunks), 0)
    wbr, wbi, lr, li, wcr, wci = tabs
    full3 = lambda a: pl.BlockSpec(a.shape, lambda c: (0, 0, 0))
    full2 = lambda a: pl.BlockSpec(a.shape, lambda c: (0, 0))
    in_specs = [pl.BlockSpec((rows, GROUP_W), cmap), full3(wbr), full3(wbi), full2(lr), full2(li),
                full3(wcr), full3(wci)]
    args = [u2, wbr, wbi, lr, li, wcr, wci]
    finish = extra is not None
    if finish:
        yf, dsk, gw, gb = extra
        in_specs += [pl.BlockSpec((rows, GROUP_W), cmap), full2(dsk), full2(gw), full2(gb)]
        args += [yf, dsk, gw, gb]
    return pl.pallas_call(
        functools.partial(_s5_kernel, tc=tc, nb=nb, reverse=reverse, finish=finish),
        grid=(n_chunks,),
        in_specs=in_specs,
        out_specs=pl.BlockSpec((rows, GROUP_W), cmap),
        out_shape=jax.ShapeDtypeStruct(u2.shape, F32),
        scratch_shapes=[pltpu.VMEM((rows, S5_W), F32), pltpu.VMEM((rows, S5_W), F32),
                        pltpu.VMEM((nb, S5_W), F32), pltpu.VMEM((nb, S5_W), F32)],
        compiler_params=_params("arbitrary"),
        name="s5_bwd_finish" if finish else "s5_fwd",
    )(*args)


def _block_diag(x):
    nblk, g, r, c = x.shape
    eye = jnp.eye(g, dtype=x.dtype)
    return jnp.einsum('jgrc,gh->jgrhc', x, eye).reshape(nblk, g * r, g * c)


def s5_tables(lam_re, lam_im, log_dt, b_re, b_im, c_re, c_im):
    lam_re = jnp.minimum(lam_re.astype(F32), -1e-4)
    lam_im = lam_im.astype(F32)
    dt = jnp.exp(log_dt.astype(F32))[:, None]
    mag = jnp.exp(lam_re * dt)
    lb_re = mag * jnp.cos(lam_im * dt)
    lb_im = mag * jnp.sin(lam_im * dt)
    den = lam_re * lam_re + lam_im * lam_im
    f_re = ((lb_re - 1.0) * lam_re + lb_im * lam_im) / den
    f_im = (lb_im * lam_re - (lb_re - 1.0) * lam_im) / den
    b_re = b_re.astype(F32)
    b_im = b_im.astype(F32)
    bb_re = f_re[..., None] * b_re - f_im[..., None] * b_im
    bb_im = f_re[..., None] * b_im + f_im[..., None] * b_re
    gpb = S5_KB // S5_GROUP
    to_b = lambda x: _block_diag(
        jnp.swapaxes(x, 1, 2).reshape(S5_NB, gpb, S5_GROUP, S5_STATE)).astype(BF16)
    to_c = lambda x: _block_diag(
        jnp.swapaxes(x.astype(F32), 1, 2).reshape(S5_NB, gpb, S5_STATE, S5_GROUP)).astype(BF16)
    return (to_b(bb_re), to_b(bb_im), lb_re.reshape(1, S5_W), lb_im.reshape(1, S5_W),
            to_c(c_re), to_c(-c_im.astype(F32)))


def dft_tables(n):
    nn = 2 * n
    f = jnp.arange(n, dtype=jnp.int32)[:, None]
    t = jnp.arange(n, dtype=jnp.int32)[None, :]
    q = 1 << (int(math.log2(n)) // 2)

    def factor(tt):
        ang = ((f * tt[None, :]) % nn).astype(F32) * (2.0 * math.pi / nn)
        return jnp.cos(ang), jnp.sin(ang)

    c1, s1 = factor(jnp.arange(n // q, dtype=jnp.int32) * q)
    c0, s0 = factor(jnp.arange(q, dtype=jnp.int32))
    cos = (c1[:, :, None] * c0[:, None, :] - s1[:, :, None] * s0[:, None, :]).reshape(n, n)
    sin = (s1[:, :, None] * c0[:, None, :] + c1[:, :, None] * s0[:, None, :]).reshape(n, n)
    nyq = jnp.where(t % 2 == 0, 1.0, -1.0).astype(F32)
    msin = jnp.where(f == 0, nyq, -sin)
    fwd = jnp.concatenate([cos, msin], axis=0)
    wf = jnp.where(jnp.arange(n) == 0, 1.0 / nn, 2.0 / nn).astype(F32)
    return fwd, wf


def _filtdft_kernel(fh_ref, fl_ref, hh_ref, hl_ref, o_ref):
    fh = fh_ref[...]
    o_ref[...] = _dot(fh, hh_ref[0]) + _dot(fh, hl_ref[0]) + _dot(fl_ref[...], hh_ref[0])


def filter_dft(fwd_hi, fwd_lo, h_sum, h_dif):
    n2, n = fwd_hi.shape
    c = h_sum.shape[1]
    tr = min(256, n)
    tcn = 512
    nrb = n // tr
    hh, hl = _split_bf16(jnp.stack([h_sum, h_dif]))
    frow = lambda i, j: (jnp.where(i == 2 * nrb, nrb, i), 0)
    hsel = lambda i, j: (jnp.where((i >= nrb) & (i < 2 * nrb), 1, 0), 0, j)
    return pl.pallas_call(
        _filtdft_kernel,
        grid=(2 * nrb + 1, c // tcn),
        in_specs=[pl.BlockSpec((tr, n), frow),
                  pl.BlockSpec((tr, n), frow),
                  pl.BlockSpec((1, n, tcn), hsel),
                  pl.BlockSpec((1, n, tcn), hsel)],
        out_specs=pl.BlockSpec((tr, tcn), lambda i, j: (i, j)),
        out_shape=jax.ShapeDtypeStruct((n2 + tr, c), F32),
        compiler_params=_params("arbitrary", "arbitrary"),
        name="filter_dft",
    )(fwd_hi, fwd_lo, hh, hl)


def hyena_filter_response(n, fwd_hi, fwd_lo, wf, f_w1, f_b1, f_w2, f_b2, f_w3, f_freq, log_decay):
    t = jnp.linspace(0.0, 1.0, n, dtype=F32)[:, None]
    w = 2.0 * math.pi * jnp.arange(n, dtype=F32)[:, None] / n
    bands = jnp.linspace(1e-4, HY_BANDS - 1, HY_BANDS, dtype=F32)[None, :]
    z = jnp.concatenate([t, jnp.cos(bands * w), -jnp.sin(bands * w)], axis=-1)
    freq = f_freq.astype(F32)
    hid = jnp.sin(freq[0] * (jnp.dot(z, f_w1.astype(F32), precision=HIGHEST) + f_b1.astype(F32)))
    hid = jnp.sin(freq[1] * (jnp.dot(hid, f_w2.astype(F32), precision=HIGHEST) + f_b2.astype(F32)))
    h = jnp.dot(hid, f_w3.astype(F32), precision=HIGHEST).reshape(n, 2, HY_ORDER, GROUP_W)
    h = h * jnp.exp(-t[:, :, None, None] * jnp.exp(log_decay.astype(F32)))
    cw = HY_ORDER * GROUP_W
    h_fwd = h[:, 0].reshape(n, cw)
    h_bwd = h[:, 1].reshape(n, cw).at[0].set(0.0)
    resp = filter_dft(fwd_hi, fwd_lo, h_fwd + h_bwd, h_fwd - h_bwd)
    k_re = resp[:n]
    first = (jnp.arange(n) == 0)[:, None]
    k_im = jnp.where(first, resp[2 * n:2 * n + 1], resp[n:2 * n])
    k_re = (k_re * wf[:, None]).reshape(n, HY_ORDER, GROUP_W).transpose(1, 0, 2)
    k_im = (k_im * wf[:, None]).reshape(n, HY_ORDER, GROUP_W).transpose(1, 0, 2)
    return k_re, k_im


def _hyena_kernel(z0_ref, x1_ref, x2_ref, cw_ref, cb_ref, fw_ref, inv_ref, kr_ref, ki_ref, bias_ref,
                  o_ref, zb_s, zf_s, acc_s):
    o = pl.program_id(1)
    f = pl.program_id(2)
    nf = pl.num_programs(2)
    c = GROUP_W

    def stream(ref, s):
        x = ref[0]
        row = lax.broadcasted_iota(jnp.int32, (x.shape[0], 1), 0)
        cols = slice(s * c, (s + 1) * c)
        return _conv3(x, cw_ref[:, cols], row == 0, row == x.shape[0] - 1) + cb_ref[:, cols]

    @pl.when((o == 0) & (f == 0))
    def _():
        z0 = stream(z0_ref, 0)
        zf_s[...] = z0
        zb_s[...] = z0.astype(BF16)

    @pl.when(f == 0)
    def _():
        acc_s[...] = jnp.zeros_like(acc_s)

    xf = _dot(fw_ref[0], zb_s[...])
    fb = xf.shape[0] // 2
    xr = xf[:fb]
    xi = xf[fb:]
    kr = kr_ref[0]
    ki = ki_ref[0]
    row = lax.broadcasted_iota(jnp.int32, (fb, 1), 0)
    packed = (row == 0) & (f == 0)
    yr = jnp.where(packed, xr * kr, xr * kr - xi * ki)
    yi = jnp.where(packed, xi * ki, xr * ki + xi * kr)
    yf = jnp.concatenate([yr, yi], axis=0).astype(BF16)
    acc_s[...] += _dot(inv_ref[0], yf)

    @pl.when(f == nf - 1)
    def _():
        bias = bias_ref[...]

        @pl.when(o == 0)
        def _():
            z1 = stream(x1_ref, 1) * (acc_s[...] + zf_s[...] * bias[0:1])
            zf_s[...] = z1
            zb_s[...] = z1.astype(BF16)

        @pl.when(o == 1)
        def _():
            o_ref[0] = stream(x2_ref, 2) * (acc_s[...] + zf_s[...] * bias[1:2])


HY_FREQ_BLOCK = 256


def dft_blocks(fwd_hi):
    n = fwd_hi.shape[1]
    fb = min(HY_FREQ_BLOCK, n)
    blocks = fwd_hi.reshape(2, n // fb, fb, n).transpose(1, 0, 2, 3).reshape(n // fb, 2 * fb, n)
    return blocks, blocks.transpose(0, 2, 1)


def hyena_call(p, n, row_blk, conv_w, conv_b, fwd_blk, inv_blk, k_re, k_im, bias):
    b = p.shape[0]
    c = GROUP_W
    nf, fb2, _ = fwd_blk.shape
    fb = fb2 // 2
    once = pl.Buffered(1)
    in_specs = [pl.BlockSpec((1, n, c), lambda bb, o, f: (bb, row_blk, 0), pipeline_mode=once),
                pl.BlockSpec((1, n, c), lambda bb, o, f: (bb, row_blk, 1), pipeline_mode=once),
                pl.BlockSpec((1, n, c), lambda bb, o, f: (bb, row_blk, 2), pipeline_mode=once),
                pl.BlockSpec((SHORT_K, 3 * c), lambda bb, o, f: (0, 0)),
                pl.BlockSpec((1, 3 * c), lambda bb, o, f: (0, 0)),
                pl.BlockSpec((1, fb2, n), lambda bb, o, f: (f, 0, 0)),
                pl.BlockSpec((1, n, fb2), lambda bb, o, f: (f, 0, 0)),
                pl.BlockSpec((1, fb, c), lambda bb, o, f: (o, f, 0)),
                pl.BlockSpec((1, fb, c), lambda bb, o, f: (o, f, 0)),
                pl.BlockSpec((HY_ORDER, c), lambda bb, o, f: (0, 0))]
    return pl.pallas_call(
        _hyena_kernel,
        grid=(b, HY_ORDER, nf),
        in_specs=in_specs,
        out_specs=pl.BlockSpec((1, n, c), lambda bb, o, f: (bb, 0, 0)),
        out_shape=jax.ShapeDtypeStruct((b, n, c), F32),
        scratch_shapes=[pltpu.VMEM((n, c), BF16), pltpu.VMEM((n, c), F32), pltpu.VMEM((n, c), F32)],
        compiler_params=_params("arbitrary", "arbitrary", "arbitrary"),
        name="hyena",
    )(p, p, p, conv_w, conv_b.reshape(1, 3 * c), fwd_blk, inv_blk, k_re, k_im, bias)


def _rwkv_lora_kernel(p_ref, ww_ref, wa_ref, wg_ref, w0_ref, a0_ref, wp_ref, ap_ref, g_ref):
    x = p_ref[0]
    wa_in = x[:, :LANE]
    th = jnp.tanh(wa_in).astype(BF16)
    lin = wa_in.astype(BF16)
    w0 = w0_ref[...]
    a0 = a0_ref[...]
    for d in range(2):
        wl = -jax.nn.softplus(-(w0[d:d + 1] + _dot(th, ww_ref[d]))) - 0.5
        wp_ref[d, 0] = jnp.exp(-jnp.exp(wl))
        ap_ref[d, 0] = jax.nn.sigmoid(a0[d:d + 1] + _dot(lin, wa_ref[d]))
    g_ref[0] = _dot(jax.nn.sigmoid(x[:, LANE:]).astype(BF16), wg_ref[...])


def rwkv_lora(p, w_up, a_up, g_up, w0, a0):
    b, t, _ = p.shape
    c = GROUP_W
    tm = TOKEN_TILE
    zeros = jnp.zeros((2, 64, c), F32)
    ww = jnp.concatenate([w_up.astype(F32), zeros], axis=1).astype(BF16)
    wa = jnp.concatenate([zeros, a_up.astype(F32)], axis=1).astype(BF16)
    both = jax.ShapeDtypeStruct((2, b, t, c), F32)
    both_spec = pl.BlockSpec((2, 1, tm, c), lambda bb, i: (0, bb, i, 0))
    return pl.pallas_call(
        _rwkv_lora_kernel,
        grid=(b, t // tm),
        in_specs=[pl.BlockSpec((1, tm, RW_LORA_W), lambda bb, i: (bb, i, 3 * c // RW_LORA_W)),
                  pl.BlockSpec((2, LANE, c), lambda bb, i: (0, 0, 0)),
                  pl.BlockSpec((2, LANE, c), lambda bb, i: (0, 0, 0)),
                  pl.BlockSpec((LANE, c), lambda bb, i: (0, 0)),
                  pl.BlockSpec((2, c), lambda bb, i: (0, 0)),
                  pl.BlockSpec((2, c), lambda bb, i: (0, 0))],
        out_specs=[both_spec, both_spec, pl.BlockSpec((1, tm, c), lambda bb, i: (bb, i, 0))],
        out_shape=[both, both, jax.ShapeDtypeStruct((b, t, c), F32)],
        compiler_params=_params("arbitrary", "arbitrary"),
        name="rwkv_lora",
    )(p, ww, wa, g_up.astype(BF16), w0, a0)


SCAN_UNROLL = 16


def _rwkv_scan_kernel(wpf_ref, wpb_ref, apf_ref, apb_ref, kf_ref, kb_ref, vf_ref, vb_ref, rf_ref, rb_ref,
                      kkc_ref, kac_ref, yf_ref, yb_ref, s_ref, w_s, kk_s, b_s, ke_s, r_s, v_s, *, tb):
    n = RW_HEAD

    @pl.when(pl.program_id(0) == 0)
    def _():
        s_ref[...] = jnp.zeros_like(s_ref)

    def both(f_ref, b_ref, t):
        return jnp.concatenate([f_ref[0, t], b_ref[0, tb - 1 - t]], axis=-1)

    def prepare(t, carry):
        k = both(kf_ref, kb_ref, t)
        a = both(apf_ref, apb_ref, t)
        kk = k * kkc_ref[...]
        nrm = jnp.sqrt(jnp.sum(kk * kk, axis=0, keepdims=True))
        kk = kk / jnp.maximum(nrm, 1e-12)
        w_s[t] = both(wpf_ref, wpb_ref, t)
        kk_s[t] = kk
        b_s[t] = kk * a
        ke_s[t] = k * (1.0 + (a - 1.0) * kac_ref[...])
        r_s[t] = both(rf_ref, rb_ref, t)
        v_s[t] = both(vf_ref, vb_ref, t)
        return carry

    lax.fori_loop(0, tb, prepare, 0, unroll=8)
    sa0 = jnp.zeros((n, LANE), F32)
    for j in range(n):
        sa0 = sa0 + s_ref[j] * kk_s[0, j:j + 1, :]

    def step(t, sa):
        nxt = jnp.minimum(t + 1, tb - 1)
        v = v_s[t]

        def keys(j, carry):
            y, sa_next = carry
            row = pl.ds(j, 1)
            s = s_ref[j] * w_s[t, row, :] - sa * b_s[t, row, :] + v * ke_s[t, row, :]
            s_ref[j] = s
            return y + s * r_s[t, row, :], sa_next + s * kk_s[nxt, row, :]

        zero = jnp.zeros((n, LANE), F32)
        y, sa_next = lax.fori_loop(0, n, keys, (zero, zero), unroll=SCAN_UNROLL)
        yf_ref[t] = y
        yb_ref[tb - 1 - t] = y
        return sa_next

    lax.fori_loop(0, tb, step, sa0)


def rwkv_scan(w, a, rkv, kkc, kac, n_lat):
    t = rkv.shape[1]
    tb = 32
    n = RW_HEAD
    half = LANE // 2
    n_blk = t // tb
    n_lat_blk = n_lat // tb
    n_ctx_blk = n_blk - n_lat_blk

    def fblk(c):
        return jnp.where(c < n_ctx_blk, n_lat_blk + c, c - n_ctx_blk)

    def rblk(c):
        return n_blk - 1 - c

    def pair(qf, qb):
        return [pl.BlockSpec((1, tb, n, half), lambda c: (qf, fblk(c), 0, 0)),
                pl.BlockSpec((1, tb, n, half), lambda c: (qb, rblk(c), 0, 0))]

    cst = pl.BlockSpec((n, LANE), lambda c: (0, 0))
    buf = pltpu.VMEM((tb, n, LANE), F32)
    out = jax.ShapeDtypeStruct((t, n, LANE), F32)
    return pl.pallas_call(
        functools.partial(_rwkv_scan_kernel, tb=tb),
        grid=(n_blk,),
        in_specs=pair(0, 1) + pair(0, 1) + pair(1, 1) + pair(2, 2) + pair(0, 0) + [cst] * 2,
        out_specs=[pl.BlockSpec((tb, n, LANE), lambda c: (fblk(c), 0, 0)),
                   pl.BlockSpec((tb, n, LANE), lambda c: (rblk(c), 0, 0))],
        out_shape=[out, out],
        scratch_shapes=[pltpu.VMEM((n, n, LANE), F32), buf, buf, buf, buf, buf, buf],
        compiler_params=_params("arbitrary"),
        name="rwkv_scan",
    )(w, w, a, a, rkv, rkv, rkv, rkv, rkv, rkv, kkc, kac)


def _rwkv_post_kernel(y_ref, r_ref, k_ref, v_ref, ap_ref, g_ref, m_ref, ka_ref, rk_ref, lg_ref, lb_ref,
                      o_ref):
    m = m_ref[...]

    def head_mean(x):
        hi, lo = _split_bf16(x)
        return _dot(hi, m) + _dot(lo, m)

    y = y_ref[0]
    d = y - head_mean(y)
    var = head_mean(d * d)
    yn = d * lax.rsqrt(var + RW_LN_EPS) * lg_ref[...] + lb_ref[...]
    a_sum = ap_ref[0, 0] + ap_ref[1, 0]
    k_sum = k_ref[0, 0] * (2.0 + (a_sum - 2.0) * ka_ref[...])
    bonus = head_mean(r_ref[0, 0] * k_sum * rk_ref[...]) * float(RW_HEAD) * v_ref[0, 0]
    o_ref[0] = (yn + bonus) * g_ref[0]


def rwkv_post(y, rkv, ap, gate, k_a, r_k, ln_g, ln_b, n_rows):
    b, t, c = y.shape
    tm = TOKEN_TILE
    hm = jnp.kron(jnp.eye(RW_HEADS, dtype=F32), jnp.full((RW_HEAD, RW_HEAD), 1.0 / RW_HEAD, F32)).astype(BF16)
    tok = pl.BlockSpec((1, tm, c), lambda bb, i: (bb, i, 0))
    vec = pl.BlockSpec((1, c), lambda bb, i: (0, 0))
    rkv_spec = lambda q: pl.BlockSpec((1, 1, tm, c), lambda bb, i: (q, bb, i, 0))
    return pl.pallas_call(
        _rwkv_post_kernel,
        grid=(b, n_rows // tm),
        in_specs=[tok, rkv_spec(0), rkv_spec(1), rkv_spec(2),
                  pl.BlockSpec((2, 1, tm, c), lambda bb, i: (0, bb, i, 0)), tok,
                  pl.BlockSpec((c, c), lambda bb, i: (0, 0)), vec, vec, vec, vec],
        out_specs=tok,
        out_shape=jax.ShapeDtypeStruct((b, n_rows, c), F32),
        compiler_params=_params("arbitrary", "arbitrary"),
        name="rwkv_post",
    )(y, rkv, rkv, rkv, ap, gate, hm, k_a.reshape(1, c), r_k.reshape(1, c),
      ln_g.reshape(1, c), ln_b.reshape(1, c))


def _to_scan_layout(a):
    s, b, t, _ = a.shape
    x = a.reshape(s, b, t, RW_HEADS, RW_HEAD).transpose(0, 2, 4, 1, 3)
    return x.reshape(s, t, RW_HEAD, b * RW_HEADS)


def _from_scan_layout(yf, yb, b):
    t = yf.shape[0]
    half = b * RW_HEADS
    y = yf[..., :half] + yb[..., half:]
    return y.reshape(t, RW_HEAD, b, RW_HEADS).transpose(2, 0, 3, 1).reshape(b, t, GROUP_W)


def _chain_const(x, b):
    return jnp.tile(x.astype(F32).reshape(RW_HEADS, RW_HEAD).T, (1, 2 * b))


def _rope(x, cos, sin):
    lane = lax.broadcasted_iota(jnp.int32, (1, LANE), 1)
    first = (lane % (2 * ROPE_FREQS)) < ROPE_FREQS
    partner = jnp.where(first, pltpu.roll(x, LANE - ROPE_FREQS, 1), pltpu.roll(x, ROPE_FREQS, 1))
    return x * cos + partner * sin


def _attn_tile(q, k, v, lam):
    lane = lax.broadcasted_iota(jnp.int32, (1, LANE), 1)
    m0 = lane < DA_HEAD
    q0 = jnp.where(m0, q, 0.0).astype(BF16)
    q1 = jnp.where(m0, 0.0, q).astype(BF16)
    dn = (((1,), (1,)), ((), ()))
    s0 = lax.dot_general(q0, k, dn, preferred_element_type=F32)
    s1 = lax.dot_general(q1, k, dn, preferred_element_type=F32)
    p0 = jnp.exp2(s0 - jnp.max(s0, axis=-1, keepdims=True))
    p1 = jnp.exp2(s1 - jnp.max(s1, axis=-1, keepdims=True))
    c0 = (1.0 / jnp.sum(p0, axis=-1, keepdims=True)).astype(BF16)
    c1 = (lam / jnp.sum(p1, axis=-1, keepdims=True)).astype(BF16)
    return _dot(p0.astype(BF16) * c0 - p1.astype(BF16) * c1, v)


def _attn_kernel(q_ref, k_ref, v_ref, cq_ref, sq_ref, ck_ref, sk_ref, lp_ref, g_ref, o_ref,
                 kr_s, vb_s, *, n_lat, n_lat_tiles, lam_init):
    i = pl.program_id(2)

    @pl.when(i == 0)
    def _():
        kr_s[...] = _rope(k_ref[0], ck_ref[...], sk_ref[...]).astype(BF16)
        vb_s[...] = v_ref[0].astype(BF16)

    lp = lp_ref[...]
    lam = (jnp.exp(jnp.sum(lp[0:1] * lp[1:2], axis=-1, keepdims=True))
           - jnp.exp(jnp.sum(lp[2:3] * lp[3:4], axis=-1, keepdims=True)) + lam_init)
    q = _rope(q_ref[0], cq_ref[...], sq_ref[...]) * (DA_SCALE * math.log2(math.e))

    def finish(o):
        on = o * lax.rsqrt(jnp.mean(o * o, axis=-1, keepdims=True) + DA_SUBLN_EPS)
        o_ref[0] = on * g_ref[...] * (1.0 - lam_init)

    @pl.when(i < n_lat_tiles)
    def _():
        finish(_attn_tile(q, kr_s[...], vb_s[...], lam))

    @pl.when(i >= n_lat_tiles)
    def _():
        finish(_attn_tile(q, kr_s[n_lat:], vb_s[n_lat:], lam))


def diff_attention(p, col0, cosf, sins, lam_p, subln_g, lam_init, n_lat, n_rows):
    b, t, _ = p.shape
    tq = TOKEN_TILE
    off = col0 // LANE
    hq = DA_HEADS
    return pl.pallas_call(
        functools.partial(_attn_kernel, n_lat=n_lat, n_lat_tiles=n_lat // tq, lam_init=lam_init),
        grid=(b, hq, n_rows // tq),
        in_specs=[pl.BlockSpec((1, tq, LANE), lambda bb, h, i: (bb, i, off + h)),
                  pl.BlockSpec((1, t, LANE), lambda bb, h, i: (bb, 0, off + hq + h)),
                  pl.BlockSpec((1, t, LANE), lambda bb, h, i: (bb, 0, off + 2 * hq + h)),
                  pl.BlockSpec((tq, LANE), lambda bb, h, i: (i, 0)),
                  pl.BlockSpec((tq, LANE), lambda bb, h, i: (i, 0)),
                  pl.BlockSpec((t, LANE), lambda bb, h, i: (0, 0)),
                  pl.BlockSpec((t, LANE), lambda bb, h, i: (0, 0)),
                  pl.BlockSpec((4, DA_HEAD), lambda bb, h, i: (0, 0)),
                  pl.BlockSpec((1, LANE), lambda bb, h, i: (0, 0))],
        out_specs=pl.BlockSpec((1, tq, LANE), lambda bb, h, i: (bb, i, h)),
        out_shape=jax.ShapeDtypeStruct((b, n_rows, hq * DA_V), F32),
        scratch_shapes=[pltpu.VMEM((t, LANE), BF16), pltpu.VMEM((t, LANE), BF16)],
        compiler_params=_params("arbitrary", "arbitrary", "arbitrary"),
        name="diff_attention",
    )(p, p, p, cosf, sins, cosf, sins, lam_p, subln_g.reshape(1, DA_V))


def rope_tables(n_lat, n_ctx):
    rows = n_lat // GRID_W
    row = jnp.repeat(jnp.arange(rows, dtype=F32), GRID_W)
    col = jnp.tile(jnp.arange(GRID_W, dtype=F32), rows)
    inv = ROPE_BASE ** (-jnp.arange(ROPE_FREQS, dtype=F32) / ROPE_FREQS)
    ang = jnp.stack([row[:, None] * inv, col[:, None] * inv], axis=1)
    cos, sin = jnp.cos(ang), jnp.sin(ang)
    cosf = jnp.concatenate([cos, cos], axis=-1).reshape(n_lat, DA_HEAD)
    sins = jnp.concatenate([-sin, sin], axis=-1).reshape(n_lat, DA_HEAD)
    cosf = jnp.concatenate([jnp.tile(cosf, (1, 2)), jnp.ones((n_ctx, LANE), F32)], axis=0)
    sins = jnp.concatenate([jnp.tile(sins, (1, 2)), jnp.zeros((n_ctx, LANE), F32)], axis=0)
    return cosf, sins


def _even_mixers(xs, mods, keep_ctx, n_lat, norm1_g, w_in, s5p, hyp, dft):
    b, t, _ = xs.shape
    n_ctx = t - n_lat
    u_t, p_h = in_proj(xs, norm1_g, mods, w_in.astype(BF16), n_lat, GROUP_W)

    (lam_re, lam_im, log_dt, b_re, b_im, c_re, c_im, d_skip, glu_w, glu_b) = s5p
    u2 = u_t.reshape(t * b, GROUP_W)
    tabs = [s5_tables(lam_re[d], lam_im[d], log_dt[d], b_re[d], b_im[d], c_re[d], c_im[d]) for d in range(2)]
    y_f = s5_scan_call(u2, tabs[0], n_lat, t, b, False, None)
    a_t = s5_scan_call(u2, tabs[1], n_lat, t, b, True,
                       (y_f, d_skip.reshape(1, GROUP_W), glu_w.astype(BF16), glu_b.reshape(1, GROUP_W)))

    (conv_w, conv_b, f_w1, f_b1, f_w2, f_b2, f_w3, f_freq, log_decay, bias) = hyp
    fh_l, fl_l, (fw_l, inv_l), wf_l = dft[0]
    kr, ki = hyena_filter_response(n_lat, fh_l, fl_l, wf_l, f_w1, f_b1, f_w2, f_b2, f_w3, f_freq, log_decay)
    b_l = hyena_call(p_h, n_lat, 0, conv_w, conv_b, fw_l, inv_l, kr, ki, bias)
    b_c = None
    if keep_ctx:
        fh_c, fl_c, (fw_c, inv_c), wf_c = dft[1]
        kr, ki = hyena_filter_response(n_ctx, fh_c, fl_c, wf_c, f_w1, f_b1, f_w2, f_b2, f_w3, f_freq,
                                       log_decay)
        b_c = hyena_call(p_h, n_ctx, n_lat // n_ctx, conv_w, conv_b, fw_c, inv_c, kr, ki, bias)
    return a_t.reshape(t, b * GROUP_W), b_l, b_c


def _odd_mixers(xs, mods, keep_ctx, n_lat, lam_init, norm1_g, w_in, rwp, dap, rope):
    b, t, _ = xs.shape
    n_rows = t if keep_ctx else n_lat
    (p,) = in_proj(xs, norm1_g, mods, w_in.astype(BF16), n_lat, 0)

    (conv_w, w0, w_up, a0, a_up, g_up, k_k, k_a, r_k, ln_g, ln_b) = rwp
    rkv = short_conv(p, conv_w, n_lat)
    decay, a_lr, gate = rwkv_lora(p, w_up, a_up, g_up, w0, a0)
    yf, yb = rwkv_scan(_to_scan_layout(decay), _to_scan_layout(a_lr), _to_scan_layout(rkv),
                       _chain_const(k_k, b), _chain_const(k_a, b), n_lat)
    a_m = rwkv_post(_from_scan_layout(yf, yb, b), rkv, a_lr, gate, k_a, r_k, ln_g, ln_b, n_rows)

    lam_p, subln_g = dap
    b_m = diff_attention(p, RW_IN, rope[0], rope[1], lam_p, subln_g, lam_init, n_lat, n_rows)
    return a_m, b_m, None


def kernel(x, c, ctx, c_ctx, ada_w, ada_b, norm1_g, norm2_g, mlp_w1, mlp_w2, final_g, ev_w_in, ev_w_out, s5_lam_re, s5_lam_im, s5_log_dt, s5_b_re, s5_b_im, s5_c_re, s5_c_im, s5_d, s5_glu_w, s5_glu_b, hy_conv_w, hy_conv_b, hy_f_w1, hy_f_b1, hy_f_w2, hy_f_b2, hy_f_w3, hy_f_freq, hy_log_decay, hy_bias, od_w_in, od_w_out, rw_conv_w, rw_w0, rw_w_up, rw_a0, rw_a_up, rw_g_up, rw_k_k, rw_k_a, rw_r_k, rw_ln_g, rw_ln_b, da_lam, da_subln_g):
    b, n_lat, d = x.shape
    n_ctx = ctx.shape[1]
    assert d == D_MODEL and b == SUBLANE
    assert n_lat % TOKEN_TILE == 0 and n_ctx % TOKEN_TILE == 0 and n_lat % n_ctx == 0
    xs = jnp.concatenate([x.astype(F32), ctx.astype(F32)], axis=1)

    c_rows = jnp.zeros((16, d), F32).at[:b].set(c.astype(F32)).at[b].set(c_ctx.astype(F32))
    depth = ada_w.shape[0]
    mods_all = ada_mods(c_rows, ada_w, ada_b).reshape(depth, 16, 6, d)

    rope = rope_tables(n_lat, n_ctx)
    dft = []
    for n in (n_lat, n_ctx):
        fwd, wf = dft_tables(n)
        hi, lo = _split_bf16(fwd)
        dft.append((hi, lo, dft_blocks(hi), wf))

    for l in range(depth):
        keep_ctx = l < depth - 1
        i = l // 2
        mods = mods_all[l]
        if l % 2 == 0:
            s5p = (s5_lam_re[i], s5_lam_im[i], s5_log_dt[i], s5_b_re[i], s5_b_im[i], s5_c_re[i], s5_c_im[i],
                   s5_d[i], s5_glu_w[i], s5_glu_b[i])
            hyp = (hy_conv_w[i], hy_conv_b[i], hy_f_w1[i], hy_f_b1[i], hy_f_w2[i], hy_f_b2[i], hy_f_w3[i],
                   hy_f_freq[i], hy_log_decay[i], hy_bias[i])
            a_m, b_m, b_c = _even_mixers(xs, mods, keep_ctx, n_lat, norm1_g[l], ev_w_in[i], s5p, hyp, dft)
            w_out = ev_w_out[i]
        else:
            rwp = (rw_conv_w[i], rw_w0[i], rw_w_up[i], rw_a0[i], rw_a_up[i], rw_g_up[i], rw_k_k[i],
                   rw_k_a[i], rw_r_k[i], rw_ln_g[i], rw_ln_b[i])
            lam_init = 0.8 - 0.6 * math.exp(-0.3 * l)
            a_m, b_m, b_c = _odd_mixers(xs, mods, keep_ctx, n_lat, lam_init, norm1_g[l], od_w_in[i], rwp,
                                        (da_lam[i], da_subln_g[i]), rope)
            w_out = od_w_out[i]
        n_rows = n_lat + n_ctx if keep_ctx else n_lat
        xs = mix_mlp(xs, a_m, b_m, b_c, w_out.astype(BF16), norm2_g[l], mods, mlp_w1[l].astype(BF16),
                     mlp_w2[l].astype(BF16), final_g, n_lat, n_rows, l % 2 == 0, l == depth - 1)
    return xs
```

```python
import functools
import math

import jax
import jax.numpy as jnp
from jax import lax
from jax.experimental import pallas as pl
from jax.experimental.pallas import tpu as pltpu

F32 = jnp.float32
BF16 = jnp.bfloat16
HIGHEST = lax.Precision.HIGHEST

D_MODEL = 1024
DEPTH = 4
GRID_W = 64
MLP_HIDDEN = 4 * D_MODEL
GROUP_W = D_MODEL // 2
EPS = 1e-6

S5_GROUP = 16
S5_GROUPS = GROUP_W // S5_GROUP
S5_STATE = 64
S5_W = S5_GROUPS * S5_STATE

HY_ORDER = 2
HY_EMB = 33
HY_BANDS = (HY_EMB - 1) // 2
SHORT_K = 3

RW_HEAD = 64
RW_HEADS = GROUP_W // RW_HEAD
RW_LORA_W = 256
RW_LN_EPS = 64e-5
RW_IN = 3 * GROUP_W + RW_LORA_W

DA_HEADS = 4
DA_HEAD = 64
DA_V = 2 * DA_HEAD
DA_SCALE = DA_HEAD ** -0.5
DA_SUBLN_EPS = 1e-5
ROPE_BASE = 10000.0
ROPE_FREQS = DA_HEAD // 4

LANE = 128
SUBLANE = 8
TOKEN_TILE = 256
VMEM_LIMIT = 48 * 1024 * 1024


def _params(*sem):
    return pltpu.CompilerParams(dimension_semantics=sem, vmem_limit_bytes=VMEM_LIMIT)


def _split_bf16(x):
    hi = x.astype(BF16)
    lo = (x - hi.astype(F32)).astype(BF16)
    return hi, lo


def _dot(a, b):
    return jnp.dot(a, b, preferred_element_type=F32)


def _ada_kernel(c_ref, w_ref, b_ref, o_ref):
    c = c_ref[...]
    s = c * jax.nn.sigmoid(c)
    o_ref[0] = jnp.dot(s, w_ref[0], preferred_element_type=F32, precision=HIGHEST) + b_ref[0]


def ada_mods(c_rows, ada_w, ada_b):
    depth, d, n = ada_w.shape
    tn = 1024
    return pl.pallas_call(
        _ada_kernel,
        grid=(depth, n // tn),
        in_specs=[pl.BlockSpec((16, d), lambda l, j: (0, 0)),
                  pl.BlockSpec((1, d, tn), lambda l, j: (l, 0, j)),
                  pl.BlockSpec((1, 1, tn), lambda l, j: (l, 0, j))],
        out_specs=pl.BlockSpec((1, 16, tn), lambda l, j: (l, 0, j)),
        out_shape=jax.ShapeDtypeStruct((depth, 16, n), F32),
        compiler_params=_params("arbitrary", "arbitrary"),
        name="ada_mods",
    )(c_rows, ada_w, ada_b.reshape(depth, 1, n))


def _norm_mod(x, g, ml, mc, is_ctx, k):
    ms = jnp.mean(x * x, axis=-1, keepdims=True)
    xn = x * lax.rsqrt(ms + EPS) * g
    m = jnp.where(is_ctx, mc, ml)
    return xn * (1.0 + m[k + 1:k + 2]) + m[k:k + 1]


def _inproj_kernel(x_ref, g_ref, ml_ref, mc_ref, w_ref, *o_refs, n_lat_tiles, n_tmaj):
    i = pl.program_id(1)
    h = _norm_mod(x_ref[0], g_ref[...], ml_ref[0], mc_ref[0], i >= n_lat_tiles, 0)
    r = _dot(h.astype(BF16), w_ref[...])
    if n_tmaj:
        o_refs[0][...] = r[:, :n_tmaj]
        o_refs[1][0] = r[:, n_tmaj:]
    else:
        o_refs[0][0] = r


def in_proj(x, g, mods, w, n_lat, n_tmaj):
    b, t, d = x.shape
    n = w.shape[1]
    tm = TOKEN_TILE
    if n_tmaj:
        out_shape = [jax.ShapeDtypeStruct((t, b * n_tmaj), F32), jax.ShapeDtypeStruct((b, t, n - n_tmaj), F32)]
        out_specs = [pl.BlockSpec((tm, n_tmaj), lambda bb, i: (i, bb)),
                     pl.BlockSpec((1, tm, n - n_tmaj), lambda bb, i: (bb, i, 0))]
    else:
        out_shape = [jax.ShapeDtypeStruct((b, t, n), F32)]
        out_specs = [pl.BlockSpec((1, tm, n), lambda bb, i: (bb, i, 0))]
    return pl.pallas_call(
        functools.partial(_inproj_kernel, n_lat_tiles=n_lat // tm, n_tmaj=n_tmaj),
        grid=(b, t // tm),
        in_specs=[pl.BlockSpec((1, tm, d), lambda bb, i: (bb, i, 0)),
                  pl.BlockSpec((1, d), lambda bb, i: (0, 0)),
                  pl.BlockSpec((1, 6, d), lambda bb, i: (bb, 0, 0)),
                  pl.BlockSpec((1, 6, d), lambda bb, i: (b, 0, 0)),
                  pl.BlockSpec((d, n), lambda bb, i: (0, 0), pipeline_mode=pl.Buffered(1))],
        out_specs=out_specs,
        out_shape=out_shape,
        compiler_params=_params("arbitrary", "arbitrary"),
        name="in_proj",
    )(x, g.reshape(1, d), mods, mods, w)


MLP_CHUNK = 1024


def _mix_mlp_kernel(x_ref, a_ref, b_ref, bc_ref, wa_ref, wb_ref, g_ref, ml_ref, mc_ref, w1_ref, w2_ref, fg_ref,
                    o_ref, acc_ref, *, n_lat_tiles, a_tmaj, b_split, final):
    i = pl.program_id(1)
    is_ctx = i >= n_lat_tiles
    m = jnp.where(is_ctx, mc_ref[0], ml_ref[0])
    a = a_ref[...] if a_tmaj else a_ref[0]
    bmix = jnp.where(is_ctx, bc_ref[0], b_ref[0]) if b_split else b_ref[0]
    mix = _dot(a.astype(BF16), wa_ref[...]) + _dot(bmix.astype(BF16), wb_ref[...])
    x = x_ref[0] + m[2:3] * mix
    h = _norm_mod(x, g_ref[...], ml_ref[0], mc_ref[0], is_ctx, 3).astype(BF16)
    for k in range(w1_ref.shape[1] // MLP_CHUNK):
        ks = slice(k * MLP_CHUNK, (k + 1) * MLP_CHUNK)
        hid = jnp.maximum(_dot(h, w1_ref[:, ks]), 0.0)
        part = _dot((hid * hid).astype(BF16), w2_ref[ks, :])
        if k == 0:
            acc_ref[...] = part
        else:
            acc_ref[...] += part
    y = x + m[5:6] * acc_ref[...]
    if final:
        ms = jnp.mean(y * y, axis=-1, keepdims=True)
        y = y * lax.rsqrt(ms + EPS) * fg_ref[...]
    o_ref[0] = y


def mix_mlp(x, a, bm, bm_ctx, w_out, g, mods, w1, w2, final_g, n_lat, n_rows, a_tmaj, final):
    b, t, d = x.shape
    wd = bm.shape[-1]
    hid = w1.shape[1]
    tm = TOKEN_TILE
    nl = n_lat // tm
    once = pl.Buffered(1)
    if a_tmaj:
        a_spec = pl.BlockSpec((tm, wd), lambda bb, i: (i, bb))
    else:
        a_spec = pl.BlockSpec((1, tm, wd), lambda bb, i: (bb, i, 0))
    b_split = bm_ctx is not None
    if b_split:
        b_spec = pl.BlockSpec((1, tm, wd), lambda bb, i: (bb, jnp.minimum(i, nl - 1), 0))
        bc_spec = pl.BlockSpec((1, tm, wd), lambda bb, i: (bb, jnp.maximum(i - nl, 0), 0))
    else:
        b_spec = pl.BlockSpec((1, tm, wd), lambda bb, i: (bb, i, 0))
        bc_spec = pl.BlockSpec((1, tm, wd), lambda bb, i: (0, 0, 0))
        bm_ctx = bm
    return pl.pallas_call(
        functools.partial(_mix_mlp_kernel, n_lat_tiles=nl, a_tmaj=a_tmaj, b_split=b_split, final=final),
        grid=(b, n_rows // tm),
        in_specs=[pl.BlockSpec((1, tm, d), lambda bb, i: (bb, i, 0)),
                  a_spec,
                  b_spec,
                  bc_spec,
                  pl.BlockSpec((wd, d), lambda bb, i: (0, 0), pipeline_mode=once),
                  pl.BlockSpec((wd, d), lambda bb, i: (1, 0), pipeline_mode=once),
                  pl.BlockSpec((1, d), lambda bb, i: (0, 0)),
                  pl.BlockSpec((1, 6, d), lambda bb, i: (bb, 0, 0)),
                  pl.BlockSpec((1, 6, d), lambda bb, i: (b, 0, 0)),
                  pl.BlockSpec((d, hid), lambda bb, i: (0, 0), pipeline_mode=once),
                  pl.BlockSpec((hid, d), lambda bb, i: (0, 0), pipeline_mode=once),
                  pl.BlockSpec((1, d), lambda bb, i: (0, 0))],
        out_specs=pl.BlockSpec((1, tm, d), lambda bb, i: (bb, i, 0)),
        out_shape=jax.ShapeDtypeStruct((b, n_rows, d), F32),
        scratch_shapes=[pltpu.VMEM((tm, d), F32)],
        compiler_params=_params("arbitrary", "arbitrary"),
        name="mix_mlp",
    )(x, a, bm, bm_ctx, w_out, w_out, g.reshape(1, d), mods, mods, w1, w2, final_g.reshape(1, d))


def _conv3(x, w, first, last):
    rows = x.shape[0]
    xm = jnp.where(first, 0.0, pltpu.roll(x, 1, 0))
    xp = jnp.where(last, 0.0, pltpu.roll(x, rows - 1, 0))
    return xm * w[0:1] + x * w[1:2] + xp * w[2:3]


def _shortconv_kernel(x_ref, w_ref, o_ref, *, n_lat):
    x = x_ref[0]
    t = x.shape[0]
    row = lax.broadcasted_iota(jnp.int32, (t, 1), 0)
    first = (row == 0) | (row == n_lat)
    last = (row == n_lat - 1) | (row == t - 1)
    o_ref[0, 0] = _conv3(x, w_ref[...], first, last)


def short_conv(p, w, n_lat):
    b, t, _ = p.shape
    c = w.shape[1]
    cb = GROUP_W
    return pl.pallas_call(
        functools.partial(_shortconv_kernel, n_lat=n_lat),
        grid=(b, c // cb),
        in_specs=[pl.BlockSpec((1, t, cb), lambda bb, j: (bb, 0, j)),
                  pl.BlockSpec((SHORT_K, cb), lambda bb, j: (0, j))],
        out_specs=pl.BlockSpec((1, 1, t, cb), lambda bb, j: (j, bb, 0, 0)),
        out_shape=jax.ShapeDtypeStruct((c // cb, b, t, cb), F32),
        compiler_params=_params("arbitrary", "arbitrary"),
        name="short_conv",
    )(p, w)


S5_KB = LANE
S5_NB = GROUP_W // S5_KB
S5_SB = S5_W // S5_NB


def _s5_kernel(*refs, tc, nb, reverse, finish):
    if finish:
        (u_ref, wbr_ref, wbi_ref, lr_ref, li_ref, wcr_ref, wci_ref, yf_ref, d_ref, gw_ref, gb_ref,
         o_ref, hr_s, hi_s, sr_s, si_s) = refs
    else:
        (u_ref, wbr_ref, wbi_ref, lr_ref, li_ref, wcr_ref, wci_ref,
         o_ref, hr_s, hi_s, sr_s, si_s) = refs

    @pl.when(pl.program_id(0) == 0)
    def _():
        sr_s[...] = jnp.zeros_like(sr_s)
        si_s[...] = jnp.zeros_like(si_s)

    u = u_ref[...]
    ub = u.astype(BF16)
    for j in range(S5_NB):
        uj = ub[:, j * S5_KB:(j + 1) * S5_KB]
        hr_s[:, j * S5_SB:(j + 1) * S5_SB] = _dot(uj, wbr_ref[j])
        hi_s[:, j * S5_SB:(j + 1) * S5_SB] = _dot(uj, wbi_ref[j])

    for s in range(S5_NB):
        sl = slice(s * S5_SB, (s + 1) * S5_SB)
        lr = jnp.broadcast_to(lr_ref[:, sl], (nb, S5_SB))
        li = jnp.broadcast_to(li_ref[:, sl], (nb, S5_SB))

        def body(k, carry, sl=sl, lr=lr, li=li):
            hr, hi = carry
            t = (tc - 1 - k) if reverse else k
            r0 = pl.multiple_of(t * nb, nb)
            nr = lr * hr - li * hi + hr_s[pl.ds(r0, nb), sl]
            ni = lr * hi + li * hr + hi_s[pl.ds(r0, nb), sl]
            hr_s[pl.ds(r0, nb), sl] = nr
            hi_s[pl.ds(r0, nb), sl] = ni
            return nr, ni

        hr, hi = lax.fori_loop(0, tc, body, (sr_s[:, sl], si_s[:, sl]), unroll=4)
        sr_s[:, sl] = hr
        si_s[:, sl] = hi

    for j in range(S5_NB):
        sl = slice(j * S5_SB, (j + 1) * S5_SB)
        co = slice(j * S5_KB, (j + 1) * S5_KB)
        y = _dot(hr_s[:, sl].astype(BF16), wcr_ref[j]) + _dot(hi_s[:, sl].astype(BF16), wci_ref[j])
        if finish:
            o_ref[:, co] = y + yf_ref[:, co] + u[:, co] * d_ref[:, co]
        else:
            o_ref[:, co] = y

    if finish:
        y = o_ref[...]
        y = 0.5 * y * (1.0 + lax.erf(y * (2.0 ** -0.5)))
        z = _dot(y.astype(BF16), gw_ref[...]) + gb_ref[...]
        o_ref[...] = y * jax.nn.sigmoid(z)


def s5_scan_call(u2, tabs, n_lat_steps, n_steps, nb, reverse, extra):
    tc = 32
    rows = tc * nb
    n_chunks = n_steps // tc
    n_lat_chunks = n_lat_steps // tc
    n_ctx_chunks = n_chunks - n_lat_chunks
    if reverse:
        def cmap(c):
            return (n_chunks - 1 - c, 0)
    else:
        def cmap(c):
            return (jnp.where(c < n_ctx_chunks, n_lat_chunks + c, c - n_ctx_chunks), 0)
    wbr, wbi, lr, li, wcr, wci = tabs
    full3 = lambda a: pl.BlockSpec(a.shape, lambda c: (0, 0, 0))
    full2 = lambda a: pl.BlockSpec(a.shape, lambda c: (0, 0))
    in_specs = [pl.BlockSpec((rows, GROUP_W), cmap), full3(wbr), full3(wbi), full2(lr), full2(li),
                full3(wcr), full3(wci)]
    args = [u2, wbr, wbi, lr, li, wcr, wci]
    finish = extra is not None
    if finish:
        yf, dsk, gw, gb = extra
        in_specs += [pl.BlockSpec((rows, GROUP_W), cmap), full2(dsk), full2(gw), full2(gb)]
        args += [yf, dsk, gw, gb]
    return pl.pallas_call(
        functools.partial(_s5_kernel, tc=tc, nb=nb, reverse=reverse, finish=finish),
        grid=(n_chunks,),
        in_specs=in_specs,
        out_specs=pl.BlockSpec((rows, GROUP_W), cmap),
        out_shape=jax.ShapeDtypeStruct(u2.shape, F32),
        scratch_shapes=[pltpu.VMEM((rows, S5_W), F32), pltpu.VMEM((rows, S5_W), F32),
                        pltpu.VMEM((nb, S5_W), F32), pltpu.VMEM((nb, S5_W), F32)],
        compiler_params=_params("arbitrary"),
        name="s5_bwd_finish" if finish else "s5_fwd",
    )(*args)


def _block_diag(x):
    nblk, g, r, c = x.shape
    eye = jnp.eye(g, dtype=x.dtype)
    return jnp.einsum('jgrc,gh->jgrhc', x, eye).reshape(nblk, g * r, g * c)


def s5_tables(lam_re, lam_im, log_dt, b_re, b_im, c_re, c_im):
    lam_re = jnp.minimum(lam_re.astype(F32), -1e-4)
    lam_im = lam_im.astype(F32)
    dt = jnp.exp(log_dt.astype(F32))[:, None]
    mag = jnp.exp(lam_re * dt)
    lb_re = mag * jnp.cos(lam_im * dt)
    lb_im = mag * jnp.sin(lam_im * dt)
    den = lam_re * lam_re + lam_im * lam_im
    f_re = ((lb_re - 1.0) * lam_re + lb_im * lam_im) / den
    f_im = (lb_im * lam_re - (lb_re - 1.0) * lam_im) / den
    b_re = b_re.astype(F32)
    b_im = b_im.astype(F32)
    bb_re = f_re[..., None] * b_re - f_im[..., None] * b_im
    bb_im = f_re[..., None] * b_im + f_im[..., None] * b_re
    gpb = S5_KB // S5_GROUP
    to_b = lambda x: _block_diag(
        jnp.swapaxes(x, 1, 2).reshape(S5_NB, gpb, S5_GROUP, S5_STATE)).astype(BF16)
    to_c = lambda x: _block_diag(
        jnp.swapaxes(x.astype(F32), 1, 2).reshape(S5_NB, gpb, S5_STATE, S5_GROUP)).astype(BF16)
    return (to_b(bb_re), to_b(bb_im), lb_re.reshape(1, S5_W), lb_im.reshape(1, S5_W),
            to_c(c_re), to_c(-c_im.astype(F32)))


def dft_tables(n):
    nn = 2 * n
    f = jnp.arange(n, dtype=jnp.int32)[:, None]
    t = jnp.arange(n, dtype=jnp.int32)[None, :]
    q = 1 << (int(math.log2(n)) // 2)

    def factor(tt):
        ang = ((f * tt[None, :]) % nn).astype(F32) * (2.0 * math.pi / nn)
        return jnp.cos(ang), jnp.sin(ang)

    c1, s1 = factor(jnp.arange(n // q, dtype=jnp.int32) * q)
    c0, s0 = factor(jnp.arange(q, dtype=jnp.int32))
    cos = (c1[:, :, None] * c0[:, None, :] - s1[:, :, None] * s0[:, None, :]).reshape(n, n)
    sin = (s1[:, :, None] * c0[:, None, :] + c1[:, :, None] * s0[:, None, :]).reshape(n, n)
    nyq = jnp.where(t % 2 == 0, 1.0, -1.0).astype(F32)
    msin = jnp.where(f == 0, nyq, -sin)
    fwd = jnp.concatenate([cos, msin], axis=0)
    wf = jnp.where(jnp.arange(n) == 0, 1.0 / nn, 2.0 / nn).astype(F32)
    return fwd, wf


def _filtdft_kernel(fh_ref, fl_ref, hh_ref, hl_ref, o_ref):
    fh = fh_ref[...]
    o_ref[...] = _dot(fh, hh_ref[0]) + _dot(fh, hl_ref[0]) + _dot(fl_ref[...], hh_ref[0])


def filter_dft(fwd_hi, fwd_lo, h_sum, h_dif):
    n2, n = fwd_hi.shape
    c = h_sum.shape[1]
    tr = min(256, n)
    tcn = 512
    nrb = n // tr
    hh, hl = _split_bf16(jnp.stack([h_sum, h_dif]))
    frow = lambda i, j: (jnp.where(i == 2 * nrb, nrb, i), 0)
    hsel = lambda i, j: (jnp.where((i >= nrb) & (i < 2 * nrb), 1, 0), 0, j)
    return pl.pallas_call(
        _filtdft_kernel,
        grid=(2 * nrb + 1, c // tcn),
        in_specs=[pl.BlockSpec((tr, n), frow),
                  pl.BlockSpec((tr, n), frow),
                  pl.BlockSpec((1, n, tcn), hsel),
                  pl.BlockSpec((1, n, tcn), hsel)],
        out_specs=pl.BlockSpec((tr, tcn), lambda i, j: (i, j)),
        out_shape=jax.ShapeDtypeStruct((n2 + tr, c), F32),
        compiler_params=_params("arbitrary", "arbitrary"),
        name="filter_dft",
    )(fwd_hi, fwd_lo, hh, hl)


def hyena_filter_response(n, fwd_hi, fwd_lo, wf, f_w1, f_b1, f_w2, f_b2, f_w3, f_freq, log_decay):
    t = jnp.linspace(0.0, 1.0, n, dtype=F32)[:, None]
    w = 2.0 * math.pi * jnp.arange(n, dtype=F32)[:, None] / n
    bands = jnp.linspace(1e-4, HY_BANDS - 1, HY_BANDS, dtype=F32)[None, :]
    z = jnp.concatenate([t, jnp.cos(bands * w), -jnp.sin(bands * w)], axis=-1)
    freq = f_freq.astype(F32)
    hid = jnp.sin(freq[0] * (jnp.dot(z, f_w1.astype(F32), precision=HIGHEST) + f_b1.astype(F32)))
    hid = jnp.sin(freq[1] * (jnp.dot(hid, f_w2.astype(F32), precision=HIGHEST) + f_b2.astype(F32)))
    h = jnp.dot(hid, f_w3.astype(F32), precision=HIGHEST).reshape(n, 2, HY_ORDER, GROUP_W)
    h = h * jnp.exp(-t[:, :, None, None] * jnp.exp(log_decay.astype(F32)))
    cw = HY_ORDER * GROUP_W
    h_fwd = h[:, 0].reshape(n, cw)
    h_bwd = h[:, 1].reshape(n, cw).at[0].set(0.0)
    resp = filter_dft(fwd_hi, fwd_lo, h_fwd + h_bwd, h_fwd - h_bwd)
    k_re = resp[:n]
    first = (jnp.arange(n) == 0)[:, None]
    k_im = jnp.where(first, resp[2 * n:2 * n + 1], resp[n:2 * n])
    k_re = (k_re * wf[:, None]).reshape(n, HY_ORDER, GROUP_W).transpose(1, 0, 2)
    k_im = (k_im * wf[:, None]).reshape(n, HY_ORDER, GROUP_W).transpose(1, 0, 2)
    return k_re, k_im


def _hyena_kernel(z0_ref, x1_ref, x2_ref, cw_ref, cb_ref, fw_ref, inv_ref, kr_ref, ki_ref, bias_ref,
                  o_ref, zb_s, zf_s, acc_s):
    o = pl.program_id(1)
    f = pl.program_id(2)
    nf = pl.num_programs(2)
    c = GROUP_W

    def stream(ref, s):
        x = ref[0]
        row = lax.broadcasted_iota(jnp.int32, (x.shape[0], 1), 0)
        cols = slice(s * c, (s + 1) * c)
        return _conv3(x, cw_ref[:, cols], row == 0, row == x.shape[0] - 1) + cb_ref[:, cols]

    @pl.when((o == 0) & (f == 0))
    def _():
        z0 = stream(z0_ref, 0)
        zf_s[...] = z0
        zb_s[...] = z0.astype(BF16)

    @pl.when(f == 0)
    def _():
        acc_s[...] = jnp.zeros_like(acc_s)

    xf = _dot(fw_ref[0], zb_s[...])
    fb = xf.shape[0] // 2
    xr = xf[:fb]
    xi = xf[fb:]
    kr = kr_ref[0]
    ki = ki_ref[0]
    row = lax.broadcasted_iota(jnp.int32, (fb, 1), 0)
    packed = (row == 0) & (f == 0)
    yr = jnp.where(packed, xr * kr, xr * kr - xi * ki)
    yi = jnp.where(packed, xi * ki, xr * ki + xi * kr)
    yf = jnp.concatenate([yr, yi], axis=0).astype(BF16)
    acc_s[...] += _dot(inv_ref[0], yf)

    @pl.when(f == nf - 1)
    def _():
        bias = bias_ref[...]

        @pl.when(o == 0)
        def _():
            z1 = stream(x1_ref, 1) * (acc_s[...] + zf_s[...] * bias[0:1])
            zf_s[...] = z1
            zb_s[...] = z1.astype(BF16)

        @pl.when(o == 1)
        def _():
            o_ref[0] = stream(x2_ref, 2) * (acc_s[...] + zf_s[...] * bias[1:2])


HY_FREQ_BLOCK = 256


def dft_blocks(fwd_hi):
    n = fwd_hi.shape[1]
    fb = min(HY_FREQ_BLOCK, n)
    blocks = fwd_hi.reshape(2, n // fb, fb, n).transpose(1, 0, 2, 3).reshape(n // fb, 2 * fb, n)
    return blocks, blocks.transpose(0, 2, 1)


def hyena_call(p, n, row_blk, conv_w, conv_b, fwd_blk, inv_blk, k_re, k_im, bias):
    b = p.shape[0]
    c = GROUP_W
    nf, fb2, _ = fwd_blk.shape
    fb = fb2 // 2
    once = pl.Buffered(1)
    in_specs = [pl.BlockSpec((1, n, c), lambda bb, o, f: (bb, row_blk, 0), pipeline_mode=once),
                pl.BlockSpec((1, n, c), lambda bb, o, f: (bb, row_blk, 1), pipeline_mode=once),
                pl.BlockSpec((1, n, c), lambda bb, o, f: (bb, row_blk, 2), pipeline_mode=once),
                pl.BlockSpec((SHORT_K, 3 * c), lambda bb, o, f: (0, 0)),
                pl.BlockSpec((1, 3 * c), lambda bb, o, f: (0, 0)),
                pl.BlockSpec((1, fb2, n), lambda bb, o, f: (f, 0, 0)),
                pl.BlockSpec((1, n, fb2), lambda bb, o, f: (f, 0, 0)),
                pl.BlockSpec((1, fb, c), lambda bb, o, f: (o, f, 0)),
                pl.BlockSpec((1, fb, c), lambda bb, o, f: (o, f, 0)),
                pl.BlockSpec((HY_ORDER, c), lambda bb, o, f: (0, 0))]
    return pl.pallas_call(
        _hyena_kernel,
        grid=(b, HY_ORDER, nf),
        in_specs=in_specs,
        out_specs=pl.BlockSpec((1, n, c), lambda bb, o, f: (bb, 0, 0)),
        out_shape=jax.ShapeDtypeStruct((b, n, c), F32),
        scratch_shapes=[pltpu.VMEM((n, c), BF16), pltpu.VMEM((n, c), F32), pltpu.VMEM((n, c), F32)],
        compiler_params=_params("arbitrary", "arbitrary", "arbitrary"),
        name="hyena",
    )(p, p, p, conv_w, conv_b.reshape(1, 3 * c), fwd_blk, inv_blk, k_re, k_im, bias)


def _rwkv_lora_kernel(p_ref, ww_ref, wa_ref, wg_ref, w0_ref, a0_ref, wp_ref, ap_ref, g_ref):
    x = p_ref[0]
    wa_in = x[:, :LANE]
    th = jnp.tanh(wa_in).astype(BF16)
    lin = wa_in.astype(BF16)
    w0 = w0_ref[...]
    a0 = a0_ref[...]
    for d in range(2):
        wl = -jax.nn.softplus(-(w0[d:d + 1] + _dot(th, ww_ref[d]))) - 0.5
        wp_ref[d, 0] = jnp.exp(-jnp.exp(wl))
        ap_ref[d, 0] = jax.nn.sigmoid(a0[d:d + 1] + _dot(lin, wa_ref[d]))
    g_ref[0] = _dot(jax.nn.sigmoid(x[:, LANE:]).astype(BF16), wg_ref[...])


def rwkv_lora(p, w_up, a_up, g_up, w0, a0):
    b, t, _ = p.shape
    c = GROUP_W
    tm = TOKEN_TILE
    zeros = jnp.zeros((2, 64, c), F32)
    ww = jnp.concatenate([w_up.astype(F32), zeros], axis=1).astype(BF16)
    wa = jnp.concatenate([zeros, a_up.astype(F32)], axis=1).astype(BF16)
    both = jax.ShapeDtypeStruct((2, b, t, c), F32)
    both_spec = pl.BlockSpec((2, 1, tm, c), lambda bb, i: (0, bb, i, 0))
    return pl.pallas_call(
        _rwkv_lora_kernel,
        grid=(b, t // tm),
        in_specs=[pl.BlockSpec((1, tm, RW_LORA_W), lambda bb, i: (bb, i, 3 * c // RW_LORA_W)),
                  pl.BlockSpec((2, LANE, c), lambda bb, i: (0, 0, 0)),
                  pl.BlockSpec((2, LANE, c), lambda bb, i: (0, 0, 0)),
                  pl.BlockSpec((LANE, c), lambda bb, i: (0, 0)),
                  pl.BlockSpec((2, c), lambda bb, i: (0, 0)),
                  pl.BlockSpec((2, c), lambda bb, i: (0, 0))],
        out_specs=[both_spec, both_spec, pl.BlockSpec((1, tm, c), lambda bb, i: (bb, i, 0))],
        out_shape=[both, both, jax.ShapeDtypeStruct((b, t, c), F32)],
        compiler_params=_params("arbitrary", "arbitrary"),
        name="rwkv_lora",
    )(p, ww, wa, g_up.astype(BF16), w0, a0)


SCAN_UNROLL = 32


def _rwkv_scan_kernel(wpf_ref, wpb_ref, apf_ref, apb_ref, kf_ref, kb_ref, vf_ref, vb_ref, rf_ref, rb_ref,
                      kkc_ref, kac_ref, yf_ref, yb_ref, s_ref, w_s, kk_s, b_s, ke_s, r_s, v_s, *, tb):
    n = RW_HEAD

    @pl.when(pl.program_id(0) == 0)
    def _():
        s_ref[...] = jnp.zeros_like(s_ref)

    def both(f_ref, b_ref, t):
        return jnp.concatenate([f_ref[0, t], b_ref[0, tb - 1 - t]], axis=-1)

    def prepare(t, carry):
        k = both(kf_ref, kb_ref, t)
        a = both(apf_ref, apb_ref, t)
        kk = k * kkc_ref[...]
        nrm = jnp.sqrt(jnp.sum(kk * kk, axis=0, keepdims=True))
        kk = kk / jnp.maximum(nrm, 1e-12)
        w_s[t] = both(wpf_ref, wpb_ref, t)
        kk_s[t] = kk
        b_s[t] = kk * a
        ke_s[t] = k * (1.0 + (a - 1.0) * kac_ref[...])
        r_s[t] = both(rf_ref, rb_ref, t)
        v_s[t] = both(vf_ref, vb_ref, t)
        return carry

    lax.fori_loop(0, tb, prepare, 0, unroll=16)
    sa0 = jnp.zeros((n, LANE), F32)
    for j in range(n):
        sa0 = sa0 + s_ref[j] * kk_s[0, j:j + 1, :]

    def step(t, sa):
        nxt = jnp.minimum(t + 1, tb - 1)
        v = v_s[t]

        def keys(j, carry):
            y, sa_next = carry
            row = pl.ds(j, 1)
            s = s_ref[j] * w_s[t, row, :] - sa * b_s[t, row, :] + v * ke_s[t, row, :]
            s_ref[j] = s
            return y + s * r_s[t, row, :], sa_next + s * kk_s[nxt, row, :]

        zero = jnp.zeros((n, LANE), F32)
        y, sa_next = lax.fori_loop(0, n, keys, (zero, zero), unroll=SCAN_UNROLL)
        yf_ref[t] = y
        yb_ref[tb - 1 - t] = y
        return sa_next

    lax.fori_loop(0, tb, step, sa0)


def rwkv_scan(w, a, rkv, kkc, kac, n_lat):
    t = rkv.shape[1]
    tb = 32
    n = RW_HEAD
    half = LANE // 2
    n_blk = t // tb
    n_lat_blk = n_lat // tb
    n_ctx_blk = n_blk - n_lat_blk

    def fblk(c):
        return jnp.where(c < n_ctx_blk, n_lat_blk + c, c - n_ctx_blk)

    def rblk(c):
        return n_blk - 1 - c

    def pair(qf, qb):
        return [pl.BlockSpec((1, tb, n, half), lambda c: (qf, fblk(c), 0, 0)),
                pl.BlockSpec((1, tb, n, half), lambda c: (qb, rblk(c), 0, 0))]

    cst = pl.BlockSpec((n, LANE), lambda c: (0, 0))
    buf = pltpu.VMEM((tb, n, LANE), F32)
    out = jax.ShapeDtypeStruct((t, n, LANE), F32)
    return pl.pallas_call(
        functools.partial(_rwkv_scan_kernel, tb=tb),
        grid=(n_blk,),
        in_specs=pair(0, 1) + pair(0, 1) + pair(1, 1) + pair(2, 2) + pair(0, 0) + [cst] * 2,
        out_specs=[pl.BlockSpec((tb, n, LANE), lambda c: (fblk(c), 0, 0)),
                   pl.BlockSpec((tb, n, LANE), lambda c: (rblk(c), 0, 0))],
        out_shape=[out, out],
        scratch_shapes=[pltpu.VMEM((n, n, LANE), F32), buf, buf, buf, buf, buf, buf],
        compiler_params=_params("arbitrary"),
        name="rwkv_scan",
    )(w, w, a, a, rkv, rkv, rkv, rkv, rkv, rkv, kkc, kac)


def _rwkv_post_kernel(y_ref, r_ref, k_ref, v_ref, ap_ref, g_ref, m_ref, ka_ref, rk_ref, lg_ref, lb_ref,
                      o_ref):
    m = m_ref[...]

    def head_mean(x):
        hi, lo = _split_bf16(x)
        return _dot(hi, m) + _dot(lo, m)

    y = y_ref[0]
    d = y - head_mean(y)
    var = head_mean(d * d)
    yn = d * lax.rsqrt(var + RW_LN_EPS) * lg_ref[...] + lb_ref[...]
    a_sum = ap_ref[0, 0] + ap_ref[1, 0]
    k_sum = k_ref[0, 0] * (2.0 + (a_sum - 2.0) * ka_ref[...])
    bonus = head_mean(r_ref[0, 0] * k_sum * rk_ref[...]) * float(RW_HEAD) * v_ref[0, 0]
    o_ref[0] = (yn + bonus) * g_ref[0]


def rwkv_post(y, rkv, ap, gate, k_a, r_k, ln_g, ln_b, n_rows):
    b, t, c = y.shape
    tm = TOKEN_TILE
    hm = jnp.kron(jnp.eye(RW_HEADS, dtype=F32), jnp.full((RW_HEAD, RW_HEAD), 1.0 / RW_HEAD, F32)).astype(BF16)
    tok = pl.BlockSpec((1, tm, c), lambda bb, i: (bb, i, 0))
    vec = pl.BlockSpec((1, c), lambda bb, i: (0, 0))
    rkv_spec = lambda q: pl.BlockSpec((1, 1, tm, c), lambda bb, i: (q, bb, i, 0))
    return pl.pallas_call(
        _rwkv_post_kernel,
        grid=(b, n_rows // tm),
        in_specs=[tok, rkv_spec(0), rkv_spec(1), rkv_spec(2),
                  pl.BlockSpec((2, 1, tm, c), lambda bb, i: (0, bb, i, 0)), tok,
                  pl.BlockSpec((c, c), lambda bb, i: (0, 0)), vec, vec, vec, vec],
        out_specs=tok,
        out_shape=jax.ShapeDtypeStruct((b, n_rows, c), F32),
        compiler_params=_params("arbitrary", "arbitrary"),
        name="rwkv_post",
    )(y, rkv, rkv, rkv, ap, gate, hm, k_a.reshape(1, c), r_k.reshape(1, c),
      ln_g.reshape(1, c), ln_b.reshape(1, c))


def _to_scan_layout(a):
    s, b, t, _ = a.shape
    x = a.reshape(s, b, t, RW_HEADS, RW_HEAD).transpose(0, 2, 4, 1, 3)
    return x.reshape(s, t, RW_HEAD, b * RW_HEADS)


def _from_scan_layout(yf, yb, b):
    t = yf.shape[0]
    half = b * RW_HEADS
    y = yf[..., :half] + yb[..., half:]
    return y.reshape(t, RW_HEAD, b, RW_HEADS).transpose(2, 0, 3, 1).reshape(b, t, GROUP_W)


def _chain_const(x, b):
    return jnp.tile(x.astype(F32).reshape(RW_HEADS, RW_HEAD).T, (1, 2 * b))


def _rope(x, cos, sin):
    lane = lax.broadcasted_iota(jnp.int32, (1, LANE), 1)
    first = (lane % (2 * ROPE_FREQS)) < ROPE_FREQS
    partner = jnp.where(first, pltpu.roll(x, LANE - ROPE_FREQS, 1), pltpu.roll(x, ROPE_FREQS, 1))
    return x * cos + partner * sin


def _attn_tile(q, k, v, lam):
    lane = lax.broadcasted_iota(jnp.int32, (1, LANE), 1)
    m0 = lane < DA_HEAD
    q0 = jnp.where(m0, q, 0.0).astype(BF16)
    q1 = jnp.where(m0, 0.0, q).astype(BF16)
    dn = (((1,), (1,)), ((), ()))
    s0 = lax.dot_general(q0, k, dn, preferred_element_type=F32)
    s1 = lax.dot_general(q1, k, dn, preferred_element_type=F32)
    p0 = jnp.exp2(s0 - jnp.max(s0, axis=-1, keepdims=True))
    p1 = jnp.exp2(s1 - jnp.max(s1, axis=-1, keepdims=True))
    c0 = (1.0 / jnp.sum(p0, axis=-1, keepdims=True)).astype(BF16)
    c1 = (lam / jnp.sum(p1, axis=-1, keepdims=True)).astype(BF16)
    return _dot(p0.astype(BF16) * c0 - p1.astype(BF16) * c1, v)


def _attn_kernel(q_ref, k_ref, v_ref, cq_ref, sq_ref, ck_ref, sk_ref, lp_ref, g_ref, o_ref,
                 kr_s, vb_s, *, n_lat, n_lat_tiles, lam_init):
    i = pl.program_id(2)

    @pl.when(i == 0)
    def _():
        kr_s[...] = _rope(k_ref[0], ck_ref[...], sk_ref[...]).astype(BF16)
        vb_s[...] = v_ref[0].astype(BF16)

    lp = lp_ref[...]
    lam = (jnp.exp(jnp.sum(lp[0:1] * lp[1:2], axis=-1, keepdims=True))
           - jnp.exp(jnp.sum(lp[2:3] * lp[3:4], axis=-1, keepdims=True)) + lam_init)
    q = _rope(q_ref[0], cq_ref[...], sq_ref[...]) * (DA_SCALE * math.log2(math.e))

    def finish(o):
        on = o * lax.rsqrt(jnp.mean(o * o, axis=-1, keepdims=True) + DA_SUBLN_EPS)
        o_ref[0] = on * g_ref[...] * (1.0 - lam_init)

    @pl.when(i < n_lat_tiles)
    def _():
        finish(_attn_tile(q, kr_s[...], vb_s[...], lam))

    @pl.when(i >= n_lat_tiles)
    def _():
        finish(_attn_tile(q, kr_s[n_lat:], vb_s[n_lat:], lam))


def diff_attention(p, col0, cosf, sins, lam_p, subln_g, lam_init, n_lat, n_rows):
    b, t, _ = p.shape
    tq = TOKEN_TILE
    off = col0 // LANE
    hq = DA_HEADS
    return pl.pallas_call(
        functools.partial(_attn_kernel, n_lat=n_lat, n_lat_tiles=n_lat // tq, lam_init=lam_init),
        grid=(b, hq, n_rows // tq),
        in_specs=[pl.BlockSpec((1, tq, LANE), lambda bb, h, i: (bb, i, off + h)),
                  pl.BlockSpec((1, t, LANE), lambda bb, h, i: (bb, 0, off + hq + h)),
                  pl.BlockSpec((1, t, LANE), lambda bb, h, i: (bb, 0, off + 2 * hq + h)),
                  pl.BlockSpec((tq, LANE), lambda bb, h, i: (i, 0)),
                  pl.BlockSpec((tq, LANE), lambda bb, h, i: (i, 0)),
                  pl.BlockSpec((t, LANE), lambda bb, h, i: (0, 0)),
                  pl.BlockSpec((t, LANE), lambda bb, h, i: (0, 0)),
                  pl.BlockSpec((4, DA_HEAD), lambda bb, h, i: (0, 0)),
                  pl.BlockSpec((1, LANE), lambda bb, h, i: (0, 0))],
        out_specs=pl.BlockSpec((1, tq, LANE), lambda bb, h, i: (bb, i, h)),
        out_shape=jax.ShapeDtypeStruct((b, n_rows, hq * DA_V), F32),
        scratch_shapes=[pltpu.VMEM((t, LANE), BF16), pltpu.VMEM((t, LANE), BF16)],
        compiler_params=_params("arbitrary", "arbitrary", "arbitrary"),
        name="diff_attention",
    )(p, p, p, cosf, sins, cosf, sins, lam_p, subln_g.reshape(1, DA_V))


def rope_tables(n_lat, n_ctx):
    rows = n_lat // GRID_W
    row = jnp.repeat(jnp.arange(rows, dtype=F32), GRID_W)
    col = jnp.tile(jnp.arange(GRID_W, dtype=F32), rows)
    inv = ROPE_BASE ** (-jnp.arange(ROPE_FREQS, dtype=F32) / ROPE_FREQS)
    ang = jnp.stack([row[:, None] * inv, col[:, None] * inv], axis=1)
    cos, sin = jnp.cos(ang), jnp.sin(ang)
    cosf = jnp.concatenate([cos, cos], axis=-1).reshape(n_lat, DA_HEAD)
    sins = jnp.concatenate([-sin, sin], axis=-1).reshape(n_lat, DA_HEAD)
    cosf = jnp.concatenate([jnp.tile(cosf, (1, 2)), jnp.ones((n_ctx, LANE), F32)], axis=0)
    sins = jnp.concatenate([jnp.tile(sins, (1, 2)), jnp.zeros((n_ctx, LANE), F32)], axis=0)
    return cosf, sins


def _even_mixers(xs, mods, keep_ctx, n_lat, norm1_g, w_in, s5p, hyp, dft):
    b, t, _ = xs.shape
    n_ctx = t - n_lat
    u_t, p_h = in_proj(xs, norm1_g, mods, w_in.astype(BF16), n_lat, GROUP_W)

    (lam_re, lam_im, log_dt, b_re, b_im, c_re, c_im, d_skip, glu_w, glu_b) = s5p
    u2 = u_t.reshape(t * b, GROUP_W)
    tabs = [s5_tables(lam_re[d], lam_im[d], log_dt[d], b_re[d], b_im[d], c_re[d], c_im[d]) for d in range(2)]
    y_f = s5_scan_call(u2, tabs[0], n_lat, t, b, False, None)
    a_t = s5_scan_call(u2, tabs[1], n_lat, t, b, True,
                       (y_f, d_skip.reshape(1, GROUP_W), glu_w.astype(BF16), glu_b.reshape(1, GROUP_W)))

    (conv_w, conv_b, f_w1, f_b1, f_w2, f_b2, f_w3, f_freq, log_decay, bias) = hyp
    fh_l, fl_l, (fw_l, inv_l), wf_l = dft[0]
    kr, ki = hyena_filter_response(n_lat, fh_l, fl_l, wf_l, f_w1, f_b1, f_w2, f_b2, f_w3, f_freq, log_decay)
    b_l = hyena_call(p_h, n_lat, 0, conv_w, conv_b, fw_l, inv_l, kr, ki, bias)
    b_c = None
    if keep_ctx:
        fh_c, fl_c, (fw_c, inv_c), wf_c = dft[1]
        kr, ki = hyena_filter_response(n_ctx, fh_c, fl_c, wf_c, f_w1, f_b1, f_w2, f_b2, f_w3, f_freq,
                                       log_decay)
        b_c = hyena_call(p_h, n_ctx, n_lat // n_ctx, conv_w, conv_b, fw_c, inv_c, kr, ki, bias)
    return a_t.reshape(t, b * GROUP_W), b_l, b_c


def _odd_mixers(xs, mods, keep_ctx, n_lat, lam_init, norm1_g, w_in, rwp, dap, rope):
    b, t, _ = xs.shape
    n_rows = t if keep_ctx else n_lat
    (p,) = in_proj(xs, norm1_g, mods, w_in.astype(BF16), n_lat, 0)

    (conv_w, w0, w_up, a0, a_up, g_up, k_k, k_a, r_k, ln_g, ln_b) = rwp
    rkv = short_conv(p, conv_w, n_lat)
    decay, a_lr, gate = rwkv_lora(p, w_up, a_up, g_up, w0, a0)
    yf, yb = rwkv_scan(_to_scan_layout(decay), _to_scan_layout(a_lr), _to_scan_layout(rkv),
                       _chain_const(k_k, b), _chain_const(k_a, b), n_lat)
    a_m = rwkv_post(_from_scan_layout(yf, yb, b), rkv, a_lr, gate, k_a, r_k, ln_g, ln_b, n_rows)

    lam_p, subln_g = dap
    b_m = diff_attention(p, RW_IN, rope[0], rope[1], lam_p, subln_g, lam_init, n_lat, n_rows)
    return a_m, b_m, None


def kernel(x, c, ctx, c_ctx, ada_w, ada_b, norm1_g, norm2_g, mlp_w1, mlp_w2, final_g, ev_w_in, ev_w_out, s5_lam_re, s5_lam_im, s5_log_dt, s5_b_re, s5_b_im, s5_c_re, s5_c_im, s5_d, s5_glu_w, s5_glu_b, hy_conv_w, hy_conv_b, hy_f_w1, hy_f_b1, hy_f_w2, hy_f_b2, hy_f_w3, hy_f_freq, hy_log_decay, hy_bias, od_w_in, od_w_out, rw_conv_w, rw_w0, rw_w_up, rw_a0, rw_a_up, rw_g_up, rw_k_k, rw_k_a, rw_r_k, rw_ln_g, rw_ln_b, da_lam, da_subln_g):
    b, n_lat, d = x.shape
    n_ctx = ctx.shape[1]
    assert d == D_MODEL and b == SUBLANE
    assert n_lat % TOKEN_TILE == 0 and n_ctx % TOKEN_TILE == 0 and n_lat % n_ctx == 0
    xs = jnp.concatenate([x.astype(F32), ctx.astype(F32)], axis=1)

    c_rows = jnp.zeros((16, d), F32).at[:b].set(c.astype(F32)).at[b].set(c_ctx.astype(F32))
    depth = ada_w.shape[0]
    mods_all = ada_mods(c_rows, ada_w, ada_b).reshape(depth, 16, 6, d)

    rope = rope_tables(n_lat, n_ctx)
    dft = []
    for n in (n_lat, n_ctx):
        fwd, wf = dft_tables(n)
        hi, lo = _split_bf16(fwd)
        dft.append((hi, lo, dft_blocks(hi), wf))

    for l in range(depth):
        keep_ctx = l < depth - 1
        i = l // 2
        mods = mods_all[l]
        if l % 2 == 0:
            s5p = (s5_lam_re[i], s5_lam_im[i], s5_log_dt[i], s5_b_re[i], s5_b_im[i], s5_c_re[i], s5_c_im[i],
                   s5_d[i], s5_glu_w[i], s5_glu_b[i])
            hyp = (hy_conv_w[i], hy_conv_b[i], hy_f_w1[i], hy_f_b1[i], hy_f_w2[i], hy_f_b2[i], hy_f_w3[i],
                   hy_f_freq[i], hy_log_decay[i], hy_bias[i])
            a_m, b_m, b_c = _even_mixers(xs, mods, keep_ctx, n_lat, norm1_g[l], ev_w_in[i], s5p, hyp, dft)
            w_out = ev_w_out[i]
        else:
            rwp = (rw_conv_w[i], rw_w0[i], rw_w_up[i], rw_a0[i], rw_a_up[i], rw_g_up[i], rw_k_k[i],
                   rw_k_a[i], rw_r_k[i], rw_ln_g[i], rw_ln_b[i])
            lam_init = 0.8 - 0.6 * math.exp(-0.3 * l)
            a_m, b_m, b_c = _odd_mixers(xs, mods, keep_ctx, n_lat, lam_init, norm1_g[l], od_w_in[i], rwp,
                                        (da_lam[i], da_subln_g[i]), rope)
            w_out = od_w_out[i]
        n_rows = n_lat + n_ctx if keep_ctx else n_lat
        xs = mix_mlp(xs, a_m, b_m, b_c, w_out.astype(BF16), norm2_g[l], mods, mlp_w1[l].astype(BF16),
                     mlp_w2[l].astype(BF16), final_g, n_lat, n_rows, l % 2 == 0, l == depth - 1)
    return xs
```

```python
import functools
import math

import jax
import jax.numpy as jnp
from jax import lax
from jax.experimental import pallas as pl
from jax.experimental.pallas import tpu as pltpu

F32 = jnp.float32
BF16 = jnp.bfloat16
HIGHEST = lax.Precision.HIGHEST

D_MODEL = 1024
DEPTH = 4
GRID_W = 64
MLP_HIDDEN = 4 * D_MODEL
GROUP_W = D_MODEL // 2
EPS = 1e-6

S5_GROUP = 16
S5_GROUPS = GROUP_W // S5_GROUP
S5_STATE = 64
S5_W = S5_GROUPS * S5_STATE

HY_ORDER = 2
HY_EMB = 33
HY_BANDS = (HY_EMB - 1) // 2
SHORT_K = 3

RW_HEAD = 64
RW_HEADS = GROUP_W // RW_HEAD
RW_LORA_W = 256
RW_LN_EPS = 64e-5
RW_IN = 3 * GROUP_W + RW_LORA_W

DA_HEADS = 4
DA_HEAD = 64
DA_V = 2 * DA_HEAD
DA_SCALE = DA_HEAD ** -0.5
DA_SUBLN_EPS = 1e-5
ROPE_BASE = 10000.0
ROPE_FREQS = DA_HEAD // 4

LANE = 128
SUBLANE = 8
TOKEN_TILE = 256
VMEM_LIMIT = 48 * 1024 * 1024


def _params(*sem):
    return pltpu.CompilerParams(dimension_semantics=sem, vmem_limit_bytes=VMEM_LIMIT)


def _split_bf16(x):
    hi = x.astype(BF16)
    lo = (x - hi.astype(F32)).astype(BF16)
    return hi, lo


def _dot(a, b):
    return jnp.dot(a, b, preferred_element_type=F32)


def _ada_kernel(c_ref, w_ref, b_ref, o_ref):
    c = c_ref[...]
    s = c * jax.nn.sigmoid(c)
    o_ref[0] = jnp.dot(s, w_ref[0], preferred_element_type=F32, precision=HIGHEST) + b_ref[0]


def ada_mods(c_rows, ada_w, ada_b):
    depth, d, n = ada_w.shape
    tn = 1024
    return pl.pallas_call(
        _ada_kernel,
        grid=(depth, n // tn),
        in_specs=[pl.BlockSpec((16, d), lambda l, j: (0, 0)),
                  pl.BlockSpec((1, d, tn), lambda l, j: (l, 0, j)),
                  pl.BlockSpec((1, 1, tn), lambda l, j: (l, 0, j))],
        out_specs=pl.BlockSpec((1, 16, tn), lambda l, j: (l, 0, j)),
        out_shape=jax.ShapeDtypeStruct((depth, 16, n), F32),
        compiler_params=_params("arbitrary", "arbitrary"),
        name="ada_mods",
    )(c_rows, ada_w, ada_b.reshape(depth, 1, n))


def _norm_mod(x, g, ml, mc, is_ctx, k):
    ms = jnp.mean(x * x, axis=-1, keepdims=True)
    xn = x * lax.rsqrt(ms + EPS) * g
    m = jnp.where(is_ctx, mc, ml)
    return xn * (1.0 + m[k + 1:k + 2]) + m[k:k + 1]


def _inproj_kernel(x_ref, g_ref, ml_ref, mc_ref, w_ref, *o_refs, n_lat_tiles, n_tmaj):
    i = pl.program_id(1)
    h = _norm_mod(x_ref[0], g_ref[...], ml_ref[0], mc_ref[0], i >= n_lat_tiles, 0)
    r = _dot(h.astype(BF16), w_ref[...])
    if n_tmaj:
        o_refs[0][...] = r[:, :n_tmaj]
        o_refs[1][0] = r[:, n_tmaj:]
    else:
        o_refs[0][0] = r


def in_proj(x, g, mods, w, n_lat, n_tmaj):
    b, t, d = x.shape
    n = w.shape[1]
    tm = TOKEN_TILE
    if n_tmaj:
        out_shape = [jax.ShapeDtypeStruct((t, b * n_tmaj), F32), jax.ShapeDtypeStruct((b, t, n - n_tmaj), F32)]
        out_specs = [pl.BlockSpec((tm, n_tmaj), lambda bb, i: (i, bb)),
                     pl.BlockSpec((1, tm, n - n_tmaj), lambda bb, i: (bb, i, 0))]
    else:
        out_shape = [jax.ShapeDtypeStruct((b, t, n), F32)]
        out_specs = [pl.BlockSpec((1, tm, n), lambda bb, i: (bb, i, 0))]
    return pl.pallas_call(
        functools.partial(_inproj_kernel, n_lat_tiles=n_lat // tm, n_tmaj=n_tmaj),
        grid=(b, t // tm),
        in_specs=[pl.BlockSpec((1, tm, d), lambda bb, i: (bb, i, 0)),
                  pl.BlockSpec((1, d), lambda bb, i: (0, 0)),
                  pl.BlockSpec((1, 6, d), lambda bb, i: (bb, 0, 0)),
                  pl.BlockSpec((1, 6, d), lambda bb, i: (b, 0, 0)),
                  pl.BlockSpec((d, n), lambda bb, i: (0, 0), pipeline_mode=pl.Buffered(1))],
        out_specs=out_specs,
        out_shape=out_shape,
        compiler_params=_params("arbitrary", "arbitrary"),
        name="in_proj",
    )(x, g.reshape(1, d), mods, mods, w)


MLP_CHUNK = 1024


def _mix_mlp_kernel(x_ref, a_ref, b_ref, bc_ref, wa_ref, wb_ref, g_ref, ml_ref, mc_ref, w1_ref, w2_ref, fg_ref,
                    o_ref, acc_ref, *, n_lat_tiles, a_tmaj, b_split, final):
    i = pl.program_id(1)
    is_ctx = i >= n_lat_tiles
    m = jnp.where(is_ctx, mc_ref[0], ml_ref[0])
    a = a_ref[...] if a_tmaj else a_ref[0]
    bmix = jnp.where(is_ctx, bc_ref[0], b_ref[0]) if b_split else b_ref[0]
    mix = _dot(a.astype(BF16), wa_ref[...]) + _dot(bmix.astype(BF16), wb_ref[...])
    x = x_ref[0] + m[2:3] * mix
    h = _norm_mod(x, g_ref[...], ml_ref[0], mc_ref[0], is_ctx, 3).astype(BF16)
    for k in range(w1_ref.shape[1] // MLP_CHUNK):
        ks = slice(k * MLP_CHUNK, (k + 1) * MLP_CHUNK)
        hid = jnp.maximum(_dot(h, w1_ref[:, ks]), 0.0)
        part = _dot((hid * hid).astype(BF16), w2_ref[ks, :])
        if k == 0:
            acc_ref[...] = part
        else:
            acc_ref[...] += part
    y = x + m[5:6] * acc_ref[...]
    if final:
        ms = jnp.mean(y * y, axis=-1, keepdims=True)
        y = y * lax.rsqrt(ms + EPS) * fg_ref[...]
    o_ref[0] = y


def mix_mlp(x, a, bm, bm_ctx, w_out, g, mods, w1, w2, final_g, n_lat, n_rows, a_tmaj, final):
    b, t, d = x.shape
    wd = bm.shape[-1]
    hid = w1.shape[1]
    tm = TOKEN_TILE
    nl = n_lat // tm
    once = pl.Buffered(1)
    if a_tmaj:
        a_spec = pl.BlockSpec((tm, wd), lambda bb, i: (i, bb))
    else:
        a_spec = pl.BlockSpec((1, tm, wd), lambda bb, i: (bb, i, 0))
    b_split = bm_ctx is not None
    if b_split:
        b_spec = pl.BlockSpec((1, tm, wd), lambda bb, i: (bb, jnp.minimum(i, nl - 1), 0))
        bc_spec = pl.BlockSpec((1, tm, wd), lambda bb, i: (bb, jnp.maximum(i - nl, 0), 0))
    else:
        b_spec = pl.BlockSpec((1, tm, wd), lambda bb, i: (bb, i, 0))
        bc_spec = pl.BlockSpec((1, tm, wd), lambda bb, i: (0, 0, 0))
        bm_ctx = bm
    return pl.pallas_call(
        functools.partial(_mix_mlp_kernel, n_lat_tiles=nl, a_tmaj=a_tmaj, b_split=b_split, final=final),
        grid=(b, n_rows // tm),
        in_specs=[pl.BlockSpec((1, tm, d), lambda bb, i: (bb, i, 0)),
                  a_spec,
                  b_spec,
                  bc_spec,
                  pl.BlockSpec((wd, d), lambda bb, i: (0, 0), pipeline_mode=once),
                  pl.BlockSpec((wd, d), lambda bb, i: (1, 0), pipeline_mode=once),
                  pl.BlockSpec((1, d), lambda bb, i: (0, 0)),
                  pl.BlockSpec((1, 6, d), lambda bb, i: (bb, 0, 0)),
                  pl.BlockSpec((1, 6, d), lambda bb, i: (b, 0, 0)),
                  pl.BlockSpec((d, hid), lambda bb, i: (0, 0), pipeline_mode=once),
                  pl.BlockSpec((hid, d), lambda bb, i: (0, 0), pipeline_mode=once),
                  pl.BlockSpec((1, d), lambda bb, i: (0, 0))],
        out_specs=pl.BlockSpec((1, tm, d), lambda bb, i: (bb, i, 0)),
        out_shape=jax.ShapeDtypeStruct((b, n_rows, d), F32),
        scratch_shapes=[pltpu.VMEM((tm, d), F32)],
        compiler_params=_params("arbitrary", "arbitrary"),
        name="mix_mlp",
    )(x, a, bm, bm_ctx, w_out, w_out, g.reshape(1, d), mods, mods, w1, w2, final_g.reshape(1, d))


def _conv3(x, w, first, last):
    rows = x.shape[0]
    xm = jnp.where(first, 0.0, pltpu.roll(x, 1, 0))
    xp = jnp.where(last, 0.0, pltpu.roll(x, rows - 1, 0))
    return xm * w[0:1] + x * w[1:2] + xp * w[2:3]


def _shortconv_kernel(x_ref, w_ref, o_ref, *, n_lat):
    x = x_ref[0]
    t = x.shape[0]
    row = lax.broadcasted_iota(jnp.int32, (t, 1), 0)
    first = (row == 0) | (row == n_lat)
    last = (row == n_lat - 1) | (row == t - 1)
    o_ref[0, 0] = _conv3(x, w_ref[...], first, last)


def short_conv(p, w, n_lat):
    b, t, _ = p.shape
    c = w.shape[1]
    cb = GROUP_W
    return pl.pallas_call(
        functools.partial(_shortconv_kernel, n_lat=n_lat),
        grid=(b, c // cb),
        in_specs=[pl.BlockSpec((1, t, cb), lambda bb, j: (bb, 0, j)),
                  pl.BlockSpec((SHORT_K, cb), lambda bb, j: (0, j))],
        out_specs=pl.BlockSpec((1, 1, t, cb), lambda bb, j: (j, bb, 0, 0)),
        out_shape=jax.ShapeDtypeStruct((c // cb, b, t, cb), F32),
        compiler_params=_params("arbitrary", "arbitrary"),
        name="short_conv",
    )(p, w)


S5_KB = LANE
S5_NB = GROUP_W // S5_KB
S5_SB = S5_W // S5_NB


def _s5_kernel(*refs, tc, nb, reverse, finish):
    if finish:
        (u_ref, wbr_ref, wbi_ref, lr_ref, li_ref, wcr_ref, wci_ref, yf_ref, d_ref, gw_ref, gb_ref,
         o_ref, hr_s, hi_s, sr_s, si_s) = refs
    else:
        (u_ref, wbr_ref, wbi_ref, lr_ref, li_ref, wcr_ref, wci_ref,
         o_ref, hr_s, hi_s, sr_s, si_s) = refs

    @pl.when(pl.program_id(0) == 0)
    def _():
        sr_s[...] = jnp.zeros_like(sr_s)
        si_s[...] = jnp.zeros_like(si_s)

    u = u_ref[...]
    ub = u.astype(BF16)
    for j in range(S5_NB):
        uj = ub[:, j * S5_KB:(j + 1) * S5_KB]
        hr_s[:, j * S5_SB:(j + 1) * S5_SB] = _dot(uj, wbr_ref[j])
        hi_s[:, j * S5_SB:(j + 1) * S5_SB] = _dot(uj, wbi_ref[j])

    for s in range(S5_NB):
        sl = slice(s * S5_SB, (s + 1) * S5_SB)
        lr = jnp.broadcast_to(lr_ref[:, sl], (nb, S5_SB))
        li = jnp.broadcast_to(li_ref[:, sl], (nb, S5_SB))

        def body(k, carry, sl=sl, lr=lr, li=li):
            hr, hi = carry
            t = (tc - 1 - k) if reverse else k
            r0 = pl.multiple_of(t * nb, nb)
            nr = lr * hr - li * hi + hr_s[pl.ds(r0, nb), sl]
            ni = lr * hi + li * hr + hi_s[pl.ds(r0, nb), sl]
            hr_s[pl.ds(r0, nb), sl] = nr
            hi_s[pl.ds(r0, nb), sl] = ni
            return nr, ni

        hr, hi = lax.fori_loop(0, tc, body, (sr_s[:, sl], si_s[:, sl]), unroll=4)
        sr_s[:, sl] = hr
        si_s[:, sl] = hi

    for j in range(S5_NB):
        sl = slice(j * S5_SB, (j + 1) * S5_SB)
        co = slice(j * S5_KB, (j + 1) * S5_KB)
        y = _dot(hr_s[:, sl].astype(BF16), wcr_ref[j]) + _dot(hi_s[:, sl].astype(BF16), wci_ref[j])
        if finish:
            o_ref[:, co] = y + yf_ref[:, co] + u[:, co] * d_ref[:, co]
        else:
            o_ref[:, co] = y

    if finish:
        y = o_ref[...]
        y = 0.5 * y * (1.0 + lax.erf(y * (2.0 ** -0.5)))
        z = _dot(y.astype(BF16), gw_ref[...]) + gb_ref[...]
        o_ref[...] = y * jax.nn.sigmoid(z)


def s5_scan_call(u2, tabs, n_lat_steps, n_steps, nb, reverse, extra):
    tc = 32
    rows = tc * nb
    n_chunks = n_steps // tc
    n_lat_chunks = n_lat_steps // tc
    n_ctx_chunks = n_chunks - n_lat_chunks
    if reverse:
        def cmap(c):
            return (n_chunks - 1 - c, 0)
    else:
        def cmap(c):
            return (jnp.where(c < n_ctx_chunks, n_lat_chunks + c, c - n_ctx_chunks), 0)
    wbr, wbi, lr, li, wcr, wci = tabs
    full3 = lambda a: pl.BlockSpec(a.shape, lambda c: (0, 0, 0))
    full2 = lambda a: pl.BlockSpec(a.shape, lambda c: (0, 0))
    in_specs = [pl.BlockSpec((rows, GROUP_W), cmap), full3(wbr), full3(wbi), full2(lr), full2(li),
                full3(wcr), full3(wci)]
    args = [u2, wbr, wbi, lr, li, wcr, wci]
    finish = extra is not None
    if finish:
        yf, dsk, gw, gb = extra
        in_specs += [pl.BlockSpec((rows, GROUP_W), cmap), full2(dsk), full2(gw), full2(gb)]
        args += [yf, dsk, gw, gb]
    return pl.pallas_call(
        functools.partial(_s5_kernel, tc=tc, nb=nb, reverse=reverse, finish=finish),
        grid=(n_chunks,),
        in_specs=in_specs,
        out_specs=pl.BlockSpec((rows, GROUP_W), cmap),
        out_shape=jax.ShapeDtypeStruct(u2.shape, F32),
        scratch_shapes=[pltpu.VMEM((rows, S5_W), F32), pltpu.VMEM((rows, S5_W), F32),
                        pltpu.VMEM((nb, S5_W), F32), pltpu.VMEM((nb, S5_W), F32)],
        compiler_params=_params("arbitrary"),
        name="s5_bwd_finish" if finish else "s5_fwd",
    )(*args)


def _block_diag(x):
    nblk, g, r, c = x.shape
    eye = jnp.eye(g, dtype=x.dtype)
    return jnp.einsum('jgrc,gh->jgrhc', x, eye).reshape(nblk, g * r, g * c)


def s5_tables(lam_re, lam_im, log_dt, b_re, b_im, c_re, c_im):
    lam_re = jnp.minimum(lam_re.astype(F32), -1e-4)
    lam_im = lam_im.astype(F32)
    dt = jnp.exp(log_dt.astype(F32))[:, None]
    mag = jnp.exp(lam_re * dt)
    lb_re = mag * jnp.cos(lam_im * dt)
    lb_im = mag * jnp.sin(lam_im * dt)
    den = lam_re * lam_re + lam_im * lam_im
    f_re = ((lb_re - 1.0) * lam_re + lb_im * lam_im) / den
    f_im = (lb_im * lam_re - (lb_re - 1.0) * lam_im) / den
    b_re = b_re.astype(F32)
    b_im = b_im.astype(F32)
    bb_re = f_re[..., None] * b_re - f_im[..., None] * b_im
    bb_im = f_re[..., None] * b_im + f_im[..., None] * b_re
    gpb = S5_KB // S5_GROUP
    to_b = lambda x: _block_diag(
        jnp.swapaxes(x, 1, 2).reshape(S5_NB, gpb, S5_GROUP, S5_STATE)).astype(BF16)
    to_c = lambda x: _block_diag(
        jnp.swapaxes(x.astype(F32), 1, 2).reshape(S5_NB, gpb, S5_STATE, S5_GROUP)).astype(BF16)
    return (to_b(bb_re), to_b(bb_im), lb_re.reshape(1, S5_W), lb_im.reshape(1, S5_W),
            to_c(c_re), to_c(-c_im.astype(F32)))


def dft_tables(n):
    nn = 2 * n
    f = jnp.arange(n, dtype=jnp.int32)[:, None]
    t = jnp.arange(n, dtype=jnp.int32)[None, :]
    q = 1 << (int(math.log2(n)) // 2)

    def factor(tt):
        ang = ((f * tt[None, :]) % nn).astype(F32) * (2.0 * math.pi / nn)
        return jnp.cos(ang), jnp.sin(ang)

    c1, s1 = factor(jnp.arange(n // q, dtype=jnp.int32) * q)
    c0, s0 = factor(jnp.arange(q, dtype=jnp.int32))
    cos = (c1[:, :, None] * c0[:, None, :] - s1[:, :, None] * s0[:, None, :]).reshape(n, n)
    sin = (s1[:, :, None] * c0[:, None, :] + c1[:, :, None] * s0[:, None, :]).reshape(n, n)
    nyq = jnp.where(t % 2 == 0, 1.0, -1.0).astype(F32)
    msin = jnp.where(f == 0, nyq, -sin)
    fwd = jnp.concatenate([cos, msin], axis=0)
    wf = jnp.where(jnp.arange(n) == 0, 1.0 / nn, 2.0 / nn).astype(F32)
    return fwd, wf


def _filtdft_kernel(fh_ref, fl_ref, hh_ref, hl_ref, o_ref):
    fh = fh_ref[...]
    o_ref[...] = _dot(fh, hh_ref[0]) + _dot(fh, hl_ref[0]) + _dot(fl_ref[...], hh_ref[0])


def filter_dft(fwd_hi, fwd_lo, h_sum, h_dif):
    n2, n = fwd_hi.shape
    c = h_sum.shape[1]
    tr = min(256, n)
    tcn = 512
    nrb = n // tr
    hh, hl = _split_bf16(jnp.stack([h_sum, h_dif]))
    frow = lambda i, j: (jnp.where(i == 2 * nrb, nrb, i), 0)
    hsel = lambda i, j: (jnp.where((i >= nrb) & (i < 2 * nrb), 1, 0), 0, j)
    return pl.pallas_call(
        _filtdft_kernel,
        grid=(2 * nrb + 1, c // tcn),
        in_specs=[pl.BlockSpec((tr, n), frow),
                  pl.BlockSpec((tr, n), frow),
                  pl.BlockSpec((1, n, tcn), hsel),
                  pl.BlockSpec((1, n, tcn), hsel)],
        out_specs=pl.BlockSpec((tr, tcn), lambda i, j: (i, j)),
        out_shape=jax.ShapeDtypeStruct((n2 + tr, c), F32),
        compiler_params=_params("arbitrary", "arbitrary"),
        name="filter_dft",
    )(fwd_hi, fwd_lo, hh, hl)


def hyena_filter_response(n, fwd_hi, fwd_lo, wf, f_w1, f_b1, f_w2, f_b2, f_w3, f_freq, log_decay):
    t = jnp.linspace(0.0, 1.0, n, dtype=F32)[:, None]
    w = 2.0 * math.pi * jnp.arange(n, dtype=F32)[:, None] / n
    bands = jnp.linspace(1e-4, HY_BANDS - 1, HY_BANDS, dtype=F32)[None, :]
    z = jnp.concatenate([t, jnp.cos(bands * w), -jnp.sin(bands * w)], axis=-1)
    freq = f_freq.astype(F32)
    hid = jnp.sin(freq[0] * (jnp.dot(z, f_w1.astype(F32), precision=HIGHEST) + f_b1.astype(F32)))
    hid = jnp.sin(freq[1] * (jnp.dot(hid, f_w2.astype(F32), precision=HIGHEST) + f_b2.astype(F32)))
    h = jnp.dot(hid, f_w3.astype(F32), precision=HIGHEST).reshape(n, 2, HY_ORDER, GROUP_W)
    h = h * jnp.exp(-t[:, :, None, None] * jnp.exp(log_decay.astype(F32)))
    cw = HY_ORDER * GROUP_W
    h_fwd = h[:, 0].reshape(n, cw)
    h_bwd = h[:, 1].reshape(n, cw).at[0].set(0.0)
    resp = filter_dft(fwd_hi, fwd_lo, h_fwd + h_bwd, h_fwd - h_bwd)
    k_re = resp[:n]
    first = (jnp.arange(n) == 0)[:, None]
    k_im = jnp.where(first, resp[2 * n:2 * n + 1], resp[n:2 * n])
    k_re = (k_re * wf[:, None]).reshape(n, HY_ORDER, GROUP_W).transpose(1, 0, 2)
    k_im = (k_im * wf[:, None]).reshape(n, HY_ORDER, GROUP_W).transpose(1, 0, 2)
    return k_re, k_im


def _hyena_kernel(z0_ref, x1_ref, x2_ref, cw_ref, cb_ref, fw_ref, inv_ref, kr_ref, ki_ref, bias_ref,
                  o_ref, zb_s, zf_s, acc_s):
    o = pl.program_id(1)
    f = pl.program_id(2)
    nf = pl.num_programs(2)
    c = GROUP_W

    def stream(ref, s):
        x = ref[0]
        row = lax.broadcasted_iota(jnp.int32, (x.shape[0], 1), 0)
        cols = slice(s * c, (s + 1) * c)
        return _conv3(x, cw_ref[:, cols], row == 0, row == x.shape[0] - 1) + cb_ref[:, cols]

    @pl.when((o == 0) & (f == 0))
    def _():
        z0 = stream(z0_ref, 0)
        zf_s[...] = z0
        zb_s[...] = z0.astype(BF16)

    @pl.when(f == 0)
    def _():
        acc_s[...] = jnp.zeros_like(acc_s)

    xf = _dot(fw_ref[0], zb_s[...])
    fb = xf.shape[0] // 2
    xr = xf[:fb]
    xi = xf[fb:]
    kr = kr_ref[0]
    ki = ki_ref[0]
    row = lax.broadcasted_iota(jnp.int32, (fb, 1), 0)
    packed = (row == 0) & (f == 0)
    yr = jnp.where(packed, xr * kr, xr * kr - xi * ki)
    yi = jnp.where(packed, xi * ki, xr * ki + xi * kr)
    yf = jnp.concatenate([yr, yi], axis=0).astype(BF16)
    acc_s[...] += _dot(inv_ref[0], yf)

    @pl.when(f == nf - 1)
    def _():
        bias = bias_ref[...]

        @pl.when(o == 0)
        def _():
            z1 = stream(x1_ref, 1) * (acc_s[...] + zf_s[...] * bias[0:1])
            zf_s[...] = z1
            zb_s[...] = z1.astype(BF16)

        @pl.when(o == 1)
        def _():
            o_ref[0] = stream(x2_ref, 2) * (acc_s[...] + zf_s[...] * bias[1:2])


HY_FREQ_BLOCK = 256


def dft_blocks(fwd_hi):
    n = fwd_hi.shape[1]
    fb = min(HY_FREQ_BLOCK, n)
    blocks = fwd_hi.reshape(2, n // fb, fb, n).transpose(1, 0, 2, 3).reshape(n // fb, 2 * fb, n)
    return blocks, blocks.transpose(0, 2, 1)


def hyena_call(p, n, row_blk, conv_w, conv_b, fwd_blk, inv_blk, k_re, k_im, bias):
    b = p.shape[0]
    c = GROUP_W
    nf, fb2, _ = fwd_blk.shape
    fb = fb2 // 2
    once = pl.Buffered(1)
    in_specs = [pl.BlockSpec((1, n, c), lambda bb, o, f: (bb, row_blk, 0), pipeline_mode=once),
                pl.BlockSpec((1, n, c), lambda bb, o, f: (bb, row_blk, 1), pipeline_mode=once),
                pl.BlockSpec((1, n, c), lambda bb, o, f: (bb, row_blk, 2), pipeline_mode=once),
                pl.BlockSpec((SHORT_K, 3 * c), lambda bb, o, f: (0, 0)),
                pl.BlockSpec((1, 3 * c), lambda bb, o, f: (0, 0)),
                pl.BlockSpec((1, fb2, n), lambda bb, o, f: (f, 0, 0)),
                pl.BlockSpec((1, n, fb2), lambda bb, o, f: (f, 0, 0)),
                pl.BlockSpec((1, fb, c), lambda bb, o, f: (o, f, 0)),
                pl.BlockSpec((1, fb, c), lambda bb, o, f: (o, f, 0)),
                pl.BlockSpec((HY_ORDER, c), lambda bb, o, f: (0, 0))]
    return pl.pallas_call(
        _hyena_kernel,
        grid=(b, HY_ORDER, nf),
        in_specs=in_specs,
        out_specs=pl.BlockSpec((1, n, c), lambda bb, o, f: (bb, 0, 0)),
        out_shape=jax.ShapeDtypeStruct((b, n, c), F32),
        scratch_shapes=[pltpu.VMEM((n, c), BF16), pltpu.VMEM((n, c), F32), pltpu.VMEM((n, c), F32)],
        compiler_params=_params("arbitrary", "arbitrary", "arbitrary"),
        name="hyena",
    )(p, p, p, conv_w, conv_b.reshape(1, 3 * c), fwd_blk, inv_blk, k_re, k_im, bias)


def _rwkv_lora_kernel(p_ref, ww_ref, wa_ref, wg_ref, w0_ref, a0_ref, wp_ref, ap_ref, g_ref):
    x = p_ref[0]
    wa_in = x[:, :LANE]
    th = jnp.tanh(wa_in).astype(BF16)
    lin = wa_in.astype(BF16)
    w0 = w0_ref[...]
    a0 = a0_ref[...]
    for d in range(2):
        wl = -jax.nn.softplus(-(w0[d:d + 1] + _dot(th, ww_ref[d]))) - 0.5
        wp_ref[d, 0] = jnp.exp(-jnp.exp(wl))
        ap_ref[d, 0] = jax.nn.sigmoid(a0[d:d + 1] + _dot(lin, wa_ref[d]))
    g_ref[0] = _dot(jax.nn.sigmoid(x[:, LANE:]).astype(BF16), wg_ref[...])


def rwkv_lora(p, w_up, a_up, g_up, w0, a0):
    b, t, _ = p.shape
    c = GROUP_W
    tm = TOKEN_TILE
    zeros = jnp.zeros((2, 64, c), F32)
    ww = jnp.concatenate([w_up.astype(F32), zeros], axis=1).astype(BF16)
    wa = jnp.concatenate([zeros, a_up.astype(F32)], axis=1).astype(BF16)
    both = jax.ShapeDtypeStruct((2, b, t, c), F32)
    both_spec = pl.BlockSpec((2, 1, tm, c), lambda bb, i: (0, bb, i, 0))
    return pl.pallas_call(
        _rwkv_lora_kernel,
        grid=(b, t // tm),
        in_specs=[pl.BlockSpec((1, tm, RW_LORA_W), lambda bb, i: (bb, i, 3 * c // RW_LORA_W)),
                  pl.BlockSpec((2, LANE, c), lambda bb, i: (0, 0, 0)),
                  pl.BlockSpec((2, LANE, c), lambda bb, i: (0, 0, 0)),
                  pl.BlockSpec((LANE, c), lambda bb, i: (0, 0)),
                  pl.BlockSpec((2, c), lambda bb, i: (0, 0)),
                  pl.BlockSpec((2, c), lambda bb, i: (0, 0))],
        out_specs=[both_spec, both_spec, pl.BlockSpec((1, tm, c), lambda bb, i: (bb, i, 0))],
        out_shape=[both, both, jax.ShapeDtypeStruct((b, t, c), F32)],
        compiler_params=_params("arbitrary", "arbitrary"),
        name="rwkv_lora",
    )(p, ww, wa, g_up.astype(BF16), w0, a0)


SCAN_UNROLL = 64


def _rwkv_scan_kernel(wpf_ref, wpb_ref, apf_ref, apb_ref, kf_ref, kb_ref, vf_ref, vb_ref, rf_ref, rb_ref,
                      kkc_ref, kac_ref, yf_ref, yb_ref, s_ref, w_s, kk_s, b_s, ke_s, r_s, v_s, *, tb):
    n = RW_HEAD

    @pl.when(pl.program_id(0) == 0)
    def _():
        s_ref[...] = jnp.zeros_like(s_ref)

    def both(f_ref, b_ref, t):
        return jnp.concatenate([f_ref[0, t], b_ref[0, tb - 1 - t]], axis=-1)

    def prepare(t, carry):
        k = both(kf_ref, kb_ref, t)
        a = both(apf_ref, apb_ref, t)
        kk = k * kkc_ref[...]
        nrm = jnp.sqrt(jnp.sum(kk * kk, axis=0, keepdims=True))
        kk = kk / jnp.maximum(nrm, 1e-12)
        w_s[t] = both(wpf_ref, wpb_ref, t)
        kk_s[t] = kk
        b_s[t] = kk * a
        ke_s[t] = k * (1.0 + (a - 1.0) * kac_ref[...])
        r_s[t] = both(rf_ref, rb_ref, t)
        v_s[t] = both(vf_ref, vb_ref, t)
        return carry

    lax.fori_loop(0, tb, prepare, 0, unroll=32)
    sa0 = jnp.zeros((n, LANE), F32)
    for j in range(n):
        sa0 = sa0 + s_ref[j] * kk_s[0, j:j + 1, :]

    def step(t, sa):
        nxt = jnp.minimum(t + 1, tb - 1)
        v = v_s[t]

        def keys(j, carry):
            y, sa_next = carry
            row = pl.ds(j, 1)
            s = s_ref[j] * w_s[t, row, :] - sa * b_s[t, row, :] + v * ke_s[t, row, :]
            s_ref[j] = s
            return y + s * r_s[t, row, :], sa_next + s * kk_s[nxt, row, :]

        zero = jnp.zeros((n, LANE), F32)
        y, sa_next = lax.fori_loop(0, n, keys, (zero, zero), unroll=SCAN_UNROLL)
        yf_ref[t] = y
        yb_ref[tb - 1 - t] = y
        return sa_next

    lax.fori_loop(0, tb, step, sa0)


def rwkv_scan(w, a, rkv, kkc, kac, n_lat):
    t = rkv.shape[1]
    tb = 32
    n = RW_HEAD
    half = LANE // 2
    n_blk = t // tb
    n_lat_blk = n_lat // tb
    n_ctx_blk = n_blk - n_lat_blk

    def fblk(c):
        return jnp.where(c < n_ctx_blk, n_lat_blk + c, c - n_ctx_blk)

    def rblk(c):
        return n_blk - 1 - c

    def pair(qf, qb):
        return [pl.BlockSpec((1, tb, n, half), lambda c: (qf, fblk(c), 0, 0)),
                pl.BlockSpec((1, tb, n, half), lambda c: (qb, rblk(c), 0, 0))]

    cst = pl.BlockSpec((n, LANE), lambda c: (0, 0))
    buf = pltpu.VMEM((tb, n, LANE), F32)
    out = jax.ShapeDtypeStruct((t, n, LANE), F32)
    return pl.pallas_call(
        functools.partial(_rwkv_scan_kernel, tb=tb),
        grid=(n_blk,),
        in_specs=pair(0, 1) + pair(0, 1) + pair(1, 1) + pair(2, 2) + pair(0, 0) + [cst] * 2,
        out_specs=[pl.BlockSpec((tb, n, LANE), lambda c: (fblk(c), 0, 0)),
                   pl.BlockSpec((tb, n, LANE), lambda c: (rblk(c), 0, 0))],
        out_shape=[out, out],
        scratch_shapes=[pltpu.VMEM((n, n, LANE), F32), buf, buf, buf, buf, buf, buf],
        compiler_params=_params("arbitrary"),
        name="rwkv_scan",
    )(w, w, a, a, rkv, rkv, rkv, rkv, rkv, rkv, kkc, kac)


def _rwkv_post_kernel(y_ref, r_ref, k_ref, v_ref, ap_ref, g_ref, m_ref, ka_ref, rk_ref, lg_ref, lb_ref,
                      o_ref):
    m = m_ref[...]

    def head_mean(x):
        hi, lo = _split_bf16(x)
        return _dot(hi, m) + _dot(lo, m)

    y = y_ref[0]
    d = y - head_mean(y)
    var = head_mean(d * d)
    yn = d * lax.rsqrt(var + RW_LN_EPS) * lg_ref[...] + lb_ref[...]
    a_sum = ap_ref[0, 0] + ap_ref[1, 0]
    k_sum = k_ref[0, 0] * (2.0 + (a_sum - 2.0) * ka_ref[...])
    bonus = head_mean(r_ref[0, 0] * k_sum * rk_ref[...]) * float(RW_HEAD) * v_ref[0, 0]
    o_ref[0] = (yn + bonus) * g_ref[0]


def rwkv_post(y, rkv, ap, gate, k_a, r_k, ln_g, ln_b, n_rows):
    b, t, c = y.shape
    tm = TOKEN_TILE
    hm = jnp.kron(jnp.eye(RW_HEADS, dtype=F32), jnp.full((RW_HEAD, RW_HEAD), 1.0 / RW_HEAD, F32)).astype(BF16)
    tok = pl.BlockSpec((1, tm, c), lambda bb, i: (bb, i, 0))
    vec = pl.BlockSpec((1, c), lambda bb, i: (0, 0))
    rkv_spec = lambda q: pl.BlockSpec((1, 1, tm, c), lambda bb, i: (q, bb, i, 0))
    return pl.pallas_call(
        _rwkv_post_kernel,
        grid=(b, n_rows // tm),
        in_specs=[tok, rkv_spec(0), rkv_spec(1), rkv_spec(2),
                  pl.BlockSpec((2, 1, tm, c), lambda bb, i: (0, bb, i, 0)), tok,
                  pl.BlockSpec((c, c), lambda bb, i: (0, 0)), vec, vec, vec, vec],
        out_specs=tok,
        out_shape=jax.ShapeDtypeStruct((b, n_rows, c), F32),
        compiler_params=_params("arbitrary", "arbitrary"),
        name="rwkv_post",
    )(y, rkv, rkv, rkv, ap, gate, hm, k_a.reshape(1, c), r_k.reshape(1, c),
      ln_g.reshape(1, c), ln_b.reshape(1, c))


def _to_scan_layout(a):
    s, b, t, _ = a.shape
    x = a.reshape(s, b, t, RW_HEADS, RW_HEAD).transpose(0, 2, 4, 1, 3)
    return x.reshape(s, t, RW_HEAD, b * RW_HEADS)


def _from_scan_layout(yf, yb, b):
    t = yf.shape[0]
    half = b * RW_HEADS
    y = yf[..., :half] + yb[..., half:]
    return y.reshape(t, RW_HEAD, b, RW_HEADS).transpose(2, 0, 3, 1).reshape(b, t, GROUP_W)


def _chain_const(x, b):
    return jnp.tile(x.astype(F32).reshape(RW_HEADS, RW_HEAD).T, (1, 2 * b))


def _rope(x, cos, sin):
    lane = lax.broadcasted_iota(jnp.int32, (1, LANE), 1)
    first = (lane % (2 * ROPE_FREQS)) < ROPE_FREQS
    partner = jnp.where(first, pltpu.roll(x, LANE - ROPE_FREQS, 1), pltpu.roll(x, ROPE_FREQS, 1))
    return x * cos + partner * sin


def _attn_tile(q, k, v, lam):
    lane = lax.broadcasted_iota(jnp.int32, (1, LANE), 1)
    m0 = lane < DA_HEAD
    q0 = jnp.where(m0, q, 0.0).astype(BF16)
    q1 = jnp.where(m0, 0.0, q).astype(BF16)
    dn = (((1,), (1,)), ((), ()))
    s0 = lax.dot_general(q0, k, dn, preferred_element_type=F32)
    s1 = lax.dot_general(q1, k, dn, preferred_element_type=F32)
    p0 = jnp.exp2(s0 - jnp.max(s0, axis=-1, keepdims=True))
    p1 = jnp.exp2(s1 - jnp.max(s1, axis=-1, keepdims=True))
    c0 = (1.0 / jnp.sum(p0, axis=-1, keepdims=True)).astype(BF16)
    c1 = (lam / jnp.sum(p1, axis=-1, keepdims=True)).astype(BF16)
    return _dot(p0.astype(BF16) * c0 - p1.astype(BF16) * c1, v)


def _attn_kernel(q_ref, k_ref, v_ref, cq_ref, sq_ref, ck_ref, sk_ref, lp_ref, g_ref, o_ref,
                 kr_s, vb_s, *, n_lat, n_lat_tiles, lam_init):
    i = pl.program_id(2)

    @pl.when(i == 0)
    def _():
        kr_s[...] = _rope(k_ref[0], ck_ref[...], sk_ref[...]).astype(BF16)
        vb_s[...] = v_ref[0].astype(BF16)

    lp = lp_ref[...]
    lam = (jnp.exp(jnp.sum(lp[0:1] * lp[1:2], axis=-1, keepdims=True))
           - jnp.exp(jnp.sum(lp[2:3] * lp[3:4], axis=-1, keepdims=True)) + lam_init)
    q = _rope(q_ref[0], cq_ref[...], sq_ref[...]) * (DA_SCALE * math.log2(math.e))

    def finish(o):
        on = o * lax.rsqrt(jnp.mean(o * o, axis=-1, keepdims=True) + DA_SUBLN_EPS)
        o_ref[0] = on * g_ref[...] * (1.0 - lam_init)

    @pl.when(i < n_lat_tiles)
    def _():
        finish(_attn_tile(q, kr_s[...], vb_s[...], lam))

    @pl.when(i >= n_lat_tiles)
    def _():
        finish(_attn_tile(q, kr_s[n_lat:], vb_s[n_lat:], lam))


def diff_attention(p, col0, cosf, sins, lam_p, subln_g, lam_init, n_lat, n_rows):
    b, t, _ = p.shape
    tq = TOKEN_TILE
    off = col0 // LANE
    hq = DA_HEADS
    return pl.pallas_call(
        functools.partial(_attn_kernel, n_lat=n_lat, n_lat_tiles=n_lat // tq, lam_init=lam_init),
        grid=(b, hq, n_rows // tq),
        in_specs=[pl.BlockSpec((1, tq, LANE), lambda bb, h, i: (bb, i, off + h)),
                  pl.BlockSpec((1, t, LANE), lambda bb, h, i: (bb, 0, off + hq + h)),
                  pl.BlockSpec((1, t, LANE), lambda bb, h, i: (bb, 0, off + 2 * hq + h)),
                  pl.BlockSpec((tq, LANE), lambda bb, h, i: (i, 0)),
                  pl.BlockSpec((tq, LANE), lambda bb, h, i: (i, 0)),
                  pl.BlockSpec((t, LANE), lambda bb, h, i: (0, 0)),
                  pl.BlockSpec((t, LANE), lambda bb, h, i: (0, 0)),
                  pl.BlockSpec((4, DA_HEAD), lambda bb, h, i: (0, 0)),
                  pl.BlockSpec((1, LANE), lambda bb, h, i: (0, 0))],
        out_specs=pl.BlockSpec((1, tq, LANE), lambda bb, h, i: (bb, i, h)),
        out_shape=jax.ShapeDtypeStruct((b, n_rows, hq * DA_V), F32),
        scratch_shapes=[pltpu.VMEM((t, LANE), BF16), pltpu.VMEM((t, LANE), BF16)],
        compiler_params=_params("arbitrary", "arbitrary", "arbitrary"),
        name="diff_attention",
    )(p, p, p, cosf, sins, cosf, sins, lam_p, subln_g.reshape(1, DA_V))


def rope_tables(n_lat, n_ctx):
    rows = n_lat // GRID_W
    row = jnp.repeat(jnp.arange(rows, dtype=F32), GRID_W)
    col = jnp.tile(jnp.arange(GRID_W, dtype=F32), rows)
    inv = ROPE_BASE ** (-jnp.arange(ROPE_FREQS, dtype=F32) / ROPE_FREQS)
    ang = jnp.stack([row[:, None] * inv, col[:, None] * inv], axis=1)
    cos, sin = jnp.cos(ang), jnp.sin(ang)
    cosf = jnp.concatenate([cos, cos], axis=-1).reshape(n_lat, DA_HEAD)
    sins = jnp.concatenate([-sin, sin], axis=-1).reshape(n_lat, DA_HEAD)
    cosf = jnp.concatenate([jnp.tile(cosf, (1, 2)), jnp.ones((n_ctx, LANE), F32)], axis=0)
    sins = jnp.concatenate([jnp.tile(sins, (1, 2)), jnp.zeros((n_ctx, LANE), F32)], axis=0)
    return cosf, sins


def _even_mixers(xs, mods, keep_ctx, n_lat, norm1_g, w_in, s5p, hyp, dft):
    b, t, _ = xs.shape
    n_ctx = t - n_lat
    u_t, p_h = in_proj(xs, norm1_g, mods, w_in.astype(BF16), n_lat, GROUP_W)

    (lam_re, lam_im, log_dt, b_re, b_im, c_re, c_im, d_skip, glu_w, glu_b) = s5p
    u2 = u_t.reshape(t * b, GROUP_W)
    tabs = [s5_tables(lam_re[d], lam_im[d], log_dt[d], b_re[d], b_im[d], c_re[d], c_im[d]) for d in range(2)]
    y_f = s5_scan_call(u2, tabs[0], n_lat, t, b, False, None)
    a_t = s5_scan_call(u2, tabs[1], n_lat, t, b, True,
                       (y_f, d_skip.reshape(1, GROUP_W), glu_w.astype(BF16), glu_b.reshape(1, GROUP_W)))

    (conv_w, conv_b, f_w1, f_b1, f_w2, f_b2, f_w3, f_freq, log_decay, bias) = hyp
    fh_l, fl_l, (fw_l, inv_l), wf_l = dft[0]
    kr, ki = hyena_filter_response(n_lat, fh_l, fl_l, wf_l, f_w1, f_b1, f_w2, f_b2, f_w3, f_freq, log_decay)
    b_l = hyena_call(p_h, n_lat, 0, conv_w, conv_b, fw_l, inv_l, kr, ki, bias)
    b_c = None
    if keep_ctx:
        fh_c, fl_c, (fw_c, inv_c), wf_c = dft[1]
        kr, ki = hyena_filter_response(n_ctx, fh_c, fl_c, wf_c, f_w1, f_b1, f_w2, f_b2, f_w3, f_freq,
                                       log_decay)
        b_c = hyena_call(p_h, n_ctx, n_lat // n_ctx, conv_w, conv_b, fw_c, inv_c, kr, ki, bias)
    return a_t.reshape(t, b * GROUP_W), b_l, b_c


def _odd_mixers(xs, mods, keep_ctx, n_lat, lam_init, norm1_g, w_in, rwp, dap, rope):
    b, t, _ = xs.shape
    n_rows = t if keep_ctx else n_lat
    (p,) = in_proj(xs, norm1_g, mods, w_in.astype(BF16), n_lat, 0)

    (conv_w, w0, w_up, a0, a_up, g_up, k_k, k_a, r_k, ln_g, ln_b) = rwp
    rkv = short_conv(p, conv_w, n_lat)
    decay, a_lr, gate = rwkv_lora(p, w_up, a_up, g_up, w0, a0)
    yf, yb = rwkv_scan(_to_scan_layout(decay), _to_scan_layout(a_lr), _to_scan_layout(rkv),
                       _chain_const(k_k, b), _chain_const(k_a, b), n_lat)
    a_m = rwkv_post(_from_scan_layout(yf, yb, b), rkv, a_lr, gate, k_a, r_k, ln_g, ln_b, n_rows)

    lam_p, subln_g = dap
    b_m = diff_attention(p, RW_IN, rope[0], rope[1], lam_p, subln_g, lam_init, n_lat, n_rows)
    return a_m, b_m, None


def kernel(x, c, ctx, c_ctx, ada_w, ada_b, norm1_g, norm2_g, mlp_w1, mlp_w2, final_g, ev_w_in, ev_w_out, s5_lam_re, s5_lam_im, s5_log_dt, s5_b_re, s5_b_im, s5_c_re, s5_c_im, s5_d, s5_glu_w, s5_glu_b, hy_conv_w, hy_conv_b, hy_f_w1, hy_f_b1, hy_f_w2, hy_f_b2, hy_f_w3, hy_f_freq, hy_log_decay, hy_bias, od_w_in, od_w_out, rw_conv_w, rw_w0, rw_w_up, rw_a0, rw_a_up, rw_g_up, rw_k_k, rw_k_a, rw_r_k, rw_ln_g, rw_ln_b, da_lam, da_subln_g):
    b, n_lat, d = x.shape
    n_ctx = ctx.shape[1]
    assert d == D_MODEL and b == SUBLANE
    assert n_lat % TOKEN_TILE == 0 and n_ctx % TOKEN_TILE == 0 and n_lat % n_ctx == 0
    xs = jnp.concatenate([x.astype(F32), ctx.astype(F32)], axis=1)

    c_rows = jnp.zeros((16, d), F32).at[:b].set(c.astype(F32)).at[b].set(c_ctx.astype(F32))
    depth = ada_w.shape[0]
    mods_all = ada_mods(c_rows, ada_w, ada_b).reshape(depth, 16, 6, d)

    rope = rope_tables(n_lat, n_ctx)
    dft = []
    for n in (n_lat, n_ctx):
        fwd, wf = dft_tables(n)
        hi, lo = _split_bf16(fwd)
        dft.append((hi, lo, dft_blocks(hi), wf))

    for l in range(depth):
        keep_ctx = l < depth - 1
        i = l // 2
        mods = mods_all[l]
        if l % 2 == 0:
            s5p = (s5_lam_re[i], s5_lam_im[i], s5_log_dt[i], s5_b_re[i], s5_b_im[i], s5_c_re[i], s5_c_im[i],
                   s5_d[i], s5_glu_w[i], s5_glu_b[i])
            hyp = (hy_conv_w[i], hy_conv_b[i], hy_f_w1[i], hy_f_b1[i], hy_f_w2[i], hy_f_b2[i], hy_f_w3[i],
                   hy_f_freq[i], hy_log_decay[i], hy_bias[i])
            a_m, b_m, b_c = _even_mixers(xs, mods, keep_ctx, n_lat, norm1_g[l], ev_w_in[i], s5p, hyp, dft)
            w_out = ev_w_out[i]
        else:
            rwp = (rw_conv_w[i], rw_w0[i], rw_w_up[i], rw_a0[i], rw_a_up[i], rw_g_up[i], rw_k_k[i],
                   rw_k_a[i], rw_r_k[i], rw_ln_g[i], rw_ln_b[i])
            lam_init = 0.8 - 0.6 * math.exp(-0.3 * l)
            a_m, b_m, b_c = _odd_mixers(xs, mods, keep_ctx, n_lat, lam_init, norm1_g[l], od_w_in[i], rwp,
                                        (da_lam[i], da_subln_g[i]), rope)
            w_out = od_w_out[i]
        n_rows = n_lat + n_ctx if keep_ctx else n_lat
        xs = mix_mlp(xs, a_m, b_m, b_c, w_out.astype(BF16), norm2_g[l], mods, mlp_w1[l].astype(BF16),
                     mlp_w2[l].astype(BF16), final_g, n_lat, n_rows, l % 2 == 0, l == depth - 1)
    return xs
```
